```python
import math
import jax, jax.numpy as jnp
from jax import lax
import numpy as np

D_MODEL = 1024
BATCH = 16
SEQ = 256
DEPTH = 2
DEC_BATCH = 8
DEC_SEQ = 2048
PAST_LEN = 512

GRID_W = 64
D_MIX = D_MODEL
F_W = D_MIX // 4
F_GROUPS = 4
F_GW = F_W // F_GROUPS
DN_W = D_MIX // 2
DN_HEADS = 4
DK = DN_W // DN_HEADS
DV = DN_W // DN_HEADS
CV_W = D_MIX // 4
SC_W = 5
CV_K = 31
CHUNK = 64
N_DIRS = 2
PROJ_W = F_W + 4 * DN_W + 4 * DN_HEADS + 2 * CV_W
D_FF = 2816
N_EXPERTS = 8
TOP_K = 2
D_FF_EXPERT = 3584
N_DENSE = (DEPTH + 1) // 2
N_MOE = DEPTH // 2
EPS = 1e-6

kernel_name = 'hybrid_fnet_deltanet_conformer_dit_step'


def rmsnorm(x, g):
    x32 = x.astype(jnp.float32)
    y = x32 * lax.rsqrt(jnp.mean(x32 * x32, axis=-1, keepdims=True) + EPS)
    return (y * g.astype(jnp.float32)).astype(x.dtype)


def layernorm(x, g, b):
    x32 = x.astype(jnp.float32)
    mu = jnp.mean(x32, axis=-1, keepdims=True)
    xc = x32 - mu
    var = jnp.mean(xc * xc, axis=-1, keepdims=True)
    return (xc * lax.rsqrt(var + EPS) * g.astype(jnp.float32) + b.astype(jnp.float32)).astype(x.dtype)


def l2norm(x):
    return x * lax.rsqrt(jnp.sum(x * x, axis=-1, keepdims=True) + EPS)


def dwconv(x, w, b):
    pad = (w.shape[0] - 1) // 2
    y = lax.conv_general_dilated(x, w[:, None, :].astype(x.dtype), window_strides=(1,), padding=((pad, pad),),
                                 dimension_numbers=('NWC', 'WIO', 'NWC'), feature_group_count=x.shape[-1])
    return y + b.astype(x.dtype)


def fourier_mix(u):
    b, n, _ = u.shape
    u32 = u.astype(jnp.float32).reshape(b, n, F_GROUPS, F_GW)
    y = jnp.fft.fftn(u32, axes=(1, 3), norm='ortho').real
    return y.reshape(b, n, F_W).astype(u.dtype)


def chunk_gated_delta(q, k, v, g, beta, s0):
    b, h, l, _ = q.shape
    dv = v.shape[-1]
    n = l // CHUNK
    q = q.reshape(b, h, n, CHUNK, -1)
    k = k.reshape(b, h, n, CHUNK, -1)
    v = v.reshape(b, h, n, CHUNK, -1)
    g = jnp.cumsum(g.reshape(b, h, n, CHUNK), axis=-1)
    beta = beta.reshape(b, h, n, CHUNK)
    idx = jnp.arange(CHUNK)
    incl = idx[:, None] >= idx[None, :]
    strict = idx[:, None] > idx[None, :]
    decay = jnp.exp(jnp.where(incl, g[..., :, None] - g[..., None, :], -jnp.inf))
    k_beta = k * beta[..., None]
    a_mat = jnp.where(strict, jnp.einsum('bhnid,bhnjd->bhnij', k_beta, k) * decay, 0.0)
    rhs = jnp.concatenate([v * beta[..., None], k_beta * jnp.exp(g)[..., None]], axis=-1)
    sol = lax.linalg.triangular_solve(a_mat + jnp.eye(CHUNK, dtype=a_mat.dtype), rhs,
                                      left_side=True, lower=True, unit_diagonal=True)
    u, w = sol[..., :dv], sol[..., dv:]
    attn = jnp.einsum('bhnid,bhnjd->bhnij', q, k) * decay
    g_last = g[..., -1]
    q_dec = q * jnp.exp(g)[..., None]
    k_dec = k * jnp.exp(g_last[..., None] - g)[..., None]

    def step(s, xs):
        attn_i, u_i, w_i, q_i, k_i, gl_i = xs
        v_new = u_i - jnp.einsum('bhcd,bhdv->bhcv', w_i, s)
        o = jnp.einsum('bhcd,bhdv->bhcv', q_i, s) + jnp.einsum('bhcj,bhjv->bhcv', attn_i, v_new)
        s = s * jnp.exp(gl_i)[..., None, None] + jnp.einsum('bhcd,bhcv->bhdv', k_i, v_new)
        return s, o

    xs = tuple(jnp.moveaxis(t, 2, 0) for t in (attn, u, w, q_dec, k_dec, g_last))
    s_fin, o = lax.scan(step, s0.astype(jnp.float32), xs)
    o = jnp.moveaxis(o, 0, 2).reshape(b, h, l, dv)
    return o, s_fin


def delta_mixer(q, k, v, z, ba, sc_w, sc_b, a_log, dt_bias, o_g, s0_f, s0_b):
    b, l, _ = q.shape
    qkv = jax.nn.silu(dwconv(jnp.concatenate([q, k, v], axis=-1), sc_w, sc_b)).astype(jnp.float32)
    q, k, v = jnp.split(qkv, 3, axis=-1)

    def heads(t):
        return t.reshape(b, l, DN_HEADS, -1).transpose(0, 2, 1, 3)

    q = l2norm(heads(q)) * (DK ** -0.5)
    k = l2norm(heads(k))
    v = heads(v)
    ba = ba.astype(jnp.float32).reshape(b, l, 4, DN_HEADS)
    beta = jax.nn.sigmoid(ba[:, :, 0:2]).transpose(2, 0, 3, 1)
    g = (-jnp.exp(a_log.astype(jnp.float32)) *
         jax.nn.softplus(ba[:, :, 2:4] + dt_bias.astype(jnp.float32))).transpose(2, 0, 3, 1)
    o_f, s_f = chunk_gated_delta(q, k, v, g[0], beta[0], s0_f)
    flip = lambda t: jnp.flip(t, axis=2)
    o_b, s_b = chunk_gated_delta(flip(q), flip(k), flip(v), flip(g[1]), flip(beta[1]), s0_b)
    o = (o_f + flip(o_b)).transpose(0, 2, 1, 3)
    o = rmsnorm(o, o_g) * jax.nn.silu(z.astype(jnp.float32).reshape(b, l, DN_HEADS, DV))
    return o.reshape(b, l, DN_W).astype(z.dtype), s_f, s_b


def conformer_conv(xc, dw_w, dw_b, ln_g, ln_b, latent):
    b, n, _ = xc.shape
    a, gt = jnp.split(xc, 2, axis=-1)
    u = a * jax.nn.sigmoid(gt)
    if latent:
        rows = n // GRID_W
        u = dwconv(u.reshape(b * rows, GRID_W, CV_W), dw_w, dw_b).reshape(b, n, CV_W)
    else:
        u = dwconv(u, dw_w, dw_b)
    return jax.nn.silu(layernorm(u, ln_g, ln_b))


def mixer(h, l, s0, latent, p):
    proj = h @ p['w_in'][l]
    o1 = F_W
    o2 = o1 + DN_W
    o3 = o2 + DN_W
    o4 = o3 + DN_W
    o5 = o4 + DN_W
    o6 = o5 + 4 * DN_HEADS
    xf, q, k, v, z, ba, xc = jnp.split(proj, [o1, o2, o3, o4, o5, o6], axis=-1)
    y_f = fourier_mix(xf)
    y_d, s_f, s_b = delta_mixer(q, k, v, z, ba, p['sc_w'][l], p['sc_b'][l], p['dn_a_log'][l],
                                p['dn_dt_bias'][l], p['dn_norm'][l], s0[0], s0[1])
    y_c = conformer_conv(xc, p['cv_dw_w'][l], p['cv_dw_b'][l], p['cv_ln_g'][l], p['cv_ln_b'][l], latent)
    y = jnp.concatenate([y_f, y_d, y_c.astype(y_f.dtype)], axis=-1) @ p['w_out'][l]
    return y, s_f, s_b


def swiglu(h, wg, wu, wd):
    return (jax.nn.silu(h @ wg) * (h @ wu)) @ wd


def moe_ffn(h, router, wg, wu, wd):
    logits = (h @ router).astype(jnp.float32)
    top_v, top_i = lax.top_k(logits, TOP_K)
    wts = jax.nn.softmax(top_v, axis=-1)
    gate = jnp.sum(jax.nn.one_hot(top_i, N_EXPERTS, dtype=jnp.float32) * wts[..., None], axis=-2).astype(h.dtype)
    out = jnp.zeros_like(h)
    for e in range(N_EXPERTS):
        out = out + gate[..., e:e + 1] * swiglu(h, wg[e], wu[e], wd[e])
    return out


def adaln(cvec, w, bias):
    m = jax.nn.silu(cvec) @ w + bias
    return jnp.split(m[:, None, :], 6, axis=-1)


def run_trunk(x, cvec, init_states, latent, p):
    new_states = []
    for l in range(DEPTH):
        sh1, sc1, g1, sh2, sc2, g2 = adaln(cvec, p['w_mod'][l], p['b_mod'][l])
        h = rmsnorm(x, p['norm1'][l]) * (1 + sc1) + sh1
        y, s_f, s_b = mixer(h, l, init_states[l], latent, p)
        x = x + g1 * y
        h = rmsnorm(x, p['norm2'][l]) * (1 + sc2) + sh2
        if l % 2 == 0:
            f = swiglu(h, p['ffn_wg'][l // 2], p['ffn_wu'][l // 2], p['ffn_wd'][l // 2])
        else:
            f = moe_ffn(h, p['moe_router'][l // 2], p['moe_wg'][l // 2], p['moe_wu'][l // 2], p['moe_wd'][l // 2])
        x = x + g2 * f
        if not latent:
            new_states.append(jnp.stack([s_f, s_b], axis=1))
    out = rmsnorm(x, p['final_norm'])
    if latent:
        return out, None
    return out, jnp.stack(new_states, axis=1)


def setup_inputs(seed: int = 0) -> dict:
    key = jax.random.key(seed)
    ks = iter(jax.random.split(key, 40))
    d = D_MODEL

    def nrm(shape, scale):
        return scale * jax.random.normal(next(ks), shape, jnp.float32)

    dt = jnp.exp(jax.random.uniform(next(ks), (DEPTH, N_DIRS, DN_HEADS), jnp.float32,
                                    math.log(1e-3), math.log(1e-1)))
    return {
        'x_prompt': nrm((BATCH, SEQ, d), 1.0),
        'x_sample': nrm((DEC_BATCH, DEC_SEQ, d), 1.0),
        'state_delta': nrm((DEC_BATCH, DEPTH, N_DIRS, DN_HEADS, DK, DV), 0.1),
        'c': nrm((DEC_BATCH, d), 1.0),
        'c_ctx': nrm((d,), 1.0),
        'norm1': 1.0 + nrm((DEPTH, d), 0.02),
        'norm2': 1.0 + nrm((DEPTH, d), 0.02),
        'w_mod': nrm((DEPTH, d, 6 * d), 0.5 * d ** -0.5),
        'b_mod': nrm((DEPTH, 6 * d), 0.01),
        'w_in': nrm((DEPTH, d, PROJ_W), d ** -0.5),
        'sc_w': nrm((DEPTH, SC_W, 3 * DN_W), SC_W ** -0.5),
        'sc_b': nrm((DEPTH, 3 * DN_W), 0.01),
        'dn_a_log': jnp.log(jax.random.uniform(next(ks), (DEPTH, N_DIRS, DN_HEADS), jnp.float32, 1.0, 16.0)),
        'dn_dt_bias': dt + jnp.log(-jnp.expm1(-dt)),
        'dn_norm': 1.0 + nrm((DEPTH, DV), 0.02),
        'cv_dw_w': nrm((DEPTH, CV_K, CV_W), CV_K ** -0.5),
        'cv_dw_b': nrm((DEPTH, CV_W), 0.01),
        'cv_ln_g': 1.0 + nrm((DEPTH, CV_W), 0.02),
        'cv_ln_b': nrm((DEPTH, CV_W), 0.01),
        'w_out': nrm((DEPTH, D_MIX, d), D_MIX ** -0.5),
        'ffn_wg': nrm((N_DENSE, d, D_FF), d ** -0.5),
        'ffn_wu': nrm((N_DENSE, d, D_FF), d ** -0.5),
        'ffn_wd': nrm((N_DENSE, D_FF, d), D_FF ** -0.5),
        'moe_router': nrm((N_MOE, d, N_EXPERTS), d ** -0.5),
        'moe_wg': nrm((N_MOE, N_EXPERTS, d, D_FF_EXPERT), d ** -0.5),
        'moe_wu': nrm((N_MOE, N_EXPERTS, d, D_FF_EXPERT), d ** -0.5),
        'moe_wd': nrm((N_MOE, N_EXPERTS, D_FF_EXPERT, d), D_FF_EXPERT ** -0.5),
        'final_norm': 1.0 + nrm((d,), 0.02),
    }


def reference(x_prompt, x_sample, state_delta, c, c_ctx, norm1, norm2, w_mod, b_mod, w_in, sc_w, sc_b,
              dn_a_log, dn_dt_bias, dn_norm, cv_dw_w, cv_dw_b, cv_ln_g, cv_ln_b, w_out,
              ffn_wg, ffn_wu, ffn_wd, moe_router, moe_wg, moe_wu, moe_wd, final_norm):
    p = dict(norm1=norm1, norm2=norm2, w_mod=w_mod, b_mod=b_mod, w_in=w_in, sc_w=sc_w, sc_b=sc_b,
             dn_a_log=dn_a_log, dn_dt_bias=dn_dt_bias, dn_norm=dn_norm, cv_dw_w=cv_dw_w, cv_dw_b=cv_dw_b,
             cv_ln_g=cv_ln_g, cv_ln_b=cv_ln_b, w_out=w_out, ffn_wg=ffn_wg, ffn_wu=ffn_wu, ffn_wd=ffn_wd,
             moe_router=moe_router, moe_wg=moe_wg, moe_wu=moe_wu, moe_wd=moe_wd, final_norm=final_norm)
    zero = jnp.zeros((x_prompt.shape[0], DN_HEADS, DK, DV), jnp.float32)
    ctx_init = [(zero, zero) for _ in range(DEPTH)]
    y_prompt, new_state_delta = run_trunk(x_prompt, c_ctx[None, :], ctx_init, False, p)
    lat_init = [(state_delta[:, l, 0], state_delta[:, l, 1]) for l in range(DEPTH)]
    y_sample, _ = run_trunk(x_sample, c, lat_init, True, p)
    return (y_prompt, y_sample, new_state_delta)
```

```python
import functools
import math

import numpy as np
import jax
import jax.numpy as jnp
from jax import lax
from jax.experimental import pallas as pl
from jax.experimental.pallas import tpu as pltpu

F32 = jnp.float32
BF16 = jnp.bfloat16
EPS = 1e-6

LANE = 128
SUBLANE = 8
VMEM_LIMIT = 56 * 1024 * 1024

CHUNK = 64
GRID_W = 64
F_GROUPS = 4
DN_HEADS = 4
N_DIRS = 2
TOP_K = 2
N_MOD = 6
MOD_ROWS = 16


def _cparams(sem):
    return pltpu.CompilerParams(dimension_semantics=sem, vmem_limit_bytes=VMEM_LIMIT)


def _pick_tile(limit, *sizes):
    t = limit
    while any(s % t for s in sizes):
        t //= 2
    return t


def _dot(a, b):
    return jnp.dot(a, b, preferred_element_type=F32)


def _mm(a, b):
    return jnp.dot(a.astype(BF16), b.astype(BF16), preferred_element_type=F32)


def _mm_nt(a, b):
    return lax.dot_general(a.astype(BF16), b.astype(BF16), (((1,), (1,)), ((), ())),
                           preferred_element_type=F32)


def _mm_tn(a, b):
    return lax.dot_general(a.astype(BF16), b.astype(BF16), (((0,), (0,)), ((), ())),
                           preferred_element_type=F32)


def _split(a):
    hi = a.astype(BF16)
    lo = (a - hi.astype(F32)).astype(BF16)
    return hi, lo


def _mm3(a, b):
    ah, al = _split(a)
    bh, bl = _split(b)
    return _dot(ah, bh) + (_dot(ah, bl) + _dot(al, bh))


def _sigmoid(x):
    return 1.0 / (1.0 + jnp.exp(-x))


def _silu(x):
    return x * _sigmoid(x)


def _rms(x):
    return x * lax.rsqrt(jnp.mean(x * x, axis=-1, keepdims=True) + EPS)


def _adaln_kernel(c_ref, w_ref, b_ref, o_ref):
    o_ref[...] = _mm(_silu(c_ref[...]), w_ref[...]) + b_ref[...]


def _adaln(cvec, w_mod, b_mod):
    depth, d, n = w_mod.shape
    tn = _pick_tile(1024, n)
    return pl.pallas_call(
        _adaln_kernel,
        out_shape=jax.ShapeDtypeStruct((depth, MOD_ROWS, n), F32),
        grid=(depth, n // tn),
        in_specs=[pl.BlockSpec((MOD_ROWS, d), lambda l, j: (0, 0)),
                  pl.BlockSpec((None, d, tn), lambda l, j: (l, 0, j)),
                  pl.BlockSpec((None, 1, tn), lambda l, j: (l, 0, j))],
        out_specs=pl.BlockSpec((None, MOD_ROWS, tn), lambda l, j: (l, 0, j)),
        compiler_params=_cparams(("parallel", "parallel")),
        name="adaln",
    )(cvec, w_mod, b_mod.reshape(depth, 1, n))


def _proj_kernel(x_ref, g_ref, mod_ref, w_ref, ccs_ref, alog_ref, dtb_ref,
                 xc_o, xs_o, qkvz_o, cv_o, gate_o, *, d, f_w, qkvz_w, cv_w):
    x = x_ref[...]
    mod = mod_ref[...]
    h = _rms(x) * g_ref[...]
    h = (h * (1.0 + mod[:, d:2 * d]) + mod[:, 0:d]).astype(BF16)

    xf = _dot(h, w_ref[:, 0:f_w])
    xcs = _dot(xf.astype(BF16), ccs_ref[...])
    xc_o[...] = xcs[:, :f_w].astype(xc_o.dtype)
    xs_o[...] = xcs[:, f_w:].astype(xs_o.dtype)
    o = f_w
    step = 4 * LANE
    for n0 in range(0, qkvz_w, step):
        qkvz_o[:, n0:n0 + step] = _dot(h, w_ref[:, o + n0:o + n0 + step])
    o += qkvz_w
    cv_o[...] = _dot(h, w_ref[:, o:o + cv_w])
    o += cv_w
    ba = _dot(h, w_ref[:, o:o + LANE])

    tm = ba.shape[0]
    lane = lax.broadcasted_iota(jnp.int32, ba.shape, 1)
    row = lax.broadcasted_iota(jnp.int32, ba.shape, 0) % CHUNK
    beta = _sigmoid(ba)
    t = ba + dtb_ref[...]
    softplus = jnp.maximum(t, 0.0) + jnp.log1p(jnp.exp(-jnp.abs(t)))
    g = -jnp.exp(alog_ref[...]) * softplus
    cf = g
    cb = g
    s = 1
    while s < CHUNK:
        cf = cf + jnp.where(row >= s, pltpu.roll(cf, s, axis=0), 0.0)
        cb = cb + jnp.where(row < CHUNK - s, pltpu.roll(cb, tm - s, axis=0), 0.0)
        s *= 2
    n_beta = N_DIRS * DN_HEADS
    gate_o[...] = jnp.where(lane < n_beta, beta, jnp.where(lane < n_beta + DN_HEADS, cf, cb))


def _proj(x, gain, mod, w_perm, ccs, alog, dtb, mod_row, tm, f_w, qkvz_w, cv_w):
    t, d = x.shape
    n = w_perm.shape[1]
    kern = functools.partial(_proj_kernel, d=d, f_w=f_w, qkvz_w=qkvz_w, cv_w=cv_w)
    row = lambda i: (i, 0)
    const = lambda i: (0, 0)
    return pl.pallas_call(
        kern,
        out_shape=(jax.ShapeDtypeStruct((t, f_w), BF16), jax.ShapeDtypeStruct((t, f_w), BF16),
                   jax.ShapeDtypeStruct((t, qkvz_w), F32), jax.ShapeDtypeStruct((t, cv_w), F32),
                   jax.ShapeDtypeStruct((t, LANE), F32)),
        grid=(t // tm,),
        in_specs=[pl.BlockSpec((tm, d), row),
                  pl.BlockSpec((1, d), const),
                  pl.BlockSpec((None, 1, N_MOD * d), lambda i: (mod_row(i), 0, 0)),
                  pl.BlockSpec((d, n), const),
                  pl.BlockSpec(ccs.shape, const),
                  pl.BlockSpec((1, LANE), const),
                  pl.BlockSpec((1, LANE), const)],
        out_specs=(pl.BlockSpec((tm, f_w), row), pl.BlockSpec((tm, f_w), row),
                   pl.BlockSpec((tm, qkvz_w), row), pl.BlockSpec((tm, cv_w), row),
                   pl.BlockSpec((tm, LANE), row)),
        compiler_params=_cparams(("parallel",)),
        name="proj",
    )(x, gain, mod, w_perm, ccs, alog, dtb)


def _fourier_kernel(xc_ref, xs_ref, cn_ref, sn_ref, o_ref, *, scale):
    y = _dot(cn_ref[...], xc_ref[...]) - _dot(sn_ref[...], xs_ref[...])
    o_ref[...] = (y * scale).astype(o_ref.dtype)


def _dft_tables(n):
    j = np.arange(n, dtype=np.int64)
    ang = (2.0 * np.pi / n) * ((j[:, None] * j[None, :]) % n).astype(np.float64)
    return np.cos(ang), np.sin(ang)


def _fourier(xc, xs, seq_len, n_batch, row_off, f_gw):
    f_w = xc.shape[1]
    cn, sn = _dft_tables(seq_len)
    cn = jnp.asarray(cn, BF16)
    sn = jnp.asarray(sn, BF16)
    tr = _pick_tile(1024, seq_len)
    nt = seq_len // tr
    blk0 = row_off // seq_len
    kern = functools.partial(_fourier_kernel, scale=1.0 / math.sqrt(seq_len * f_gw))
    return pl.pallas_call(
        kern,
        out_shape=jax.ShapeDtypeStruct((n_batch * seq_len, f_w), BF16),
        grid=(nt, n_batch),
        in_specs=[pl.BlockSpec((seq_len, f_w), lambda i, b: (blk0 + b, 0)),
                  pl.BlockSpec((seq_len, f_w), lambda i, b: (blk0 + b, 0)),
                  pl.BlockSpec((tr, seq_len), lambda i, b: (i, 0)),
                  pl.BlockSpec((tr, seq_len), lambda i, b: (i, 0))],
        out_specs=pl.BlockSpec((tr, f_w), lambda i, b: (b * nt + i, 0)),
        compiler_params=_cparams(("parallel", "parallel")),
        name="fourier",
    )(xc, xs, cn, sn)


def _unit_tri_inverse(a, m16, m32off, m64off, eye):
    x = -(a * m16)
    t = eye + x
    p = x
    for _ in range(3):
        p = _mm3(p, p)
        t = t + _mm3(t, p)
    for off in (m32off, m64off):
        t = t - _mm3(_mm3(t, a * off), t)
    return t


def _delta_kernel(*refs, seq_len, dk, zero_init, write_state):
    (q_ref, k_ref, v_ref, z_ref, gcol_ref, grow_ref,
     wq_ref, wk_ref, wv_ref, bq_ref, bk_ref, bv_ref, og_ref) = refs[:13]
    pos = 13
    if not zero_init:
        s0_ref = refs[pos]
        pos += 1
    y_ref = refs[pos]
    pos += 1
    if write_state:
        sfin_ref = refs[pos]
        pos += 1
    qs, ks, vs, of_s, ob_s, st_s = refs[pos:]

    n_chunks = seq_len // CHUNK
    rows = lax.broadcasted_iota(jnp.int32, (seq_len, dk), 0)

    def conv_silu(x_ref, w_ref, b_ref):
        x = x_ref[...]
        width = w_ref.shape[0]
        pad = (width - 1) // 2
        acc = jnp.zeros_like(x) + b_ref[...]
        for s in range(width):
            o = s - pad
            if o == 0:
                xs = x
            else:
                xs = pltpu.roll(x, (-o) % seq_len, axis=0)
                xs = jnp.where((rows + o >= 0) & (rows + o < seq_len), xs, 0.0)
            acc = acc + xs * w_ref[s:s + 1, :]
        return _silu(acc)

    def l2norm(x):
        return x * lax.rsqrt(jnp.sum(x * x, axis=-1, keepdims=True) + EPS)

    qs[...] = l2norm(conv_silu(q_ref, wq_ref, bq_ref)) * (dk ** -0.5)
    ks[...] = l2norm(conv_silu(k_ref, wk_ref, bk_ref))
    vs[...] = conv_silu(v_ref, wv_ref, bv_ref)
    if zero_init:
        st_s[...] = jnp.zeros_like(st_s)
    else:
        st_s[...] = s0_ref[...]

    ri = lax.broadcasted_iota(jnp.int32, (CHUNK, CHUNK), 0)
    ci = lax.broadcasted_iota(jnp.int32, (CHUNK, CHUNK), 1)
    eye = (ri == ci).astype(F32)
    same16 = (ri // 16) == (ci // 16)
    same32 = (ri // 32) == (ci // 32)
    m16 = same16.astype(F32)
    m32off = same32.astype(F32) - m16
    m64off = 1.0 - same32.astype(F32)
    incl = (ri >= ci, ri <= ci)
    strict = (ri > ci, ri < ci)
    n_beta = N_DIRS

    def chunk_step(c, d):
        r0 = pl.multiple_of(c * CHUNK, CHUNK)
        q = qs[pl.ds(r0, CHUNK), :]
        k = ks[pl.ds(r0, CHUNK), :]
        v = vs[pl.ds(r0, CHUNK), :]
        gc4 = gcol_ref[pl.ds(r0, CHUNK), :]
        gr4 = grow_ref[c]
        beta = gc4[:, d:d + 1]
        gcl = gc4[:, n_beta + d:n_beta + d + 1]
        grw = gr4[n_beta + d:n_beta + d + 1, :]
        decay = jnp.exp(jnp.where(incl[d], gcl - grw, -jnp.inf))
        kk = _mm_nt(k, k)
        qk = _mm_nt(q, k)
        a = jnp.where(strict[d], beta * kk * decay, 0.0)
        t = _unit_tri_inverse(a, m16, m32off, m64off, eye)
        eg = jnp.exp(gcl)
        rhs = jnp.concatenate([v * beta, k * (beta * eg)], axis=1)
        sol = _mm(t, rhs)
        u = sol[:, :dk]
        w = sol[:, dk:]
        attn = qk * decay
        g_last = gcl[CHUNK - 1:CHUNK, :] if d == 0 else gcl[0:1, :]
        qd = q * eg
        kd = k * jnp.exp(g_last - gcl)
        s = st_s[d]
        v_new = u - _mm(w, s)
        o = _mm(qd, s) + _mm(attn, v_new)
        st_s[d] = s * jnp.exp(g_last) + _mm_tn(kd, v_new)
        if d == 0:
            of_s[pl.ds(r0, CHUNK), :] = o
        else:
            ob_s[pl.ds(r0, CHUNK), :] = o

    def body(n, carry):
        chunk_step(n, 0)
        chunk_step(n_chunks - 1 - n, 1)
        return carry

    lax.fori_loop(0, n_chunks, body, 0)

    o = _rms(of_s[...] + ob_s[...]) * og_ref[...]
    y_ref[...] = (o * _silu(z_ref[...])).astype(y_ref.dtype)
    if write_state:
        sfin_ref[...] = st_s[...]


def _delta(qkvz, gcol, grow, sc_w, sc_b, o_g, s0, layer, seq_len, n_batch, row_off, dk, write_state):
    nh = DN_HEADS
    blk0 = row_off // seq_len
    n_chunks = seq_len // CHUNK
    zero_init = s0 is None
    kern = functools.partial(_delta_kernel, seq_len=seq_len, dk=dk, zero_init=zero_init,
                             write_state=write_state)

    def col(group):
        return pl.BlockSpec((seq_len, dk), lambda b, h: (blk0 + b, group * nh + h))

    def wspec(group, rows):
        return pl.BlockSpec((rows, dk), lambda b, h: (0, group * nh + h))

    width = sc_w.shape[0]
    in_specs = [col(0), col(1), col(2), col(3),
                pl.BlockSpec((None, seq_len, 2 * N_DIRS), lambda b, h: (h, blk0 + b, 0)),
                pl.BlockSpec((None, n_chunks, 2 * N_DIRS, CHUNK), lambda b, h: (h, blk0 + b, 0, 0)),
                wspec(0, width), wspec(1, width), wspec(2, width),
                wspec(0, 1), wspec(1, 1), wspec(2, 1),
                pl.BlockSpec((1, dk), lambda b, h: (0, 0))]
    args = [qkvz, qkvz, qkvz, qkvz, gcol, grow, sc_w, sc_w, sc_w, sc_b, sc_b, sc_b, o_g]
    if not zero_init:
        in_specs.append(pl.BlockSpec((None, None, N_DIRS, None, dk, dk),
                                     lambda b, h: (b, layer, 0, h, 0, 0)))
        args.append(s0)
    y_shape = jax.ShapeDtypeStruct((n_batch * seq_len, nh * dk), BF16)
    y_spec = pl.BlockSpec((seq_len, dk), lambda b, h: (b, h))
    if write_state:
        out_shape = (y_shape, jax.ShapeDtypeStruct((n_batch, N_DIRS, nh, dk, dk), F32))
        out_specs = (y_spec, pl.BlockSpec((None, N_DIRS, None, dk, dk), lambda b, h: (b, 0, h, 0, 0)))
    else:
        out_shape = y_shape
        out_specs = y_spec
    return pl.pallas_call(
        kern,
        out_shape=out_shape,
        grid=(n_batch, nh),
        in_specs=in_specs,
        out_specs=out_specs,
        scratch_shapes=[pltpu.VMEM((seq_len, dk), F32)] * 5 + [pltpu.VMEM((N_DIRS, dk, dk), F32)],
        compiler_params=_cparams(("parallel", "parallel")),
        name="delta",
    )(*args)


def _conformer_kernel(x_ref, w_ref, b_ref, lg_ref, lb_ref, o_ref, pad_s, *, n_seg, seg_blocks, halo):
    c = o_ref.shape[-1]
    width = w_ref.shape[0]
    pad = (width - 1) // 2
    seg_len = seg_blocks * GRID_W
    pad_s[:, 0:halo, :] = jnp.zeros((n_seg, halo, c), F32)
    pad_s[:, halo + seg_len:2 * halo + seg_len, :] = jnp.zeros((n_seg, halo, c), F32)
    for j in range(seg_blocks):
        if seg_blocks == 1:
            x = x_ref[...]
            pad_s[:, halo:halo + GRID_W, :] = x[..., :c] * _sigmoid(x[..., c:])
        else:
            x = x_ref[j]
            pad_s[0, halo + j * GRID_W:halo + (j + 1) * GRID_W, :] = x[:, :c] * _sigmoid(x[:, c:])

    def seg_body(r, carry):
        for j in range(seg_blocks):
            acc = jnp.zeros((GRID_W, c), F32) + b_ref[...]
            for s in range(width):
                start = halo - pad + j * GRID_W + s
                acc = acc + pad_s[r, pl.ds(start, GRID_W), :] * w_ref[s:s + 1, :]
            mu = jnp.mean(acc, axis=-1, keepdims=True)
            xc = acc - mu
            var = jnp.mean(xc * xc, axis=-1, keepdims=True)
            y = xc * lax.rsqrt(var + EPS) * lg_ref[...] + lb_ref[...]
            o_ref[r * seg_blocks + j] = _silu(y).astype(o_ref.dtype)
        return carry

    lax.fori_loop(0, n_seg, seg_body, 0)


def _conformer(cv, dw_w, dw_b, ln_g, ln_b, seq_len, n_batch, row_off, latent):
    t, c2 = cv.shape
    c = c2 // 2
    cv3 = cv.reshape(t // GRID_W, GRID_W, c2)
    blocks = seq_len // GRID_W
    n_seg, seg_blocks = (blocks, 1) if latent else (1, blocks)
    halo = 2 * SUBLANE
    assert (dw_w.shape[0] - 1) // 2 <= halo
    blk0 = row_off // seq_len
    kern = functools.partial(_conformer_kernel, n_seg=n_seg, seg_blocks=seg_blocks, halo=halo)
    const = lambda b: (0, 0)
    out = pl.pallas_call(
        kern,
        out_shape=jax.ShapeDtypeStruct((n_batch * blocks, GRID_W, c), BF16),
        grid=(n_batch,),
        in_specs=[pl.BlockSpec((blocks, GRID_W, c2), lambda b: (blk0 + b, 0, 0)),
                  pl.BlockSpec(dw_w.shape, const),
                  pl.BlockSpec((1, c), const), pl.BlockSpec((1, c), const), pl.BlockSpec((1, c), const)],
        out_specs=pl.BlockSpec((blocks, GRID_W, c), lambda b: (b, 0, 0)),
        scratch_shapes=[pltpu.VMEM((n_seg, seg_blocks * GRID_W + 2 * halo, c), F32)],
        compiler_params=_cparams(("parallel",)),
        name="conformer",
    )(cv3, dw_w, dw_b, ln_g, ln_b)
    return out.reshape(n_batch * seq_len, c)


def _outproj_kernel(yfc, yfl, ydc, ydl, ycc, ycl, w_ref, x_ref, mod_ref, o_ref, *, d, n_ctx_tiles):
    is_ctx = pl.program_id(0) < n_ctx_tiles
    y = None
    k0 = 0
    for a_c, a_l in ((yfc, yfl), (ydc, ydl), (ycc, ycl)):
        kw = a_c.shape[1]
        a = jnp.where(is_ctx, a_c[...], a_l[...])
        part = _dot(a, w_ref[k0:k0 + kw, :])
        y = part if y is None else y + part
        k0 += kw
    o_ref[...] = x_ref[...] + mod_ref[:, 2 * d:3 * d] * y


def _outproj(parts, w_out, x, mod, mod_row, tm, n_ctx_tiles):
    t, d = x.shape
    kern = functools.partial(_outproj_kernel, d=d, n_ctx_tiles=n_ctx_tiles)
    in_specs = []
    args = []
    for a_c, a_l in parts:
        kw = a_c.shape[1]
        in_specs.append(pl.BlockSpec((tm, kw), lambda i: (jnp.minimum(i, n_ctx_tiles - 1), 0)))
        in_specs.append(pl.BlockSpec((tm, kw), lambda i: (jnp.maximum(i - n_ctx_tiles, 0), 0)))
        args += [a_c, a_l]
    in_specs += [pl.BlockSpec(w_out.shape, lambda i: (0, 0)),
                 pl.BlockSpec((tm, d), lambda i: (i, 0)),
                 pl.BlockSpec((None, 1, N_MOD * d), lambda i: (mod_row(i), 0, 0))]
    args += [w_out, x, mod]
    return pl.pallas_call(
        kern,
        out_shape=jax.ShapeDtypeStruct((t, d), F32),
        grid=(t // tm,),
        in_specs=in_specs,
        out_specs=pl.BlockSpec((tm, d), lambda i: (i, 0)),
        compiler_params=_cparams(("parallel",)),
        name="outproj",
    )(*args)


def _ffn_kernel(*refs, d, n_exp, moe, final_norm):
    x_ref, g_ref, mod_ref = refs[:3]
    pos = 3
    if moe:
        router_ref = refs[pos]
        pos += 1
    wg_ref, wu_ref, wd_ref = refs[pos:pos + 3]
    pos += 3
    if final_norm:
        fg_ref = refs[pos]
        pos += 1
    o_ref = refs[pos]
    pos += 1
    h_s, acc_s = refs[pos:pos + 2]
    pos += 2
    if moe:
        gate_s = refs[pos]

    e = pl.program_id(1)
    f = pl.program_id(2)

    @pl.when((e == 0) & (f == 0))
    def _():
        mod = mod_ref[...]
        h = _rms(x_ref[...]) * g_ref[...]
        h = h * (1.0 + mod[:, 4 * d:5 * d]) + mod[:, 3 * d:4 * d]
        h_s[...] = h.astype(BF16)
        acc_s[...] = jnp.zeros_like(acc_s)
        if moe:
            logits = _mm3(h, router_ref[...])
            lane = lax.broadcasted_iota(jnp.int32, logits.shape, 1)
            lg = jnp.where(lane < n_exp, logits, -jnp.inf)
            m1 = jnp.max(lg, axis=-1, keepdims=True)
            i1 = jnp.min(jnp.where(lg == m1, lane, LANE), axis=-1, keepdims=True)
            lg2 = jnp.where(lane == i1, -jnp.inf, lg)
            m2 = jnp.max(lg2, axis=-1, keepdims=True)
            i2 = jnp.min(jnp.where(lg2 == m2, lane, LANE), axis=-1, keepdims=True)
            w1 = 1.0 / (1.0 + jnp.exp(m2 - m1))
            gate_s[...] = jnp.where(lane == i1, w1, 0.0) + jnp.where(lane == i2, 1.0 - w1, 0.0)

    h = h_s[...]
    a = _silu(_dot(h, wg_ref[...].astype(BF16))) * _dot(h, wu_ref[...].astype(BF16))
    if moe:
        gate = gate_s[...]
        lane = lax.broadcasted_iota(jnp.int32, gate.shape, 1)
        a = a * jnp.sum(jnp.where(lane == e, gate, 0.0), axis=-1, keepdims=True)
    acc_s[...] += _dot(a.astype(BF16), wd_ref[...].astype(BF16))

    @pl.when((e == n_exp - 1) & (f == pl.num_programs(2) - 1))
    def _():
        o = x_ref[...] + mod_ref[:, 5 * d:6 * d] * acc_s[...]
        if final_norm:
            o = _rms(o) * fg_ref[...]
        o_ref[...] = o


def _ffn(x, gain, mod, mod_row, wg, wu, wd, router, final_gain, tm, tf):
    t, d = x.shape
    n_exp, _, ff = wg.shape
    moe = router is not None
    final_norm = final_gain is not None
    kern = functools.partial(_ffn_kernel, d=d, n_exp=n_exp, moe=moe, final_norm=final_norm)
    row = lambda i, e, f: (i, 0)
    const = lambda i, e, f: (0, 0)
    in_specs = [pl.BlockSpec((tm, d), row),
                pl.BlockSpec((1, d), const),
                pl.BlockSpec((None, 1, N_MOD * d), lambda i, e, f: (mod_row(i), 0, 0))]
    args = [x, gain, mod]
    if moe:
        in_specs.append(pl.BlockSpec(router.shape, const))
        args.append(router)
    in_specs += [pl.BlockSpec((None, d, tf), lambda i, e, f: (e, 0, f)),
                 pl.BlockSpec((None, d, tf), lambda i, e, f: (e, 0, f)),
                 pl.BlockSpec((None, tf, d), lambda i, e, f: (e, f, 0))]
    args += [wg, wu, wd]
    if final_norm:
        in_specs.append(pl.BlockSpec((1, d), const))
        args.append(final_gain)
    scratch = [pltpu.VMEM((tm, d), BF16), pltpu.VMEM((tm, d), F32)]
    if moe:
        scratch.append(pltpu.VMEM((tm, LANE), F32))
    return pl.pallas_call(
        kern,
        out_shape=jax.ShapeDtypeStruct((t, d), F32),
        grid=(t // tm, n_exp, ff // tf),
        in_specs=in_specs,
        out_specs=pl.BlockSpec((tm, d), row),
        scratch_shapes=scratch,
        compiler_params=_cparams(("parallel", "arbitrary", "arbitrary")),
        name="moe_ffn" if moe else "ffn",
    )(*args)


def _lane_row(vals, offset):
    n = vals.shape[0]
    return jnp.zeros((1, LANE), F32).at[0, offset:offset + n].set(vals.astype(F32))


def kernel(x_prompt, x_sample, state_delta, c, c_ctx, norm1, norm2, w_mod, b_mod, w_in, sc_w, sc_b, dn_a_log, dn_dt_bias, dn_norm, cv_dw_w, cv_dw_b, cv_ln_g, cv_ln_b, w_out, ffn_wg, ffn_wu, ffn_wd, moe_router, moe_wg, moe_wu, moe_wd, final_norm):
    b_ctx, l_ctx, d = x_prompt.shape
    b_lat, l_lat, _ = x_sample.shape
    depth = w_in.shape[0]
    t_ctx = b_ctx * l_ctx
    t_lat = b_lat * l_lat
    f_w = d // 4
    f_gw = f_w // F_GROUPS
    dn_w = d // 2
    dk = dn_w // DN_HEADS
    cv_w = d // 4
    qkvz_w = 4 * dn_w
    n_gate = 2 * N_DIRS * DN_HEADS
    n_exp = moe_wg.shape[1]
    assert dk == LANE and t_ctx % l_lat == 0 and b_lat + 1 <= MOD_ROWS
    assert l_ctx % CHUNK == 0 and l_lat % CHUNK == 0 and l_lat % GRID_W == 0 and CHUNK == GRID_W

    tm = _pick_tile(512, t_ctx, l_lat)
    tm_ffn = _pick_tile(1024, t_ctx, l_lat)

    def mod_row_fn(tile):
        n_ctx_tiles = t_ctx // tile
        per_seq = l_lat // tile
        return lambda i: jnp.where(i < n_ctx_tiles, 0, 1 + (i - n_ctx_tiles) // per_seq)

    x = jnp.concatenate([x_prompt.reshape(t_ctx, d), x_sample.reshape(t_lat, d)], axis=0)
    cvec = jnp.zeros((MOD_ROWS, d), F32).at[0].set(c_ctx).at[1:1 + b_lat].set(c)
    mods = _adaln(cvec, w_mod, b_mod).reshape(depth, MOD_ROWS, 1, N_MOD * d)

    cg, sg = _dft_tables(f_gw)
    eye_g = np.eye(F_GROUPS)
    ccs = jnp.asarray(np.concatenate([np.kron(eye_g, cg), np.kron(eye_g, sg)], axis=1), BF16)

    passes = ((l_ctx, b_ctx, 0, False), (l_lat, b_lat, t_ctx, True))
    o_q = f_w
    o_ba = f_w + qkvz_w
    o_cv = o_ba + n_gate
    new_states = []
    for l in range(depth):
        w = w_in[l]
        w_perm = jnp.concatenate(
            [w[:, :o_ba], w[:, o_cv:o_cv + 2 * cv_w], w[:, o_ba:o_cv], jnp.zeros((d, LANE - n_gate), F32)],
            axis=1).astype(BF16)
        alog = _lane_row(dn_a_log[l].reshape(-1), N_DIRS * DN_HEADS)
        dtb = _lane_row(dn_dt_bias[l].reshape(-1), N_DIRS * DN_HEADS)
        xc, xs, qkvz, cv, gates = _proj(x, norm1[l][None], mods[l], w_perm, ccs, alog, dtb,
                                        mod_row_fn(tm), tm, f_w, qkvz_w, 2 * cv_w)
        g16 = gates[:, :n_gate].reshape(-1, 2 * N_DIRS, DN_HEADS)
        gcol = g16.transpose(2, 0, 1)
        grow = g16.reshape(-1, CHUNK, 2 * N_DIRS, DN_HEADS).transpose(3, 0, 2, 1)

        parts = [[], [], []]
        for seq_len, n_batch, row_off, latent in passes:
            parts[0].append(_fourier(xc, xs, seq_len, n_batch, row_off, f_gw))
            res = _delta(qkvz, gcol, grow, sc_w[l], sc_b[l][None], dn_norm[l][None],
                         state_delta if latent else None, l, seq_len, n_batch, row_off, dk,
                         write_state=not latent)
            if latent:
                parts[1].append(res)
            else:
                parts[1].append(res[0])
                new_states.append(res[1])
            parts[2].append(_conformer(cv, cv_dw_w[l], cv_dw_b[l][None], cv_ln_g[l][None],
                                       cv_ln_b[l][None], seq_len, n_batch, row_off, latent))
        x = _outproj(parts, w_out[l].astype(BF16), x, mods[l], mod_row_fn(tm), tm, t_ctx // tm)

        fg = final_norm[None] if l == depth - 1 else None
        if l % 2 == 0:
            i = l // 2
            x = _ffn(x, norm2[l][None], mods[l], mod_row_fn(tm_ffn), ffn_wg[i][None], ffn_wu[i][None],
                     ffn_wd[i][None], None, fg, tm_ffn, _pick_tile(512, ffn_wg.shape[-1]))
        else:
            i = l // 2
            router = jnp.zeros((d, LANE), F32).at[:, :n_exp].set(moe_router[i])
            x = _ffn(x, norm2[l][None], mods[l], mod_row_fn(tm_ffn), moe_wg[i], moe_wu[i], moe_wd[i],
                     router, fg, tm_ffn, _pick_tile(512, moe_wg.shape[-1]))

    y_prompt = x[:t_ctx].reshape(b_ctx, l_ctx, d)
    y_sample = x[t_ctx:].reshape(b_lat, l_lat, d)
    return y_prompt, y_sample, jnp.stack(new_states, axis=1)
```

```python
import functools
import math

import numpy as np
import jax
import jax.numpy as jnp
from jax import lax
from jax.experimental import pallas as pl
from jax.experimental.pallas import tpu as pltpu

F32 = jnp.float32
BF16 = jnp.bfloat16
EPS = 1e-6

LANE = 128
SUBLANE = 8
VMEM_LIMIT = 56 * 1024 * 1024

CHUNK = 64
GRID_W = 64
F_GROUPS = 4
DN_HEADS = 4
N_DIRS = 2
PREP_UNROLL = 8
TOP_K = 2
N_MOD = 6
MOD_ROWS = 16


def _cparams(sem):
    return pltpu.CompilerParams(dimension_semantics=sem, vmem_limit_bytes=VMEM_LIMIT)


def _pick_tile(limit, *sizes):
    t = limit
    while any(s % t for s in sizes):
        t //= 2
    return t


def _dot(a, b):
    return jnp.dot(a, b, preferred_element_type=F32)


def _mm(a, b):
    return jnp.dot(a.astype(BF16), b.astype(BF16), preferred_element_type=F32)


def _mm_nt(a, b):
    return lax.dot_general(a.astype(BF16), b.astype(BF16), (((1,), (1,)), ((), ())),
                           preferred_element_type=F32)


def _mm_tn(a, b):
    return lax.dot_general(a.astype(BF16), b.astype(BF16), (((0,), (0,)), ((), ())),
                           preferred_element_type=F32)


def _split(a):
    hi = a.astype(BF16)
    lo = (a - hi.astype(F32)).astype(BF16)
    return hi, lo


def _mm3(a, b):
    ah, al = _split(a)
    bh, bl = _split(b)
    return _dot(ah, bh) + (_dot(ah, bl) + _dot(al, bh))


def _sigmoid(x):
    return 1.0 / (1.0 + jnp.exp(-x))


def _silu(x):
    return x * _sigmoid(x)


def _rms(x):
    return x * lax.rsqrt(jnp.mean(x * x, axis=-1, keepdims=True) + EPS)


def _adaln_kernel(c_ref, w_ref, b_ref, o_ref):
    o_ref[...] = _mm(_silu(c_ref[...]), w_ref[...]) + b_ref[...]


def _adaln(cvec, w_mod, b_mod):
    depth, d, n = w_mod.shape
    tn = _pick_tile(1024, n)
    return pl.pallas_call(
        _adaln_kernel,
        out_shape=jax.ShapeDtypeStruct((depth, MOD_ROWS, n), F32),
        grid=(depth, n // tn),
        in_specs=[pl.BlockSpec((MOD_ROWS, d), lambda l, j: (0, 0)),
                  pl.BlockSpec((None, d, tn), lambda l, j: (l, 0, j)),
                  pl.BlockSpec((None, 1, tn), lambda l, j: (l, 0, j))],
        out_specs=pl.BlockSpec((None, MOD_ROWS, tn), lambda l, j: (l, 0, j)),
        compiler_params=_cparams(("parallel", "parallel")),
        name="adaln",
    )(cvec, w_mod, b_mod.reshape(depth, 1, n))


def _proj_kernel(x_ref, g_ref, mod_ref, w_ref, ccs_ref, alog_ref, dtb_ref,
                 xc_o, xs_o, qkvz_o, cv_o, gate_o, *, d, f_w, qkvz_w, cv_w):
    x = x_ref[...]
    mod = mod_ref[...]
    h = _rms(x) * g_ref[...]
    h = (h * (1.0 + mod[:, d:2 * d]) + mod[:, 0:d]).astype(BF16)

    xf = _dot(h, w_ref[:, 0:f_w])
    xcs = _dot(xf.astype(BF16), ccs_ref[...])
    xc_o[...] = xcs[:, :f_w].astype(xc_o.dtype)
    xs_o[...] = xcs[:, f_w:].astype(xs_o.dtype)
    o = f_w
    step = 4 * LANE
    for n0 in range(0, qkvz_w, step):
        qkvz_o[:, n0:n0 + step] = _dot(h, w_ref[:, o + n0:o + n0 + step])
    o += qkvz_w
    cv_o[...] = _dot(h, w_ref[:, o:o + cv_w])
    o += cv_w
    ba = _dot(h, w_ref[:, o:o + LANE])

    tm = ba.shape[0]
    lane = lax.broadcasted_iota(jnp.int32, ba.shape, 1)
    row = lax.broadcasted_iota(jnp.int32, ba.shape, 0) % CHUNK
    beta = _sigmoid(ba)
    t = ba + dtb_ref[...]
    softplus = jnp.maximum(t, 0.0) + jnp.log1p(jnp.exp(-jnp.abs(t)))
    g = -jnp.exp(alog_ref[...]) * softplus
    cf = g
    cb = g
    s = 1
    while s < CHUNK:
        cf = cf + jnp.where(row >= s, pltpu.roll(cf, s, axis=0), 0.0)
        cb = cb + jnp.where(row < CHUNK - s, pltpu.roll(cb, tm - s, axis=0), 0.0)
        s *= 2
    n_beta = N_DIRS * DN_HEADS
    gate_o[...] = jnp.where(lane < n_beta, beta, jnp.where(lane < n_beta + DN_HEADS, cf, cb))


def _proj(x, gain, mod, w_perm, ccs, alog, dtb, mod_row, tm, f_w, qkvz_w, cv_w):
    t, d = x.shape
    n = w_perm.shape[1]
    kern = functools.partial(_proj_kernel, d=d, f_w=f_w, qkvz_w=qkvz_w, cv_w=cv_w)
    row = lambda i: (i, 0)
    const = lambda i: (0, 0)
    return pl.pallas_call(
        kern,
        out_shape=(jax.ShapeDtypeStruct((t, f_w), BF16), jax.ShapeDtypeStruct((t, f_w), BF16),
                   jax.ShapeDtypeStruct((t, qkvz_w), F32), jax.ShapeDtypeStruct((t, cv_w), F32),
                   jax.ShapeDtypeStruct((t, LANE), F32)),
        grid=(t // tm,),
        in_specs=[pl.BlockSpec((tm, d), row),
                  pl.BlockSpec((1, d), const),
                  pl.BlockSpec((None, 1, N_MOD * d), lambda i: (mod_row(i), 0, 0)),
                  pl.BlockSpec((d, n), const),
                  pl.BlockSpec(ccs.shape, const),
                  pl.BlockSpec((1, LANE), const),
                  pl.BlockSpec((1, LANE), const)],
        out_specs=(pl.BlockSpec((tm, f_w), row), pl.BlockSpec((tm, f_w), row),
                   pl.BlockSpec((tm, qkvz_w), row), pl.BlockSpec((tm, cv_w), row),
                   pl.BlockSpec((tm, LANE), row)),
        compiler_params=_cparams(("parallel",)),
        name="proj",
    )(x, gain, mod, w_perm, ccs, alog, dtb)


def _fourier_kernel(xc_ref, xs_ref, cn_ref, sn_ref, o_ref, *, scale):
    y = _dot(cn_ref[...], xc_ref[...]) - _dot(sn_ref[...], xs_ref[...])
    o_ref[...] = (y * scale).astype(o_ref.dtype)


def _dft_tables(n):
    j = np.arange(n, dtype=np.int64)
    ang = (2.0 * np.pi / n) * ((j[:, None] * j[None, :]) % n).astype(np.float64)
    return np.cos(ang), np.sin(ang)


def _fourier(xc, xs, seq_len, n_batch, row_off, f_gw):
    f_w = xc.shape[1]
    cn, sn = _dft_tables(seq_len)
    cn = jnp.asarray(cn, F32).astype(BF16)
    sn = jnp.asarray(sn, F32).astype(BF16)
    tr = _pick_tile(1024, seq_len)
    nt = seq_len // tr
    blk0 = row_off // seq_len
    kern = functools.partial(_fourier_kernel, scale=1.0 / math.sqrt(seq_len * f_gw))
    return pl.pallas_call(
        kern,
        out_shape=jax.ShapeDtypeStruct((n_batch * seq_len, f_w), BF16),
        grid=(nt, n_batch),
        in_specs=[pl.BlockSpec((seq_len, f_w), lambda i, b: (blk0 + b, 0)),
                  pl.BlockSpec((seq_len, f_w), lambda i, b: (blk0 + b, 0)),
                  pl.BlockSpec((tr, seq_len), lambda i, b: (i, 0)),
                  pl.BlockSpec((tr, seq_len), lambda i, b: (i, 0))],
        out_specs=pl.BlockSpec((tr, f_w), lambda i, b: (b * nt + i, 0)),
        compiler_params=_cparams(("parallel", "parallel")),
        name="fourier",
    )(xc, xs, cn, sn)


def _unit_tri_inverse(mats, m16, m32off, m64off, eye):
    ps = [-(a * m16) for a in mats]
    ts = [eye + p for p in ps]
    for _ in range(3):
        ps = [_mm(p, p) for p in ps]
        ts = [t + _mm(t, p) for t, p in zip(ts, ps)]
    for off in (m32off, m64off):
        us = [_mm(t, a * off) for t, a in zip(ts, mats)]
        ts = [t - _mm(u, t) for t, u in zip(ts, us)]
    return ts


def _delta_kernel(*refs, seq_len, dk, zero_init, write_state):
    (q_ref, k_ref, v_ref, z_ref, gcol_ref, grow_ref,
     wq_ref, wk_ref, wv_ref, bq_ref, bk_ref, bv_ref, og_ref) = refs[:13]
    pos = 13
    if not zero_init:
        s0_ref = refs[pos]
        pos += 1
    y_ref = refs[pos]
    pos += 1
    if write_state:
        sfin_ref = refs[pos]
        pos += 1
    qs, ks, vs, of_s, ob_s, st_s, pq_s, n_s = refs[pos:]

    n_chunks = seq_len // CHUNK
    rows = lax.broadcasted_iota(jnp.int32, (seq_len, dk), 0)

    def conv_silu(x_ref, w_ref, b_ref):
        x = x_ref[...]
        width = w_ref.shape[0]
        pad = (width - 1) // 2
        acc = jnp.zeros_like(x) + b_ref[...]
        for s in range(width):
            o = s - pad
            if o == 0:
                xs = x
            else:
                xs = pltpu.roll(x, (-o) % seq_len, axis=0)
                xs = jnp.where((rows + o >= 0) & (rows + o < seq_len), xs, 0.0)
            acc = acc + xs * w_ref[s:s + 1, :]
        return _silu(acc)

    def l2norm(x):
        return x * lax.rsqrt(jnp.sum(x * x, axis=-1, keepdims=True) + EPS)

    qs[...] = l2norm(conv_silu(q_ref, wq_ref, bq_ref)) * (dk ** -0.5)
    ks[...] = l2norm(conv_silu(k_ref, wk_ref, bk_ref))
    vs[...] = conv_silu(v_ref, wv_ref, bv_ref)
    if zero_init:
        st_s[...] = jnp.zeros_like(st_s)
    else:
        st_s[...] = s0_ref[...]

    ri = lax.broadcasted_iota(jnp.int32, (CHUNK, CHUNK), 0)
    ci = lax.broadcasted_iota(jnp.int32, (CHUNK, CHUNK), 1)
    eye = (ri == ci).astype(F32)
    same16 = (ri // 16) == (ci // 16)
    same32 = (ri // 32) == (ci // 32)
    m16 = same16.astype(F32)
    m32off = same32.astype(F32) - m16
    m64off = 1.0 - same32.astype(F32)
    incl = (ri >= ci, ri <= ci)
    strict = (ri > ci, ri < ci)
    n_beta = N_DIRS

    o_s = (of_s, ob_s)
    last_row = (CHUNK - 1, 0)

    def prep_group(first_chunk, chunks):
        chains = [(c, d) for c in chunks for d in range(N_DIRS)]
        cidx = {c: first_chunk + c for c in chunks}
        r0 = {c: pl.multiple_of(cidx[c] * CHUNK, CHUNK) for c in chunks}
        q = {c: qs[pl.ds(r0[c], CHUNK), :] for c in chunks}
        k = {c: ks[pl.ds(r0[c], CHUNK), :] for c in chunks}
        v = {c: vs[pl.ds(r0[c], CHUNK), :] for c in chunks}
        gc4 = {c: gcol_ref[pl.ds(r0[c], CHUNK), :] for c in chunks}
        gr4 = {c: grow_ref[cidx[c]] for c in chunks}
        kq = {}
        for c in chunks:
            k16 = k[c].astype(BF16)
            kq[c] = _mm_nt(jnp.concatenate([k16, q[c].astype(BF16)], axis=0), k16)
        beta, gcl, decay, a = {}, {}, {}, []
        for c, d in chains:
            beta[c, d] = gc4[c][:, d:d + 1]
            gcl[c, d] = gc4[c][:, n_beta + d:n_beta + d + 1]
            grw = gr4[c][n_beta + d:n_beta + d + 1, :]
            decay[c, d] = jnp.exp(jnp.where(incl[d], gcl[c, d] - grw, -jnp.inf))
            a.append(jnp.where(strict[d], beta[c, d] * kq[c][:CHUNK] * decay[c, d], 0.0))
        t = dict(zip(chains, _unit_tri_inverse(a, m16, m32off, m64off, eye)))
        eg = {cd: jnp.exp(gcl[cd]) for cd in chains}
        sol = {(c, d): _mm(t[c, d], jnp.concatenate([v[c] * beta[c, d], k[c] * (beta[c, d] * eg[c, d])], axis=1))
               for c, d in chains}
        aw = {(c, d): _mm(kq[c][CHUNK:] * decay[c, d], sol[c, d]) for c, d in chains}
        kuw = {}
        for c, d in chains:
            g_last = gcl[c, d][last_row[d]:last_row[d] + 1, :]
            kuw[c, d] = _mm_tn(k[c] * jnp.exp(g_last - gcl[c, d]), sol[c, d])
        for c, d in chains:
            pq_s[d, cidx[c]] = jnp.concatenate(
                [kuw[c, d][:, dk:], q[c] * eg[c, d] - aw[c, d][:, dk:]], axis=0).astype(BF16)
            n_s[d, cidx[c]] = kuw[c, d][:, :dk]
            o_s[d][pl.ds(r0[c], CHUNK), :] = aw[c, d][:, :dk]

    unroll = math.gcd(PREP_UNROLL, n_chunks)

    def prep_body(i, carry):
        prep_group(i * unroll, range(unroll))
        return carry

    lax.fori_loop(0, n_chunks // unroll, prep_body, 0)

    def scan_body(n, carry):
        for d in range(N_DIRS):
            c = n if d == 0 else n_chunks - 1 - n
            r0 = pl.multiple_of(c * CHUNK, CHUNK)
            g_last = grow_ref[c][n_beta + d:n_beta + d + 1, last_row[d]:last_row[d] + 1]
            s = st_s[d]
            r = _dot(pq_s[d, c], s.astype(BF16))
            st_s[d] = s * jnp.exp(g_last) + n_s[d, c] - r[:dk]
            o_s[d][pl.ds(r0, CHUNK), :] += r[dk:]
        return carry

    lax.fori_loop(0, n_chunks, scan_body, 0)

    o = _rms(of_s[...] + ob_s[...]) * og_ref[...]
    y_ref[...] = (o * _silu(z_ref[...])).astype(y_ref.dtype)
    if write_state:
        sfin_ref[...] = st_s[...]


def _delta(qkvz, gcol, grow, sc_w, sc_b, o_g, s0, layer, seq_len, n_batch, row_off, dk, write_state):
    nh = DN_HEADS
    blk0 = row_off // seq_len
    n_chunks = seq_len // CHUNK
    zero_init = s0 is None
    kern = functools.partial(_delta_kernel, seq_len=seq_len, dk=dk, zero_init=zero_init,
                             write_state=write_state)

    def col(group):
        return pl.BlockSpec((seq_len, dk), lambda b, h: (blk0 + b, group * nh + h))

    def wspec(group, rows):
        return pl.BlockSpec((rows, dk), lambda b, h: (0, group * nh + h))

    width = sc_w.shape[0]
    in_specs = [col(0), col(1), col(2), col(3),
                pl.BlockSpec((None, seq_len, 2 * N_DIRS), lambda b, h: (h, blk0 + b, 0)),
                pl.BlockSpec((None, n_chunks, 2 * N_DIRS, CHUNK), lambda b, h: (h, blk0 + b, 0, 0)),
                wspec(0, width), wspec(1, width), wspec(2, width),
                wspec(0, 1), wspec(1, 1), wspec(2, 1),
                pl.BlockSpec((1, dk), lambda b, h: (0, 0))]
    args = [qkvz, qkvz, qkvz, qkvz, gcol, grow, sc_w, sc_w, sc_w, sc_b, sc_b, sc_b, o_g]
    if not zero_init:
        in_specs.append(pl.BlockSpec((None, None, N_DIRS, None, dk, dk),
                                     lambda b, h: (b, layer, 0, h, 0, 0)))
        args.append(s0)
    y_shape = jax.ShapeDtypeStruct((n_batch * seq_len, nh * dk), BF16)
    y_spec = pl.BlockSpec((seq_len, dk), lambda b, h: (b, h))
    if write_state:
        out_shape = (y_shape, jax.ShapeDtypeStruct((n_batch, N_DIRS, nh, dk, dk), F32))
        out_specs = (y_spec, pl.BlockSpec((None, N_DIRS, None, dk, dk), lambda b, h: (b, 0, h, 0, 0)))
    else:
        out_shape = y_shape
        out_specs = y_spec
    return pl.pallas_call(
        kern,
        out_shape=out_shape,
        grid=(n_batch, nh),
        in_specs=in_specs,
        out_specs=out_specs,
        scratch_shapes=[pltpu.VMEM((seq_len, dk), F32)] * 5 + [
            pltpu.VMEM((N_DIRS, dk, dk), F32),
            pltpu.VMEM((N_DIRS, n_chunks, dk + CHUNK, dk), BF16),
            pltpu.VMEM((N_DIRS, n_chunks, dk, dk), F32)],
        compiler_params=_cparams(("parallel", "parallel")),
        name="delta",
    )(*args)


def _conformer_kernel(x_ref, w_ref, b_ref, lg_ref, lb_ref, o_ref, pad_s, *, n_seg, seg_blocks, halo):
    c = o_ref.shape[-1]
    width = w_ref.shape[0]
    pad = (width - 1) // 2
    seg_len = seg_blocks * GRID_W
    pad_s[:, 0:halo, :] = jnp.zeros((n_seg, halo, c), F32)
    pad_s[:, halo + seg_len:2 * halo + seg_len, :] = jnp.zeros((n_seg, halo, c), F32)
    for j in range(seg_blocks):
        if seg_blocks == 1:
            x = x_ref[...]
            pad_s[:, halo:halo + GRID_W, :] = x[..., :c] * _sigmoid(x[..., c:])
        else:
            x = x_ref[j]
            pad_s[0, halo + j * GRID_W:halo + (j + 1) * GRID_W, :] = x[:, :c] * _sigmoid(x[:, c:])

    def seg_body(r, carry):
        for j in range(seg_blocks):
            acc = jnp.zeros((GRID_W, c), F32) + b_ref[...]
            for s in range(width):
                start = halo - pad + j * GRID_W + s
                acc = acc + pad_s[r, pl.ds(start, GRID_W), :] * w_ref[s:s + 1, :]
            mu = jnp.mean(acc, axis=-1, keepdims=True)
            xc = acc - mu
            var = jnp.mean(xc * xc, axis=-1, keepdims=True)
            y = xc * lax.rsqrt(var + EPS) * lg_ref[...] + lb_ref[...]
            o_ref[r * seg_blocks + j] = _silu(y).astype(o_ref.dtype)
        return carry

    lax.fori_loop(0, n_seg, seg_body, 0)


def _conformer(cv, dw_w, dw_b, ln_g, ln_b, seq_len, n_batch, row_off, latent):
    t, c2 = cv.shape
    c = c2 // 2
    cv3 = cv.reshape(t // GRID_W, GRID_W, c2)
    blocks = seq_len // GRID_W
    n_seg, seg_blocks = (blocks, 1) if latent else (1, blocks)
    halo = 2 * SUBLANE
    assert (dw_w.shape[0] - 1) // 2 <= halo
    blk0 = row_off // seq_len
    kern = functools.partial(_conformer_kernel, n_seg=n_seg, seg_blocks=seg_blocks, halo=halo)
    const = lambda b: (0, 0)
    out = pl.pallas_call(
        kern,
        out_shape=jax.ShapeDtypeStruct((n_batch * blocks, GRID_W, c), BF16),
        grid=(n_batch,),
        in_specs=[pl.BlockSpec((blocks, GRID_W, c2), lambda b: (blk0 + b, 0, 0)),
                  pl.BlockSpec(dw_w.shape, const),
                  pl.BlockSpec((1, c), const), pl.BlockSpec((1, c), const), pl.BlockSpec((1, c), const)],
        out_specs=pl.BlockSpec((blocks, GRID_W, c), lambda b: (b, 0, 0)),
        scratch_shapes=[pltpu.VMEM((n_seg, seg_blocks * GRID_W + 2 * halo, c), F32)],
        compiler_params=_cparams(("parallel",)),
        name="conformer",
    )(cv3, dw_w, dw_b, ln_g, ln_b)
    return out.reshape(n_batch * seq_len, c)


def _outproj_kernel(yfc, yfl, ydc, ydl, ycc, ycl, w_ref, x_ref, mod_ref, o_ref, *, d, n_ctx_tiles):
    is_ctx = pl.program_id(0) < n_ctx_tiles
    y = None
    k0 = 0
    for a_c, a_l in ((yfc, yfl), (ydc, ydl), (ycc, ycl)):
        kw = a_c.shape[1]
        a = jnp.where(is_ctx, a_c[...], a_l[...])
        part = _dot(a, w_ref[k0:k0 + kw, :])
        y = part if y is None else y + part
        k0 += kw
    o_ref[...] = x_ref[...] + mod_ref[:, 2 * d:3 * d] * y


def _outproj(parts, w_out, x, mod, mod_row, tm, n_ctx_tiles):
    t, d = x.shape
    kern = functools.partial(_outproj_kernel, d=d, n_ctx_tiles=n_ctx_tiles)
    in_specs = []
    args = []
    for a_c, a_l in parts:
        kw = a_c.shape[1]
        in_specs.append(pl.BlockSpec((tm, kw), lambda i: (jnp.minimum(i, n_ctx_tiles - 1), 0)))
        in_specs.append(pl.BlockSpec((tm, kw), lambda i: (jnp.maximum(i - n_ctx_tiles, 0), 0)))
        args += [a_c, a_l]
    in_specs += [pl.BlockSpec(w_out.shape, lambda i: (0, 0)),
                 pl.BlockSpec((tm, d), lambda i: (i, 0)),
                 pl.BlockSpec((None, 1, N_MOD * d), lambda i: (mod_row(i), 0, 0))]
    args += [w_out, x, mod]
    return pl.pallas_call(
        kern,
        out_shape=jax.ShapeDtypeStruct((t, d), F32),
        grid=(t // tm,),
        in_specs=in_specs,
        out_specs=pl.BlockSpec((tm, d), lambda i: (i, 0)),
        compiler_params=_cparams(("parallel",)),
        name="outproj",
    )(*args)


def _ffn_kernel(*refs, d, n_exp, moe, final_norm):
    x_ref, g_ref, mod_ref = refs[:3]
    pos = 3
    if moe:
        router_ref = refs[pos]
        pos += 1
    wg_ref, wu_ref, wd_ref = refs[pos:pos + 3]
    pos += 3
    if final_norm:
        fg_ref = refs[pos]
        pos += 1
    o_ref = refs[pos]
    pos += 1
    h_s, acc_s = refs[pos:pos + 2]
    pos += 2
    if moe:
        gate_s = refs[pos]

    e = pl.program_id(1)
    f = pl.program_id(2)

    @pl.when((e == 0) & (f == 0))
    def _():
        mod = mod_ref[...]
        h = _rms(x_ref[...]) * g_ref[...]
        h = h * (1.0 + mod[:, 4 * d:5 * d]) + mod[:, 3 * d:4 * d]
        h_s[...] = h.astype(BF16)
        acc_s[...] = jnp.zeros_like(acc_s)
        if moe:
            logits = _mm3(h, router_ref[...])
            lane = lax.broadcasted_iota(jnp.int32, logits.shape, 1)
            lg = jnp.where(lane < n_exp, logits, -jnp.inf)
            m1 = jnp.max(lg, axis=-1, keepdims=True)
            i1 = jnp.min(jnp.where(lg == m1, lane, LANE), axis=-1, keepdims=True)
            lg2 = jnp.where(lane == i1, -jnp.inf, lg)
            m2 = jnp.max(lg2, axis=-1, keepdims=True)
            i2 = jnp.min(jnp.where(lg2 == m2, lane, LANE), axis=-1, keepdims=True)
            w1 = 1.0 / (1.0 + jnp.exp(m2 - m1))
            gate_s[...] = jnp.where(lane == i1, w1, 0.0) + jnp.where(lane == i2, 1.0 - w1, 0.0)

    h = h_s[...]
    a = _silu(_dot(h, wg_ref[...].astype(BF16))) * _dot(h, wu_ref[...].astype(BF16))
    if moe:
        gate = gate_s[...]
        lane = lax.broadcasted_iota(jnp.int32, gate.shape, 1)
        a = a * jnp.sum(jnp.where(lane == e, gate, 0.0), axis=-1, keepdims=True)
    acc_s[...] += _dot(a.astype(BF16), wd_ref[...].astype(BF16))

    @pl.when((e == n_exp - 1) & (f == pl.num_programs(2) - 1))
    def _():
        o = x_ref[...] + mod_ref[:, 5 * d:6 * d] * acc_s[...]
        if final_norm:
            o = _rms(o) * fg_ref[...]
        o_ref[...] = o


def _ffn(x, gain, mod, mod_row, wg, wu, wd, router, final_gain, tm, tf):
    t, d = x.shape
    n_exp, _, ff = wg.shape
    moe = router is not None
    final_norm = final_gain is not None
    kern = functools.partial(_ffn_kernel, d=d, n_exp=n_exp, moe=moe, final_norm=final_norm)
    row = lambda i, e, f: (i, 0)
    const = lambda i, e, f: (0, 0)
    in_specs = [pl.BlockSpec((tm, d), row),
                pl.BlockSpec((1, d), const),
                pl.BlockSpec((None, 1, N_MOD * d), lambda i, e, f: (mod_row(i), 0, 0))]
    args = [x, gain, mod]
    if moe:
        in_specs.append(pl.BlockSpec(router.shape, const))
        args.append(router)
    in_specs += [pl.BlockSpec((None, d, tf), lambda i, e, f: (e, 0, f)),
                 pl.BlockSpec((None, d, tf), lambda i, e, f: (e, 0, f)),
                 pl.BlockSpec((None, tf, d), lambda i, e, f: (e, f, 0))]
    args += [wg, wu, wd]
    if final_norm:
        in_specs.append(pl.BlockSpec((1, d), const))
        args.append(final_gain)
    scratch = [pltpu.VMEM((tm, d), BF16), pltpu.VMEM((tm, d), F32)]
    if moe:
        scratch.append(pltpu.VMEM((tm, LANE), F32))
    return pl.pallas_call(
        kern,
        out_shape=jax.ShapeDtypeStruct((t, d), F32),
        grid=(t // tm, n_exp, ff // tf),
        in_specs=in_specs,
        out_specs=pl.BlockSpec((tm, d), row),
        scratch_shapes=scratch,
        compiler_params=_cparams(("parallel", "arbitrary", "arbitrary")),
        name="moe_ffn" if moe else "ffn",
    )(*args)


def _lane_row(vals, offset):
    n = vals.shape[0]
    return jnp.zeros((1, LANE), F32).at[0, offset:offset + n].set(vals.astype(F32))


def kernel(x_prompt, x_sample, state_delta, c, c_ctx, norm1, norm2, w_mod, b_mod, w_in, sc_w, sc_b, dn_a_log, dn_dt_bias, dn_norm, cv_dw_w, cv_dw_b, cv_ln_g, cv_ln_b, w_out, ffn_wg, ffn_wu, ffn_wd, moe_router, moe_wg, moe_wu, moe_wd, final_norm):
    b_ctx, l_ctx, d = x_prompt.shape
    b_lat, l_lat, _ = x_sample.shape
    depth = w_in.shape[0]
    t_ctx = b_ctx * l_ctx
    t_lat = b_lat * l_lat
    f_w = d // 4
    f_gw = f_w // F_GROUPS
    dn_w = d // 2
    dk = dn_w // DN_HEADS
    cv_w = d // 4
    qkvz_w = 4 * dn_w
    n_gate = 2 * N_DIRS * DN_HEADS
    n_exp = moe_wg.shape[1]
    assert dk == LANE and t_ctx % l_lat == 0 and b_lat + 1 <= MOD_ROWS
    assert l_ctx % CHUNK == 0 and l_lat % CHUNK == 0 and l_lat % GRID_W == 0 and CHUNK == GRID_W

    tm = _pick_tile(512, t_ctx, l_lat)
    tm_ffn = _pick_tile(1024, t_ctx, l_lat)

    def mod_row_fn(tile):
        n_ctx_tiles = t_ctx // tile
        per_seq = l_lat // tile
        return lambda i: jnp.where(i < n_ctx_tiles, 0, 1 + (i - n_ctx_tiles) // per_seq)

    x = jnp.concatenate([x_prompt.reshape(t_ctx, d), x_sample.reshape(t_lat, d)], axis=0)
    cvec = jnp.zeros((MOD_ROWS, d), F32).at[0].set(c_ctx).at[1:1 + b_lat].set(c)
    mods = _adaln(cvec, w_mod, b_mod).reshape(depth, MOD_ROWS, 1, N_MOD * d)

    cg, sg = _dft_tables(f_gw)
    eye_g = np.eye(F_GROUPS)
    ccs = jnp.asarray(np.concatenate([np.kron(eye_g, cg), np.kron(eye_g, sg)], axis=1), F32).astype(BF16)

    passes = ((l_ctx, b_ctx, 0, False), (l_lat, b_lat, t_ctx, True))
    o_q = f_w
    o_ba = f_w + qkvz_w
    o_cv = o_ba + n_gate
    new_states = []
    for l in range(depth):
        w = w_in[l]
        w_perm = jnp.concatenate(
            [w[:, :o_ba], w[:, o_cv:o_cv + 2 * cv_w], w[:, o_ba:o_cv], jnp.zeros((d, LANE - n_gate), F32)],
            axis=1).astype(BF16)
        alog = _lane_row(dn_a_log[l].reshape(-1), N_DIRS * DN_HEADS)
        dtb = _lane_row(dn_dt_bias[l].reshape(-1), N_DIRS * DN_HEADS)
        xc, xs, qkvz, cv, gates = _proj(x, norm1[l][None], mods[l], w_perm, ccs, alog, dtb,
                                        mod_row_fn(tm), tm, f_w, qkvz_w, 2 * cv_w)
        g16 = gates[:, :n_gate].reshape(-1, 2 * N_DIRS, DN_HEADS)
        gcol = g16.transpose(2, 0, 1)
        grow = g16.reshape(-1, CHUNK, 2 * N_DIRS, DN_HEADS).transpose(3, 0, 2, 1)

        parts = [[], [], []]
        for seq_len, n_batch, row_off, latent in passes:
            parts[0].append(_fourier(xc, xs, seq_len, n_batch, row_off, f_gw))
            res = _delta(qkvz, gcol, grow, sc_w[l], sc_b[l][None], dn_norm[l][None],
                         state_delta if latent else None, l, seq_len, n_batch, row_off, dk,
                         write_state=not latent)
            if latent:
                parts[1].append(res)
            else:
                parts[1].append(res[0])
                new_states.append(res[1])
            parts[2].append(_conformer(cv, cv_dw_w[l], cv_dw_b[l][None], cv_ln_g[l][None],
                                       cv_ln_b[l][None], seq_len, n_batch, row_off, latent))
        x = _outproj(parts, w_out[l].astype(BF16), x, mods[l], mod_row_fn(tm), tm, t_ctx // tm)

        fg = final_norm[None] if l == depth - 1 else None
        if l % 2 == 0:
            i = l // 2
            x = _ffn(x, norm2[l][None], mods[l], mod_row_fn(tm_ffn), ffn_wg[i][None], ffn_wu[i][None],
                     ffn_wd[i][None], None, fg, tm_ffn, _pick_tile(512, ffn_wg.shape[-1]))
        else:
            i = l // 2
            router = jnp.zeros((d, LANE), F32).at[:, :n_exp].set(moe_router[i])
            x = _ffn(x, norm2[l][None], mods[l], mod_row_fn(tm_ffn), moe_wg[i], moe_wu[i], moe_wd[i],
                     router, fg, tm_ffn, _pick_tile(512, moe_wg.shape[-1]))

    y_prompt = x[:t_ctx].reshape(b_ctx, l_ctx, d)
    y_sample = x[t_ctx:].reshape(b_lat, l_lat, d)
    return y_prompt, y_sample, jnp.stack(new_states, axis=1)
```

```python
import functools
import math

import numpy as np
import jax
import jax.numpy as jnp
from jax import lax
from jax.experimental import pallas as pl
from jax.experimental.pallas import tpu as pltpu

F32 = jnp.float32
BF16 = jnp.bfloat16
EPS = 1e-6

LANE = 128
SUBLANE = 8
VMEM_LIMIT = 56 * 1024 * 1024

CHUNK = 64
GRID_W = 64
F_GROUPS = 4
DN_HEADS = 4
N_DIRS = 2
PREP_UNROLL = 8
TOP_K = 2
N_MOD = 6
MOD_ROWS = 16


def _cparams(sem):
    return pltpu.CompilerParams(dimension_semantics=sem, vmem_limit_bytes=VMEM_LIMIT)


def _pick_tile(limit, *sizes):
    t = limit
    while any(s % t for s in sizes):
        t //= 2
    return t


def _dot(a, b):
    return jnp.dot(a, b, preferred_element_type=F32)


def _mm(a, b):
    return jnp.dot(a.astype(BF16), b.astype(BF16), preferred_element_type=F32)


def _mm_nt(a, b):
    return lax.dot_general(a.astype(BF16), b.astype(BF16), (((1,), (1,)), ((), ())),
                           preferred_element_type=F32)


def _mm_tn(a, b):
    return lax.dot_general(a.astype(BF16), b.astype(BF16), (((0,), (0,)), ((), ())),
                           preferred_element_type=F32)


def _split(a):
    hi = a.astype(BF16)
    lo = (a - hi.astype(F32)).astype(BF16)
    return hi, lo


def _mm3(a, b):
    ah, al = _split(a)
    bh, bl = _split(b)
    return _dot(ah, bh) + (_dot(ah, bl) + _dot(al, bh))


def _sigmoid(x):
    return 1.0 / (1.0 + jnp.exp(-x))


def _silu(x):
    return x * _sigmoid(x)


def _rms(x):
    return x * lax.rsqrt(jnp.mean(x * x, axis=-1, keepdims=True) + EPS)


def _adaln_kernel(c_ref, w_ref, b_ref, o_ref):
    o_ref[...] = _mm(_silu(c_ref[...]), w_ref[...]) + b_ref[...]


def _adaln(cvec, w_mod, b_mod):
    depth, d, n = w_mod.shape
    tn = _pick_tile(1024, n)
    return pl.pallas_call(
        _adaln_kernel,
        out_shape=jax.ShapeDtypeStruct((depth, MOD_ROWS, n), F32),
        grid=(depth, n // tn),
        in_specs=[pl.BlockSpec((MOD_ROWS, d), lambda l, j: (0, 0)),
                  pl.BlockSpec((None, d, tn), lambda l, j: (l, 0, j)),
                  pl.BlockSpec((None, 1, tn), lambda l, j: (l, 0, j))],
        out_specs=pl.BlockSpec((None, MOD_ROWS, tn), lambda l, j: (l, 0, j)),
        compiler_params=_cparams(("parallel", "parallel")),
        name="adaln",
    )(cvec, w_mod, b_mod.reshape(depth, 1, n))


def _proj_kernel(x_ref, g_ref, mod_ref, w_ref, ccs_ref, alog_ref, dtb_ref,
                 xc_o, xs_o, qkvz_o, cv_o, gate_o, *, d, f_w, qkvz_w, cv_w):
    x = x_ref[...]
    mod = mod_ref[...]
    h = _rms(x) * g_ref[...]
    h = (h * (1.0 + mod[:, d:2 * d]) + mod[:, 0:d]).astype(BF16)

    xf = _dot(h, w_ref[:, 0:f_w])
    xcs = _dot(xf.astype(BF16), ccs_ref[...])
    xc_o[...] = xcs[:, :f_w].astype(xc_o.dtype)
    xs_o[...] = xcs[:, f_w:].astype(xs_o.dtype)
    o = f_w
    step = 4 * LANE
    for n0 in range(0, qkvz_w, step):
        qkvz_o[:, n0:n0 + step] = _dot(h, w_ref[:, o + n0:o + n0 + step])
    o += qkvz_w
    cv_o[...] = _dot(h, w_ref[:, o:o + cv_w])
    o += cv_w
    ba = _dot(h, w_ref[:, o:o + LANE])

    tm = ba.shape[0]
    lane = lax.broadcasted_iota(jnp.int32, ba.shape, 1)
    row = lax.broadcasted_iota(jnp.int32, ba.shape, 0) % CHUNK
    beta = _sigmoid(ba)
    t = ba + dtb_ref[...]
    softplus = jnp.maximum(t, 0.0) + jnp.log1p(jnp.exp(-jnp.abs(t)))
    g = -jnp.exp(alog_ref[...]) * softplus
    cf = g
    cb = g
    s = 1
    while s < CHUNK:
        cf = cf + jnp.where(row >= s, pltpu.roll(cf, s, axis=0), 0.0)
        cb = cb + jnp.where(row < CHUNK - s, pltpu.roll(cb, tm - s, axis=0), 0.0)
        s *= 2
    n_beta = N_DIRS * DN_HEADS
    gate_o[...] = jnp.where(lane < n_beta, beta, jnp.where(lane < n_beta + DN_HEADS, cf, cb))


def _proj(x, gain, mod, w_perm, ccs, alog, dtb, mod_row, tm, f_w, qkvz_w, cv_w):
    t, d = x.shape
    n = w_perm.shape[1]
    kern = functools.partial(_proj_kernel, d=d, f_w=f_w, qkvz_w=qkvz_w, cv_w=cv_w)
    row = lambda i: (i, 0)
    const = lambda i: (0, 0)
    return pl.pallas_call(
        kern,
        out_shape=(jax.ShapeDtypeStruct((t, f_w), BF16), jax.ShapeDtypeStruct((t, f_w), BF16),
                   jax.ShapeDtypeStruct((t, qkvz_w), F32), jax.ShapeDtypeStruct((t, cv_w), F32),
                   jax.ShapeDtypeStruct((t, LANE), F32)),
        grid=(t // tm,),
        in_specs=[pl.BlockSpec((tm, d), row),
                  pl.BlockSpec((1, d), const),
                  pl.BlockSpec((None, 1, N_MOD * d), lambda i: (mod_row(i), 0, 0)),
                  pl.BlockSpec((d, n), const),
                  pl.BlockSpec(ccs.shape, const),
                  pl.BlockSpec((1, LANE), const),
                  pl.BlockSpec((1, LANE), const)],
        out_specs=(pl.BlockSpec((tm, f_w), row), pl.BlockSpec((tm, f_w), row),
                   pl.BlockSpec((tm, qkvz_w), row), pl.BlockSpec((tm, cv_w), row),
                   pl.BlockSpec((tm, LANE), row)),
        compiler_params=_cparams(("parallel",)),
        name="proj",
    )(x, gain, mod, w_perm, ccs, alog, dtb)


def _fourier_kernel(xc_ref, xs_ref, cn_ref, sn_ref, o_ref, *, scale):
    y = _dot(cn_ref[...], xc_ref[...]) - _dot(sn_ref[...], xs_ref[...])
    o_ref[...] = (y * scale).astype(o_ref.dtype)


def _dft_tables(n):
    j = np.arange(n, dtype=np.int64)
    ang = (2.0 * np.pi / n) * ((j[:, None] * j[None, :]) % n).astype(np.float64)
    return np.cos(ang), np.sin(ang)


def _fourier(xc, xs, seq_len, n_batch, row_off, f_gw):
    f_w = xc.shape[1]
    cn, sn = _dft_tables(seq_len)
    cn = jnp.asarray(cn, F32).astype(BF16)
    sn = jnp.asarray(sn, F32).astype(BF16)
    tr = _pick_tile(1024, seq_len)
    nt = seq_len // tr
    blk0 = row_off // seq_len
    kern = functools.partial(_fourier_kernel, scale=1.0 / math.sqrt(seq_len * f_gw))
    return pl.pallas_call(
        kern,
        out_shape=jax.ShapeDtypeStruct((n_batch * seq_len, f_w), BF16),
        grid=(nt, n_batch),
        in_specs=[pl.BlockSpec((seq_len, f_w), lambda i, b: (blk0 + b, 0)),
                  pl.BlockSpec((seq_len, f_w), lambda i, b: (blk0 + b, 0)),
                  pl.BlockSpec((tr, seq_len), lambda i, b: (i, 0)),
                  pl.BlockSpec((tr, seq_len), lambda i, b: (i, 0))],
        out_specs=pl.BlockSpec((tr, f_w), lambda i, b: (b * nt + i, 0)),
        compiler_params=_cparams(("parallel", "parallel")),
        name="fourier",
    )(xc, xs, cn, sn)


def _unit_tri_inverse(mats, m16, m32off, m64off, eye):
    ps = [-(a * m16) for a in mats]
    ts = [eye + p for p in ps]
    for _ in range(3):
        ps = [_mm(p, p) for p in ps]
        ts = [t + _mm(t, p) for t, p in zip(ts, ps)]
    for off in (m32off, m64off):
        us = [_mm(t, a * off) for t, a in zip(ts, mats)]
        ts = [t - _mm(u, t) for t, u in zip(ts, us)]
    return ts


def _delta_kernel(*refs, seq_len, dk, zero_init, write_state):
    (q_ref, k_ref, v_ref, z_ref, gcol_ref, grow_ref,
     wq_ref, wk_ref, wv_ref, bq_ref, bk_ref, bv_ref, og_ref) = refs[:13]
    pos = 13
    if not zero_init:
        s0_ref = refs[pos]
        pos += 1
    y_ref = refs[pos]
    pos += 1
    if write_state:
        sfin_ref = refs[pos]
        pos += 1
    qs, ks, vs, of_s, ob_s, st_s, pq_s, n_s = refs[pos:]

    n_chunks = seq_len // CHUNK
    rows = lax.broadcasted_iota(jnp.int32, (seq_len, dk), 0)

    def conv_silu(x_ref, w_ref, b_ref):
        x = x_ref[...]
        width = w_ref.shape[0]
        pad = (width - 1) // 2
        acc = jnp.zeros_like(x) + b_ref[...]
        for s in range(width):
            o = s - pad
            if o == 0:
                xs = x
            else:
                xs = pltpu.roll(x, (-o) % seq_len, axis=0)
                xs = jnp.where((rows + o >= 0) & (rows + o < seq_len), xs, 0.0)
            acc = acc + xs * w_ref[s:s + 1, :]
        return _silu(acc)

    def l2norm(x):
        return x * lax.rsqrt(jnp.sum(x * x, axis=-1, keepdims=True) + EPS)

    qs[...] = l2norm(conv_silu(q_ref, wq_ref, bq_ref)) * (dk ** -0.5)
    ks[...] = l2norm(conv_silu(k_ref, wk_ref, bk_ref))
    vs[...] = conv_silu(v_ref, wv_ref, bv_ref)
    if zero_init:
        st_s[...] = jnp.zeros_like(st_s)
    else:
        st_s[...] = s0_ref[...]

    ri = lax.broadcasted_iota(jnp.int32, (CHUNK, CHUNK), 0)
    ci = lax.broadcasted_iota(jnp.int32, (CHUNK, CHUNK), 1)
    eye = (ri == ci).astype(F32)
    same16 = (ri // 16) == (ci // 16)
    same32 = (ri // 32) == (ci // 32)
    m16 = same16.astype(F32)
    m32off = same32.astype(F32) - m16
    m64off = 1.0 - same32.astype(F32)
    incl = (ri >= ci, ri <= ci)
    strict = (ri > ci, ri < ci)
    n_beta = N_DIRS

    o_s = (of_s, ob_s)
    last_row = (CHUNK - 1, 0)

    def prep_group(first_chunk, chunks):
        chains = [(c, d) for c in chunks for d in range(N_DIRS)]
        cidx = {c: first_chunk + c for c in chunks}
        r0 = {c: pl.multiple_of(cidx[c] * CHUNK, CHUNK) for c in chunks}
        q = {c: qs[pl.ds(r0[c], CHUNK), :] for c in chunks}
        k = {c: ks[pl.ds(r0[c], CHUNK), :] for c in chunks}
        v = {c: vs[pl.ds(r0[c], CHUNK), :] for c in chunks}
        gc4 = {c: gcol_ref[pl.ds(r0[c], CHUNK), :] for c in chunks}
        gr4 = {c: grow_ref[cidx[c]] for c in chunks}
        kq = {}
        for c in chunks:
            k16 = k[c].astype(BF16)
            kq[c] = _mm_nt(jnp.concatenate([k16, q[c].astype(BF16)], axis=0), k16)
        beta, gcl, decay, a = {}, {}, {}, []
        for c, d in chains:
            beta[c, d] = gc4[c][:, d:d + 1]
            gcl[c, d] = gc4[c][:, n_beta + d:n_beta + d + 1]
            grw = gr4[c][n_beta + d:n_beta + d + 1, :]
            decay[c, d] = jnp.exp(jnp.where(incl[d], gcl[c, d] - grw, -jnp.inf))
            a.append(jnp.where(strict[d], beta[c, d] * kq[c][:CHUNK] * decay[c, d], 0.0))
        t = dict(zip(chains, _unit_tri_inverse(a, m16, m32off, m64off, eye)))
        eg = {cd: jnp.exp(gcl[cd]) for cd in chains}
        sol = {(c, d): _mm(t[c, d], jnp.concatenate([v[c] * beta[c, d], k[c] * (beta[c, d] * eg[c, d])], axis=1))
               for c, d in chains}
        aw = {(c, d): _mm(kq[c][CHUNK:] * decay[c, d], sol[c, d]) for c, d in chains}
        kuw = {}
        for c, d in chains:
            g_last = gcl[c, d][last_row[d]:last_row[d] + 1, :]
            kuw[c, d] = _mm_tn(k[c] * jnp.exp(g_last - gcl[c, d]), sol[c, d])
        for c, d in chains:
            pq_s[d, cidx[c]] = jnp.concatenate(
                [kuw[c, d][:, dk:], q[c] * eg[c, d] - aw[c, d][:, dk:]], axis=0).astype(BF16)
            n_s[d, cidx[c]] = kuw[c, d][:, :dk]
            o_s[d][pl.ds(r0[c], CHUNK), :] = aw[c, d][:, :dk]

    unroll = math.gcd(PREP_UNROLL, n_chunks)

    def prep_body(i, carry):
        prep_group(i * unroll, range(unroll))
        return carry

    lax.fori_loop(0, n_chunks // unroll, prep_body, 0)

    def scan_body(n, carry):
        for d in range(N_DIRS):
            c = n if d == 0 else n_chunks - 1 - n
            r0 = pl.multiple_of(c * CHUNK, CHUNK)
            g_last = grow_ref[c][n_beta + d:n_beta + d + 1, last_row[d]:last_row[d] + 1]
            s = st_s[d]
            r = _dot(pq_s[d, c], s.astype(BF16))
            st_s[d] = s * jnp.exp(g_last) + n_s[d, c] - r[:dk]
            o_s[d][pl.ds(r0, CHUNK), :] += r[dk:]
        return carry

    lax.fori_loop(0, n_chunks, scan_body, 0)

    o = _rms(of_s[...] + ob_s[...]) * og_ref[...]
    y_ref[...] = (o * _silu(z_ref[...])).astype(y_ref.dtype)
    if write_state:
        sfin_ref[...] = st_s[...]


def _delta(qkvz, gcol, grow, sc_w, sc_b, o_g, s0, layer, seq_len, n_batch, row_off, dk, write_state):
    nh = DN_HEADS
    blk0 = row_off // seq_len
    n_chunks = seq_len // CHUNK
    zero_init = s0 is None
    kern = functools.partial(_delta_kernel, seq_len=seq_len, dk=dk, zero_init=zero_init,
                             write_state=write_state)

    def col(group):
        return pl.BlockSpec((seq_len, dk), lambda b, h: (blk0 + b, group * nh + h))

    def wspec(group, rows):
        return pl.BlockSpec((rows, dk), lambda b, h: (0, group * nh + h))

    width = sc_w.shape[0]
    in_specs = [col(0), col(1), col(2), col(3),
                pl.BlockSpec((None, seq_len, 2 * N_DIRS), lambda b, h: (h, blk0 + b, 0)),
                pl.BlockSpec((None, n_chunks, 2 * N_DIRS, CHUNK), lambda b, h: (h, blk0 + b, 0, 0)),
                wspec(0, width), wspec(1, width), wspec(2, width),
                wspec(0, 1), wspec(1, 1), wspec(2, 1),
                pl.BlockSpec((1, dk), lambda b, h: (0, 0))]
    args = [qkvz, qkvz, qkvz, qkvz, gcol, grow, sc_w, sc_w, sc_w, sc_b, sc_b, sc_b, o_g]
    if not zero_init:
        in_specs.append(pl.BlockSpec((None, None, N_DIRS, None, dk, dk),
                                     lambda b, h: (b, layer, 0, h, 0, 0)))
        args.append(s0)
    y_shape = jax.ShapeDtypeStruct((n_batch * seq_len, nh * dk), BF16)
    y_spec = pl.BlockSpec((seq_len, dk), lambda b, h: (b, h))
    if write_state:
        out_shape = (y_shape, jax.ShapeDtypeStruct((n_batch, N_DIRS, nh, dk, dk), F32))
        out_specs = (y_spec, pl.BlockSpec((None, N_DIRS, None, dk, dk), lambda b, h: (b, 0, h, 0, 0)))
    else:
        out_shape = y_shape
        out_specs = y_spec
    return pl.pallas_call(
        kern,
        out_shape=out_shape,
        grid=(n_batch, nh),
        in_specs=in_specs,
        out_specs=out_specs,
        scratch_shapes=[pltpu.VMEM((seq_len, dk), F32)] * 5 + [
            pltpu.VMEM((N_DIRS, dk, dk), F32),
            pltpu.VMEM((N_DIRS, n_chunks, dk + CHUNK, dk), BF16),
            pltpu.VMEM((N_DIRS, n_chunks, dk, dk), F32)],
        compiler_params=_cparams(("parallel", "parallel")),
        name="delta",
    )(*args)


def _conformer_kernel(x_ref, w_ref, b_ref, lg_ref, lb_ref, o_ref, pad_s, *, n_seg, seg_blocks, halo):
    c = o_ref.shape[-1]
    width = w_ref.shape[0]
    pad = (width - 1) // 2
    seg_len = seg_blocks * GRID_W
    pad_s[:, 0:halo, :] = jnp.zeros((n_seg, halo, c), F32)
    pad_s[:, halo + seg_len:2 * halo + seg_len, :] = jnp.zeros((n_seg, halo, c), F32)
    for j in range(seg_blocks):
        if seg_blocks == 1:
            x = x_ref[...]
            pad_s[:, halo:halo + GRID_W, :] = x[..., :c] * _sigmoid(x[..., c:])
        else:
            x = x_ref[j]
            pad_s[0, halo + j * GRID_W:halo + (j + 1) * GRID_W, :] = x[:, :c] * _sigmoid(x[:, c:])

    def seg_body(r, carry):
        for j in range(seg_blocks):
            acc = jnp.zeros((GRID_W, c), F32) + b_ref[...]
            for s in range(width):
                start = halo - pad + j * GRID_W + s
                acc = acc + pad_s[r, pl.ds(start, GRID_W), :] * w_ref[s:s + 1, :]
            mu = jnp.mean(acc, axis=-1, keepdims=True)
            xc = acc - mu
            var = jnp.mean(xc * xc, axis=-1, keepdims=True)
            y = xc * lax.rsqrt(var + EPS) * lg_ref[...] + lb_ref[...]
            o_ref[r * seg_blocks + j] = _silu(y).astype(o_ref.dtype)
        return carry

    lax.fori_loop(0, n_seg, seg_body, 0)


def _conformer(cv, dw_w, dw_b, ln_g, ln_b, seq_len, n_batch, row_off, latent):
    t, c2 = cv.shape
    c = c2 // 2
    cv3 = cv.reshape(t // GRID_W, GRID_W, c2)
    blocks = seq_len // GRID_W
    n_seg, seg_blocks = (blocks, 1) if latent else (1, blocks)
    halo = 2 * SUBLANE
    assert (dw_w.shape[0] - 1) // 2 <= halo
    blk0 = row_off // seq_len
    kern = functools.partial(_conformer_kernel, n_seg=n_seg, seg_blocks=seg_blocks, halo=halo)
    const = lambda b: (0, 0)
    out = pl.pallas_call(
        kern,
        out_shape=jax.ShapeDtypeStruct((n_batch * blocks, GRID_W, c), BF16),
        grid=(n_batch,),
        in_specs=[pl.BlockSpec((blocks, GRID_W, c2), lambda b: (blk0 + b, 0, 0)),
                  pl.BlockSpec(dw_w.shape, const),
                  pl.BlockSpec((1, c), const), pl.BlockSpec((1, c), const), pl.BlockSpec((1, c), const)],
        out_specs=pl.BlockSpec((blocks, GRID_W, c), lambda b: (b, 0, 0)),
        scratch_shapes=[pltpu.VMEM((n_seg, seg_blocks * GRID_W + 2 * halo, c), F32)],
        compiler_params=_cparams(("parallel",)),
        name="conformer",
    )(cv3, dw_w, dw_b, ln_g, ln_b)
    return out.reshape(n_batch * seq_len, c)


def _outproj_kernel(yfc, yfl, ydc, ydl, ycc, ycl, w_ref, x_ref, mod_ref, o_ref, *, d, n_ctx_tiles):
    is_ctx = pl.program_id(0) < n_ctx_tiles
    y = None
    k0 = 0
    for a_c, a_l in ((yfc, yfl), (ydc, ydl), (ycc, ycl)):
        kw = a_c.shape[1]
        a = jnp.where(is_ctx, a_c[...], a_l[...])
        part = _dot(a, w_ref[k0:k0 + kw, :])
        y = part if y is None else y + part
        k0 += kw
    o_ref[...] = x_ref[...] + mod_ref[:, 2 * d:3 * d] * y


def _outproj(parts, w_out, x, mod, mod_row, tm, n_ctx_tiles):
    t, d = x.shape
    kern = functools.partial(_outproj_kernel, d=d, n_ctx_tiles=n_ctx_tiles)
    in_specs = []
    args = []
    for a_c, a_l in parts:
        kw = a_c.shape[1]
        in_specs.append(pl.BlockSpec((tm, kw), lambda i: (jnp.minimum(i, n_ctx_tiles - 1), 0)))
        in_specs.append(pl.BlockSpec((tm, kw), lambda i: (jnp.maximum(i - n_ctx_tiles, 0), 0)))
        args += [a_c, a_l]
    in_specs += [pl.BlockSpec(w_out.shape, lambda i: (0, 0)),
                 pl.BlockSpec((tm, d), lambda i: (i, 0)),
                 pl.BlockSpec((None, 1, N_MOD * d), lambda i: (mod_row(i), 0, 0))]
    args += [w_out, x, mod]
    return pl.pallas_call(
        kern,
        out_shape=jax.ShapeDtypeStruct((t, d), F32),
        grid=(t // tm,),
        in_specs=in_specs,
        out_specs=pl.BlockSpec((tm, d), lambda i: (i, 0)),
        compiler_params=_cparams(("parallel",)),
        name="outproj",
    )(*args)


def _ffn_kernel(*refs, d, n_exp, moe, final_norm):
    x_ref, g_ref, mod_ref = refs[:3]
    pos = 3
    if moe:
        router_ref = refs[pos]
        pos += 1
    wg_ref, wu_ref, wd_ref = refs[pos:pos + 3]
    pos += 3
    if final_norm:
        fg_ref = refs[pos]
        pos += 1
    o_ref = refs[pos]
    pos += 1
    h_s, acc_s = refs[pos:pos + 2]
    pos += 2
    if moe:
        gate_s = refs[pos]

    e = pl.program_id(1)
    f = pl.program_id(2)

    @pl.when((e == 0) & (f == 0))
    def _():
        mod = mod_ref[...]
        h = _rms(x_ref[...]) * g_ref[...]
        h = h * (1.0 + mod[:, 4 * d:5 * d]) + mod[:, 3 * d:4 * d]
        h_s[...] = h.astype(BF16)
        acc_s[...] = jnp.zeros_like(acc_s)
        if moe:
            logits = _mm3(h, router_ref[...])
            lane = lax.broadcasted_iota(jnp.int32, logits.shape, 1)
            lg = jnp.where(lane < n_exp, logits, -jnp.inf)
            m1 = jnp.max(lg, axis=-1, keepdims=True)
            i1 = jnp.min(jnp.where(lg == m1, lane, LANE), axis=-1, keepdims=True)
            lg2 = jnp.where(lane == i1, -jnp.inf, lg)
            m2 = jnp.max(lg2, axis=-1, keepdims=True)
            i2 = jnp.min(jnp.where(lg2 == m2, lane, LANE), axis=-1, keepdims=True)
            w1 = 1.0 / (1.0 + jnp.exp(m2 - m1))
            gate_s[...] = jnp.where(lane == i1, w1, 0.0) + jnp.where(lane == i2, 1.0 - w1, 0.0)

    h = h_s[...]
    a = _silu(_dot(h, wg_ref[...].astype(BF16))) * _dot(h, wu_ref[...].astype(BF16))
    if moe:
        gate = gate_s[...]
        lane = lax.broadcasted_iota(jnp.int32, gate.shape, 1)
        a = a * jnp.sum(jnp.where(lane == e, gate, 0.0), axis=-1, keepdims=True)
    acc_s[...] += _dot(a.astype(BF16), wd_ref[...].astype(BF16))

    @pl.when((e == n_exp - 1) & (f == pl.num_programs(2) - 1))
    def _():
        o = x_ref[...] + mod_ref[:, 5 * d:6 * d] * acc_s[...]
        if final_norm:
            o = _rms(o) * fg_ref[...]
        o_ref[...] = o


def _ffn(x, gain, mod, mod_row, wg, wu, wd, router, final_gain, tm, tf):
    t, d = x.shape
    n_exp, _, ff = wg.shape
    moe = router is not None
    final_norm = final_gain is not None
    kern = functools.partial(_ffn_kernel, d=d, n_exp=n_exp, moe=moe, final_norm=final_norm)
    row = lambda i, e, f: (i, 0)
    const = lambda i, e, f: (0, 0)
    in_specs = [pl.BlockSpec((tm, d), row),
                pl.BlockSpec((1, d), const),
                pl.BlockSpec((None, 1, N_MOD * d), lambda i, e, f: (mod_row(i), 0, 0))]
    args = [x, gain, mod]
    if moe:
        in_specs.append(pl.BlockSpec(router.shape, const))
        args.append(router)
    in_specs += [pl.BlockSpec((None, d, tf), lambda i, e, f: (e, 0, f)),
                 pl.BlockSpec((None, d, tf), lambda i, e, f: (e, 0, f)),
                 pl.BlockSpec((None, tf, d), lambda i, e, f: (e, f, 0))]
    args += [wg, wu, wd]
    if final_norm:
        in_specs.append(pl.BlockSpec((1, d), const))
        args.append(final_gain)
    scratch = [pltpu.VMEM((tm, d), BF16), pltpu.VMEM((tm, d), F32)]
    if moe:
        scratch.append(pltpu.VMEM((tm, LANE), F32))
    return pl.pallas_call(
        kern,
        out_shape=jax.ShapeDtypeStruct((t, d), F32),
        grid=(t // tm, n_exp, ff // tf),
        in_specs=in_specs,
        out_specs=pl.BlockSpec((tm, d), row),
        scratch_shapes=scratch,
        compiler_params=_cparams(("parallel", "arbitrary", "arbitrary")),
        name="moe_ffn" if moe else "ffn",
    )(*args)


INFO_E, INFO_W, INFO_R = 0, 2, 4
MOE_TILE = 1024
MOE_SUB = 256


def _route_kernel(x_ref, g_ref, mod_ref, router_ref, h_o, info_o, cnt_o, cnt_s, *, d, n_exp):
    i = pl.program_id(0)

    @pl.when(i == 0)
    def _():
        cnt_s[...] = jnp.zeros_like(cnt_s)

    mod = mod_ref[...]
    h = _rms(x_ref[...]) * g_ref[...]
    h = h * (1.0 + mod[:, 4 * d:5 * d]) + mod[:, 3 * d:4 * d]
    h_o[...] = h

    logits = _mm3(h, router_ref[...])
    tm = logits.shape[0]
    lane = lax.broadcasted_iota(jnp.int32, logits.shape, 1)
    lg = jnp.where(lane < n_exp, logits, -jnp.inf)
    m1 = jnp.max(lg, axis=-1, keepdims=True)
    i1 = jnp.min(jnp.where(lg == m1, lane, LANE), axis=-1, keepdims=True)
    lg2 = jnp.where(lane == i1, -jnp.inf, lg)
    m2 = jnp.max(lg2, axis=-1, keepdims=True)
    i2 = jnp.min(jnp.where(lg2 == m2, lane, LANE), axis=-1, keepdims=True)
    w1 = 1.0 / (1.0 + jnp.exp(m2 - m1))

    sel1 = lane == i1
    sel2 = lane == i2
    member = jnp.where(sel1 | sel2, 1.0, 0.0)
    ri = lax.broadcasted_iota(jnp.int32, (tm, tm), 0)
    ci = lax.broadcasted_iota(jnp.int32, (tm, tm), 1)
    before = jnp.where(ri > ci, 1.0, 0.0)
    rank = cnt_s[...] + _mm(before, member)
    r1 = jnp.sum(jnp.where(sel1, rank, 0.0), axis=-1, keepdims=True)
    r2 = jnp.sum(jnp.where(sel2, rank, 0.0), axis=-1, keepdims=True)
    cnt_s[...] += jnp.sum(member, axis=0, keepdims=True)
    cnt_o[...] = jnp.broadcast_to(cnt_s[...], cnt_o.shape)

    info = jnp.zeros(logits.shape, F32)
    for ln, val in ((INFO_E, i1.astype(F32)), (INFO_E + 1, i2.astype(F32)), (INFO_W, w1),
                    (INFO_W + 1, 1.0 - w1), (INFO_R, r1), (INFO_R + 1, r2)):
        info = jnp.where(lane == ln, val, info)
    info_o[...] = info


def _route(x, gain, mod, mod_row, router, tm):
    t, d = x.shape
    n_exp = router.shape[1]
    router_p = jnp.zeros((d, LANE), F32).at[:, :n_exp].set(router)
    kern = functools.partial(_route_kernel, d=d, n_exp=n_exp)
    return pl.pallas_call(
        kern,
        out_shape=(jax.ShapeDtypeStruct((t, d), F32), jax.ShapeDtypeStruct((t, LANE), F32),
                   jax.ShapeDtypeStruct((SUBLANE, LANE), F32)),
        grid=(t // tm,),
        in_specs=[pl.BlockSpec((tm, d), lambda i: (i, 0)),
                  pl.BlockSpec((1, d), lambda i: (0, 0)),
                  pl.BlockSpec((None, 1, N_MOD * d), lambda i: (mod_row(i), 0, 0)),
                  pl.BlockSpec((d, LANE), lambda i: (0, 0))],
        out_specs=(pl.BlockSpec((tm, d), lambda i: (i, 0)), pl.BlockSpec((tm, LANE), lambda i: (i, 0)),
                   pl.BlockSpec((SUBLANE, LANE), lambda i: (0, 0))),
        scratch_shapes=[pltpu.VMEM((1, LANE), F32)],
        compiler_params=_cparams(("arbitrary",)),
        name="moe_route",
    )(x, gain, mod, router_p)


def _row_copy(src_hbm, row, dst, slot, sem):
    return pltpu.make_async_copy(src_hbm.at[pl.ds(row, 1), :], dst.at[pl.ds(slot, 1), :], sem)


def _experts_kernel(expert_sm, nvalid_sm, tok_ref, h_hbm, wg_ref, wu_ref, wd_ref, y_ref,
                    hbuf, h16, acc, wg16, wu16, wd16, sem, *, sub):
    del expert_sm
    i = pl.program_id(0)
    j = pl.program_id(1)
    tm = hbuf.shape[0]
    n_sub = (nvalid_sm[i] + sub - 1) // sub

    @pl.when(j == 0)
    def _():
        def issue(r, carry):
            _row_copy(h_hbm, tok_ref[0, r], hbuf, r, sem).start()
            return carry

        lax.fori_loop(0, n_sub * sub, issue, 0)
        acc[...] = jnp.zeros_like(acc)

        def wait(r, carry):
            _row_copy(h_hbm, 0, hbuf, r, sem).wait()
            return carry

        lax.fori_loop(0, n_sub * sub, wait, 0)

        def cast(s, carry):
            rows = pl.ds(pl.multiple_of(s * sub, sub), sub)
            h16[rows, :] = hbuf[rows, :].astype(BF16)
            return carry

        lax.fori_loop(0, n_sub, cast, 0)

    @pl.when(n_sub > 0)
    def _():
        wg16[...] = wg_ref[...].astype(BF16)
        wu16[...] = wu_ref[...].astype(BF16)
        wd16[...] = wd_ref[...].astype(BF16)

        def block(s, carry):
            rows = pl.ds(pl.multiple_of(s * sub, sub), sub)
            h = h16[rows, :]
            a = _silu(_dot(h, wg16[...])) * _dot(h, wu16[...])
            acc[rows, :] += _dot(a.astype(BF16), wd16[...])
            return carry

        lax.fori_loop(0, n_sub, block, 0)

    @pl.when(j == pl.num_programs(1) - 1)
    def _():
        y_ref[...] = acc[...]


def _experts(h, src_tok, tile_expert, tile_nvalid, wg, wu, wd, tm, tf, sub):
    t, d = h.shape
    n_exp, _, ff = wg.shape
    n_tiles = tile_expert.shape[0]
    n_f = ff // tf

    def w_idx(i, j, expert_sm, nvalid_sm):
        return expert_sm[i], jnp.where(nvalid_sm[i] > 0, j, n_f - 1)

    def wgu_map(i, j, expert_sm, nvalid_sm):
        e, jj = w_idx(i, j, expert_sm, nvalid_sm)
        return e, 0, jj

    def wd_map(i, j, expert_sm, nvalid_sm):
        e, jj = w_idx(i, j, expert_sm, nvalid_sm)
        return e, jj, 0

    kern = functools.partial(_experts_kernel, sub=sub)
    grid_spec = pltpu.PrefetchScalarGridSpec(
        num_scalar_prefetch=2,
        grid=(n_tiles, n_f),
        in_specs=[pl.BlockSpec((None, 1, tm), lambda i, j, *_: (i, 0, 0), memory_space=pltpu.SMEM),
                  pl.BlockSpec(memory_space=pl.ANY),
                  pl.BlockSpec((None, d, tf), wgu_map),
                  pl.BlockSpec((None, d, tf), wgu_map),
                  pl.BlockSpec((None, tf, d), wd_map)],
        out_specs=pl.BlockSpec((tm, d), lambda i, j, *_: (i, 0)),
        scratch_shapes=[pltpu.VMEM((tm, d), F32), pltpu.VMEM((tm, d), BF16), pltpu.VMEM((tm, d), F32),
                        pltpu.VMEM((d, tf), BF16), pltpu.VMEM((d, tf), BF16), pltpu.VMEM((tf, d), BF16),
                        pltpu.SemaphoreType.DMA(())])
    return pl.pallas_call(
        kern,
        out_shape=jax.ShapeDtypeStruct((n_tiles * tm, d), F32),
        grid_spec=grid_spec,
        compiler_params=_cparams(("arbitrary", "arbitrary")),
        name="moe_experts",
    )(tile_expert, tile_nvalid, src_tok.reshape(n_tiles, 1, tm), h, wg, wu, wd)


def _combine_kernel(*refs, d, final_norm):
    pos_ref, info_ref, x_ref, mod_ref = refs[:4]
    p = 4
    if final_norm:
        fg_ref = refs[p]
        p += 1
    y_hbm, o_ref, ybuf, sem = refs[p:]
    tm = x_ref.shape[0]

    def issue(r, carry):
        for k in range(TOP_K):
            _row_copy(y_hbm, pos_ref[0, k * tm + r], ybuf.at[k], r, sem).start()
        return carry

    lax.fori_loop(0, tm, issue, 0)

    def wait(r, carry):
        for k in range(TOP_K):
            _row_copy(y_hbm, 0, ybuf.at[k], r, sem).wait()
        return carry

    lax.fori_loop(0, tm, wait, 0)

    info = info_ref[...]
    f = None
    for k in range(TOP_K):
        term = info[:, INFO_W + k:INFO_W + k + 1] * ybuf[k]
        f = term if f is None else f + term
    o = x_ref[...] + mod_ref[:, 5 * d:6 * d] * f
    if final_norm:
        o = _rms(o) * fg_ref[...]
    o_ref[...] = o


def _combine(x, info, pos, y, mod, mod_row, final_gain, tm):
    t, d = x.shape
    final_norm = final_gain is not None
    kern = functools.partial(_combine_kernel, d=d, final_norm=final_norm)
    in_specs = [pl.BlockSpec((None, 1, TOP_K * tm), lambda i: (i, 0, 0), memory_space=pltpu.SMEM),
                pl.BlockSpec((tm, LANE), lambda i: (i, 0)),
                pl.BlockSpec((tm, d), lambda i: (i, 0)),
                pl.BlockSpec((None, 1, N_MOD * d), lambda i: (mod_row(i), 0, 0))]
    args = [pos, info, x, mod]
    if final_norm:
        in_specs.append(pl.BlockSpec((1, d), lambda i: (0, 0)))
        args.append(final_gain)
    in_specs.append(pl.BlockSpec(memory_space=pl.ANY))
    args.append(y)
    return pl.pallas_call(
        kern,
        out_shape=jax.ShapeDtypeStruct((t, d), F32),
        grid=(t // tm,),
        in_specs=in_specs,
        out_specs=pl.BlockSpec((tm, d), lambda i: (i, 0)),
        scratch_shapes=[pltpu.VMEM((TOP_K, tm, d), F32), pltpu.SemaphoreType.DMA(())],
        compiler_params=_cparams(("arbitrary",)),
        name="moe_combine",
    )(*args)


def _moe(x, gain, mod, mod_row_fn, router, wg, wu, wd, final_gain, tm_route, tm_exp, tm_comb, tf, sub):
    t, d = x.shape
    n_exp = wg.shape[0]
    h, info, cnt = _route(x, gain, mod, mod_row_fn(tm_route), router, tm_route)

    counts = cnt[0, :n_exp].astype(jnp.int32)
    tiles_per = (counts + tm_exp - 1) // tm_exp
    tile_end = jnp.cumsum(tiles_per)
    tile_start = tile_end - tiles_per
    n_tiles = (TOP_K * t) // tm_exp + n_exp
    tile_id = jnp.arange(n_tiles, dtype=jnp.int32)
    tile_expert = jnp.minimum(jnp.sum(tile_id[:, None] >= tile_end[None, :], axis=1), n_exp - 1).astype(jnp.int32)
    tile_nvalid = jnp.where(tile_id < tile_end[n_exp - 1],
                            jnp.clip(counts[tile_expert] - (tile_id - tile_start[tile_expert]) * tm_exp, 0, tm_exp),
                            0).astype(jnp.int32)
    slot_start = tile_start * tm_exp
    experts = info[:, INFO_E:INFO_E + TOP_K].astype(jnp.int32)
    slots = slot_start[experts] + info[:, INFO_R:INFO_R + TOP_K].astype(jnp.int32)
    token = jnp.broadcast_to(jnp.arange(t, dtype=jnp.int32)[:, None], slots.shape)
    src_tok = jnp.zeros((n_tiles * tm_exp,), jnp.int32).at[slots.reshape(-1)].set(token.reshape(-1))

    y = _experts(h, src_tok, tile_expert, tile_nvalid, wg, wu, wd, tm_exp, tf, sub)
    pos = slots.reshape(t // tm_comb, tm_comb, TOP_K).transpose(0, 2, 1).reshape(t // tm_comb, 1, TOP_K * tm_comb)
    return _combine(x, info, pos, y, mod, mod_row_fn(tm_comb), final_gain, tm_comb)


def _lane_row(vals, offset):
    n = vals.shape[0]
    return jnp.zeros((1, LANE), F32).at[0, offset:offset + n].set(vals.astype(F32))


def kernel(x_prompt, x_sample, state_delta, c, c_ctx, norm1, norm2, w_mod, b_mod, w_in, sc_w, sc_b, dn_a_log, dn_dt_bias, dn_norm, cv_dw_w, cv_dw_b, cv_ln_g, cv_ln_b, w_out, ffn_wg, ffn_wu, ffn_wd, moe_router, moe_wg, moe_wu, moe_wd, final_norm):
    b_ctx, l_ctx, d = x_prompt.shape
    b_lat, l_lat, _ = x_sample.shape
    depth = w_in.shape[0]
    t_ctx = b_ctx * l_ctx
    t_lat = b_lat * l_lat
    f_w = d // 4
    f_gw = f_w // F_GROUPS
    dn_w = d // 2
    dk = dn_w // DN_HEADS
    cv_w = d // 4
    qkvz_w = 4 * dn_w
    n_gate = 2 * N_DIRS * DN_HEADS
    n_exp = moe_wg.shape[1]
    assert dk == LANE and t_ctx % l_lat == 0 and b_lat + 1 <= MOD_ROWS
    assert l_ctx % CHUNK == 0 and l_lat % CHUNK == 0 and l_lat % GRID_W == 0 and CHUNK == GRID_W

    tm = _pick_tile(512, t_ctx, l_lat)
    tm_ffn = _pick_tile(1024, t_ctx, l_lat)

    def mod_row_fn(tile):
        n_ctx_tiles = t_ctx // tile
        per_seq = l_lat // tile
        return lambda i: jnp.where(i < n_ctx_tiles, 0, 1 + (i - n_ctx_tiles) // per_seq)

    x = jnp.concatenate([x_prompt.reshape(t_ctx, d), x_sample.reshape(t_lat, d)], axis=0)
    cvec = jnp.zeros((MOD_ROWS, d), F32).at[0].set(c_ctx).at[1:1 + b_lat].set(c)
    mods = _adaln(cvec, w_mod, b_mod).reshape(depth, MOD_ROWS, 1, N_MOD * d)

    cg, sg = _dft_tables(f_gw)
    eye_g = np.eye(F_GROUPS)
    ccs = jnp.asarray(np.concatenate([np.kron(eye_g, cg), np.kron(eye_g, sg)], axis=1), F32).astype(BF16)

    passes = ((l_ctx, b_ctx, 0, False), (l_lat, b_lat, t_ctx, True))
    o_q = f_w
    o_ba = f_w + qkvz_w
    o_cv = o_ba + n_gate
    new_states = []
    for l in range(depth):
        w = w_in[l]
        w_perm = jnp.concatenate(
            [w[:, :o_ba], w[:, o_cv:o_cv + 2 * cv_w], w[:, o_ba:o_cv], jnp.zeros((d, LANE - n_gate), F32)],
            axis=1).astype(BF16)
        alog = _lane_row(dn_a_log[l].reshape(-1), N_DIRS * DN_HEADS)
        dtb = _lane_row(dn_dt_bias[l].reshape(-1), N_DIRS * DN_HEADS)
        xc, xs, qkvz, cv, gates = _proj(x, norm1[l][None], mods[l], w_perm, ccs, alog, dtb,
                                        mod_row_fn(tm), tm, f_w, qkvz_w, 2 * cv_w)
        g16 = gates[:, :n_gate].reshape(-1, 2 * N_DIRS, DN_HEADS)
        gcol = g16.transpose(2, 0, 1)
        grow = g16.reshape(-1, CHUNK, 2 * N_DIRS, DN_HEADS).transpose(3, 0, 2, 1)

        parts = [[], [], []]
        for seq_len, n_batch, row_off, latent in passes:
            parts[0].append(_fourier(xc, xs, seq_len, n_batch, row_off, f_gw))
            res = _delta(qkvz, gcol, grow, sc_w[l], sc_b[l][None], dn_norm[l][None],
                         state_delta if latent else None, l, seq_len, n_batch, row_off, dk,
                         write_state=not latent)
            if latent:
                parts[1].append(res)
            else:
                parts[1].append(res[0])
                new_states.append(res[1])
            parts[2].append(_conformer(cv, cv_dw_w[l], cv_dw_b[l][None], cv_ln_g[l][None],
                                       cv_ln_b[l][None], seq_len, n_batch, row_off, latent))
        x = _outproj(parts, w_out[l].astype(BF16), x, mods[l], mod_row_fn(tm), tm, t_ctx // tm)

        fg = final_norm[None] if l == depth - 1 else None
        if l % 2 == 0:
            i = l // 2
            x = _ffn(x, norm2[l][None], mods[l], mod_row_fn(tm_ffn), ffn_wg[i][None], ffn_wu[i][None],
                     ffn_wd[i][None], None, fg, tm_ffn, _pick_tile(512, ffn_wg.shape[-1]))
        else:
            i = l // 2
            x = _moe(x, norm2[l][None], mods[l], mod_row_fn, moe_router[i], moe_wg[i], moe_wu[i], moe_wd[i], fg,
                     tm_route=tm, tm_exp=MOE_TILE, tm_comb=_pick_tile(256, t_ctx, l_lat),
                     tf=_pick_tile(512, moe_wg.shape[-1]), sub=MOE_SUB)

    y_prompt = x[:t_ctx].reshape(b_ctx, l_ctx, d)
    y_sample = x[t_ctx:].reshape(b_lat, l_lat, d)
    return y_prompt, y_sample, jnp.stack(new_states, axis=1)
```

```python
import functools
import math

import numpy as np
import jax
import jax.numpy as jnp
from jax import lax
from jax.experimental import pallas as pl
from jax.experimental.pallas import tpu as pltpu

F32 = jnp.float32
BF16 = jnp.bfloat16
EPS = 1e-6

LANE = 128
SUBLANE = 8
VMEM_LIMIT = 56 * 1024 * 1024

CHUNK = 64
GRID_W = 64
F_GROUPS = 4
DN_HEADS = 4
N_DIRS = 2
PREP_UNROLL = 8
TOP_K = 2
N_MOD = 6
MOD_ROWS = 16


def _cparams(sem):
    return pltpu.CompilerParams(dimension_semantics=sem, vmem_limit_bytes=VMEM_LIMIT)


def _pick_tile(limit, *sizes):
    t = limit
    while any(s % t for s in sizes):
        t //= 2
    return t


def _dot(a, b):
    return jnp.dot(a, b, preferred_element_type=F32)


def _mm(a, b):
    return jnp.dot(a.astype(BF16), b.astype(BF16), preferred_element_type=F32)


def _mm_nt(a, b):
    return lax.dot_general(a.astype(BF16), b.astype(BF16), (((1,), (1,)), ((), ())),
                           preferred_element_type=F32)


def _mm_tn(a, b):
    return lax.dot_general(a.astype(BF16), b.astype(BF16), (((0,), (0,)), ((), ())),
                           preferred_element_type=F32)


def _split(a):
    hi = a.astype(BF16)
    lo = (a - hi.astype(F32)).astype(BF16)
    return hi, lo


def _mm3(a, b):
    ah, al = _split(a)
    bh, bl = _split(b)
    return _dot(ah, bh) + (_dot(ah, bl) + _dot(al, bh))


def _sigmoid(x):
    return 1.0 / (1.0 + jnp.exp(-x))


def _silu(x):
    return x * _sigmoid(x)


def _rms(x):
    return x * lax.rsqrt(jnp.mean(x * x, axis=-1, keepdims=True) + EPS)


def _adaln_kernel(c_ref, w_ref, b_ref, o_ref):
    o_ref[...] = _mm(_silu(c_ref[...]), w_ref[...]) + b_ref[...]


def _adaln(cvec, w_mod, b_mod):
    depth, d, n = w_mod.shape
    tn = _pick_tile(1024, n)
    return pl.pallas_call(
        _adaln_kernel,
        out_shape=jax.ShapeDtypeStruct((depth, MOD_ROWS, n), F32),
        grid=(depth, n // tn),
        in_specs=[pl.BlockSpec((MOD_ROWS, d), lambda l, j: (0, 0)),
                  pl.BlockSpec((None, d, tn), lambda l, j: (l, 0, j)),
                  pl.BlockSpec((None, 1, tn), lambda l, j: (l, 0, j))],
        out_specs=pl.BlockSpec((None, MOD_ROWS, tn), lambda l, j: (l, 0, j)),
        compiler_params=_cparams(("parallel", "parallel")),
        name="adaln",
    )(cvec, w_mod, b_mod.reshape(depth, 1, n))


def _proj_kernel(x_ref, g_ref, mod_ref, w_ref, ccs_ref, alog_ref, dtb_ref,
                 xc_o, xs_o, qkvz_o, cv_o, gate_o, *, d, f_w, qkvz_w, cv_w):
    x = x_ref[...]
    mod = mod_ref[...]
    h = _rms(x) * g_ref[...]
    h = (h * (1.0 + mod[:, d:2 * d]) + mod[:, 0:d]).astype(BF16)

    xf = _dot(h, w_ref[:, 0:f_w])
    xcs = _dot(xf.astype(BF16), ccs_ref[...])
    xc_o[...] = xcs[:, :f_w].astype(xc_o.dtype)
    xs_o[...] = xcs[:, f_w:].astype(xs_o.dtype)
    o = f_w
    step = 4 * LANE
    for n0 in range(0, qkvz_w, step):
        qkvz_o[:, n0:n0 + step] = _dot(h, w_ref[:, o + n0:o + n0 + step])
    o += qkvz_w
    cv_o[...] = _dot(h, w_ref[:, o:o + cv_w])
    o += cv_w
    ba = _dot(h, w_ref[:, o:o + LANE])

    tm = ba.shape[0]
    lane = lax.broadcasted_iota(jnp.int32, ba.shape, 1)
    row = lax.broadcasted_iota(jnp.int32, ba.shape, 0) % CHUNK
    beta = _sigmoid(ba)
    t = ba + dtb_ref[...]
    softplus = jnp.maximum(t, 0.0) + jnp.log1p(jnp.exp(-jnp.abs(t)))
    g = -jnp.exp(alog_ref[...]) * softplus
    cf = g
    cb = g
    s = 1
    while s < CHUNK:
        cf = cf + jnp.where(row >= s, pltpu.roll(cf, s, axis=0), 0.0)
        cb = cb + jnp.where(row < CHUNK - s, pltpu.roll(cb, tm - s, axis=0), 0.0)
        s *= 2
    n_beta = N_DIRS * DN_HEADS
    gate_o[...] = jnp.where(lane < n_beta, beta, jnp.where(lane < n_beta + DN_HEADS, cf, cb))


def _proj(x, gain, mod, w_perm, ccs, alog, dtb, mod_row, tm, f_w, qkvz_w, cv_w):
    t, d = x.shape
    n = w_perm.shape[1]
    kern = functools.partial(_proj_kernel, d=d, f_w=f_w, qkvz_w=qkvz_w, cv_w=cv_w)
    row = lambda i: (i, 0)
    const = lambda i: (0, 0)
    return pl.pallas_call(
        kern,
        out_shape=(jax.ShapeDtypeStruct((t, f_w), BF16), jax.ShapeDtypeStruct((t, f_w), BF16),
                   jax.ShapeDtypeStruct((t, qkvz_w), F32), jax.ShapeDtypeStruct((t, cv_w), F32),
                   jax.ShapeDtypeStruct((t, LANE), F32)),
        grid=(t // tm,),
        in_specs=[pl.BlockSpec((tm, d), row),
                  pl.BlockSpec((1, d), const),
                  pl.BlockSpec((None, 1, N_MOD * d), lambda i: (mod_row(i), 0, 0)),
                  pl.BlockSpec((d, n), const),
                  pl.BlockSpec(ccs.shape, const),
                  pl.BlockSpec((1, LANE), const),
                  pl.BlockSpec((1, LANE), const)],
        out_specs=(pl.BlockSpec((tm, f_w), row), pl.BlockSpec((tm, f_w), row),
                   pl.BlockSpec((tm, qkvz_w), row), pl.BlockSpec((tm, cv_w), row),
                   pl.BlockSpec((tm, LANE), row)),
        compiler_params=_cparams(("parallel",)),
        name="proj",
    )(x, gain, mod, w_perm, ccs, alog, dtb)


def _fourier_kernel(xc_ref, xs_ref, cn_ref, sn_ref, o_ref, *, scale):
    y = _dot(cn_ref[...], xc_ref[...]) - _dot(sn_ref[...], xs_ref[...])
    o_ref[...] = (y * scale).astype(o_ref.dtype)


def _dft_tables(n):
    j = np.arange(n, dtype=np.int64)
    ang = (2.0 * np.pi / n) * ((j[:, None] * j[None, :]) % n).astype(np.float64)
    return np.cos(ang), np.sin(ang)


def _fourier(xc, xs, seq_len, n_batch, row_off, f_gw):
    f_w = xc.shape[1]
    cn, sn = _dft_tables(seq_len)
    cn = jnp.asarray(cn, F32).astype(BF16)
    sn = jnp.asarray(sn, F32).astype(BF16)
    tr = _pick_tile(1024, seq_len)
    nt = seq_len // tr
    blk0 = row_off // seq_len
    kern = functools.partial(_fourier_kernel, scale=1.0 / math.sqrt(seq_len * f_gw))
    return pl.pallas_call(
        kern,
        out_shape=jax.ShapeDtypeStruct((n_batch * seq_len, f_w), BF16),
        grid=(nt, n_batch),
        in_specs=[pl.BlockSpec((seq_len, f_w), lambda i, b: (blk0 + b, 0)),
                  pl.BlockSpec((seq_len, f_w), lambda i, b: (blk0 + b, 0)),
                  pl.BlockSpec((tr, seq_len), lambda i, b: (i, 0)),
                  pl.BlockSpec((tr, seq_len), lambda i, b: (i, 0))],
        out_specs=pl.BlockSpec((tr, f_w), lambda i, b: (b * nt + i, 0)),
        compiler_params=_cparams(("parallel", "parallel")),
        name="fourier",
    )(xc, xs, cn, sn)


def _unit_tri_inverse(mats, m16, m32off, m64off, eye):
    ps = [-(a * m16) for a in mats]
    ts = [eye + p for p in ps]
    for _ in range(3):
        ps = [_mm(p, p) for p in ps]
        ts = [t + _mm(t, p) for t, p in zip(ts, ps)]
    for off in (m32off, m64off):
        us = [_mm(t, a * off) for t, a in zip(ts, mats)]
        ts = [t - _mm(u, t) for t, u in zip(ts, us)]
    return ts


def _delta_kernel(*refs, seq_len, dk, zero_init, write_state):
    (q_ref, k_ref, v_ref, z_ref, gcol_ref, grow_ref,
     wq_ref, wk_ref, wv_ref, bq_ref, bk_ref, bv_ref, og_ref) = refs[:13]
    pos = 13
    if not zero_init:
        s0_ref = refs[pos]
        pos += 1
    y_ref = refs[pos]
    pos += 1
    if write_state:
        sfin_ref = refs[pos]
        pos += 1
    qs, ks, vs, of_s, ob_s, st_s, pq_s, n_s = refs[pos:]

    n_chunks = seq_len // CHUNK
    rows = lax.broadcasted_iota(jnp.int32, (seq_len, dk), 0)

    def conv_silu(x_ref, w_ref, b_ref):
        x = x_ref[...]
        width = w_ref.shape[0]
        pad = (width - 1) // 2
        acc = jnp.zeros_like(x) + b_ref[...]
        for s in range(width):
            o = s - pad
            if o == 0:
                xs = x
            else:
                xs = pltpu.roll(x, (-o) % seq_len, axis=0)
                xs = jnp.where((rows + o >= 0) & (rows + o < seq_len), xs, 0.0)
            acc = acc + xs * w_ref[s:s + 1, :]
        return _silu(acc)

    def l2norm(x):
        return x * lax.rsqrt(jnp.sum(x * x, axis=-1, keepdims=True) + EPS)

    qs[...] = l2norm(conv_silu(q_ref, wq_ref, bq_ref)) * (dk ** -0.5)
    ks[...] = l2norm(conv_silu(k_ref, wk_ref, bk_ref))
    vs[...] = conv_silu(v_ref, wv_ref, bv_ref)
    if zero_init:
        st_s[...] = jnp.zeros_like(st_s)
    else:
        st_s[...] = s0_ref[...]

    ri = lax.broadcasted_iota(jnp.int32, (CHUNK, CHUNK), 0)
    ci = lax.broadcasted_iota(jnp.int32, (CHUNK, CHUNK), 1)
    eye = (ri == ci).astype(F32)
    same16 = (ri // 16) == (ci // 16)
    same32 = (ri // 32) == (ci // 32)
    m16 = same16.astype(F32)
    m32off = same32.astype(F32) - m16
    m64off = 1.0 - same32.astype(F32)
    incl = (ri >= ci, ri <= ci)
    strict = (ri > ci, ri < ci)
    n_beta = N_DIRS

    o_s = (of_s, ob_s)
    last_row = (CHUNK - 1, 0)

    def prep_group(first_chunk, chunks):
        chains = [(c, d) for c in chunks for d in range(N_DIRS)]
        cidx = {c: first_chunk + c for c in chunks}
        r0 = {c: pl.multiple_of(cidx[c] * CHUNK, CHUNK) for c in chunks}
        q = {c: qs[pl.ds(r0[c], CHUNK), :] for c in chunks}
        k = {c: ks[pl.ds(r0[c], CHUNK), :] for c in chunks}
        v = {c: vs[pl.ds(r0[c], CHUNK), :] for c in chunks}
        gc4 = {c: gcol_ref[pl.ds(r0[c], CHUNK), :] for c in chunks}
        gr4 = {c: grow_ref[cidx[c]] for c in chunks}
        kq = {}
        for c in chunks:
            k16 = k[c].astype(BF16)
            kq[c] = _mm_nt(jnp.concatenate([k16, q[c].astype(BF16)], axis=0), k16)
        beta, gcl, decay, a = {}, {}, {}, []
        for c, d in chains:
            beta[c, d] = gc4[c][:, d:d + 1]
            gcl[c, d] = gc4[c][:, n_beta + d:n_beta + d + 1]
            grw = gr4[c][n_beta + d:n_beta + d + 1, :]
            decay[c, d] = jnp.exp(jnp.where(incl[d], gcl[c, d] - grw, -jnp.inf))
            a.append(jnp.where(strict[d], beta[c, d] * kq[c][:CHUNK] * decay[c, d], 0.0))
        t = dict(zip(chains, _unit_tri_inverse(a, m16, m32off, m64off, eye)))
        eg = {cd: jnp.exp(gcl[cd]) for cd in chains}
        sol = {(c, d): _mm(t[c, d], jnp.concatenate([v[c] * beta[c, d], k[c] * (beta[c, d] * eg[c, d])], axis=1))
               for c, d in chains}
        aw = {(c, d): _mm(kq[c][CHUNK:] * decay[c, d], sol[c, d]) for c, d in chains}
        kuw = {}
        for c, d in chains:
            g_last = gcl[c, d][last_row[d]:last_row[d] + 1, :]
            kuw[c, d] = _mm_tn(k[c] * jnp.exp(g_last - gcl[c, d]), sol[c, d])
        for c, d in chains:
            pq_s[d, cidx[c]] = jnp.concatenate(
                [kuw[c, d][:, dk:], q[c] * eg[c, d] - aw[c, d][:, dk:]], axis=0).astype(BF16)
            n_s[d, cidx[c]] = kuw[c, d][:, :dk]
            o_s[d][pl.ds(r0[c], CHUNK), :] = aw[c, d][:, :dk]

    unroll = math.gcd(PREP_UNROLL, n_chunks)

    def prep_body(i, carry):
        prep_group(i * unroll, range(unroll))
        return carry

    lax.fori_loop(0, n_chunks // unroll, prep_body, 0)

    def scan_body(n, carry):
        for d in range(N_DIRS):
            c = n if d == 0 else n_chunks - 1 - n
            r0 = pl.multiple_of(c * CHUNK, CHUNK)
            g_last = grow_ref[c][n_beta + d:n_beta + d + 1, last_row[d]:last_row[d] + 1]
            s = st_s[d]
            r = _dot(pq_s[d, c], s.astype(BF16))
            st_s[d] = s * jnp.exp(g_last) + n_s[d, c] - r[:dk]
            o_s[d][pl.ds(r0, CHUNK), :] += r[dk:]
        return carry

    lax.fori_loop(0, n_chunks, scan_body, 0)

    o = _rms(of_s[...] + ob_s[...]) * og_ref[...]
    y_ref[...] = (o * _silu(z_ref[...])).astype(y_ref.dtype)
    if write_state:
        sfin_ref[...] = st_s[...]


def _delta(qkvz, gcol, grow, sc_w, sc_b, o_g, s0, layer, seq_len, n_batch, row_off, dk, write_state):
    nh = DN_HEADS
    blk0 = row_off // seq_len
    n_chunks = seq_len // CHUNK
    zero_init = s0 is None
    kern = functools.partial(_delta_kernel, seq_len=seq_len, dk=dk, zero_init=zero_init,
                             write_state=write_state)

    def col(group):
        return pl.BlockSpec((seq_len, dk), lambda b, h: (blk0 + b, group * nh + h))

    def wspec(group, rows):
        return pl.BlockSpec((rows, dk), lambda b, h: (0, group * nh + h))

    width = sc_w.shape[0]
    in_specs = [col(0), col(1), col(2), col(3),
                pl.BlockSpec((None, seq_len, 2 * N_DIRS), lambda b, h: (h, blk0 + b, 0)),
                pl.BlockSpec((None, n_chunks, 2 * N_DIRS, CHUNK), lambda b, h: (h, blk0 + b, 0, 0)),
                wspec(0, width), wspec(1, width), wspec(2, width),
                wspec(0, 1), wspec(1, 1), wspec(2, 1),
                pl.BlockSpec((1, dk), lambda b, h: (0, 0))]
    args = [qkvz, qkvz, qkvz, qkvz, gcol, grow, sc_w, sc_w, sc_w, sc_b, sc_b, sc_b, o_g]
    if not zero_init:
        in_specs.append(pl.BlockSpec((None, None, N_DIRS, None, dk, dk),
                                     lambda b, h: (b, layer, 0, h, 0, 0)))
        args.append(s0)
    y_shape = jax.ShapeDtypeStruct((n_batch * seq_len, nh * dk), BF16)
    y_spec = pl.BlockSpec((seq_len, dk), lambda b, h: (b, h))
    if write_state:
        out_shape = (y_shape, jax.ShapeDtypeStruct((n_batch, N_DIRS, nh, dk, dk), F32))
        out_specs = (y_spec, pl.BlockSpec((None, N_DIRS, None, dk, dk), lambda b, h: (b, 0, h, 0, 0)))
    else:
        out_shape = y_shape
        out_specs = y_spec
    return pl.pallas_call(
        kern,
        out_shape=out_shape,
        grid=(n_batch, nh),
        in_specs=in_specs,
        out_specs=out_specs,
        scratch_shapes=[pltpu.VMEM((seq_len, dk), F32)] * 5 + [
            pltpu.VMEM((N_DIRS, dk, dk), F32),
            pltpu.VMEM((N_DIRS, n_chunks, dk + CHUNK, dk), BF16),
            pltpu.VMEM((N_DIRS, n_chunks, dk, dk), F32)],
        compiler_params=_cparams(("parallel", "parallel")),
        name="delta",
    )(*args)


def _conformer_kernel(x_ref, w_ref, b_ref, lg_ref, lb_ref, o_ref, pad_s, *, n_seg, seg_blocks, halo):
    c = o_ref.shape[-1]
    width = w_ref.shape[0]
    pad = (width - 1) // 2
    seg_len = seg_blocks * GRID_W
    pad_s[:, 0:halo, :] = jnp.zeros((n_seg, halo, c), F32)
    pad_s[:, halo + seg_len:2 * halo + seg_len, :] = jnp.zeros((n_seg, halo, c), F32)
    for j in range(seg_blocks):
        if seg_blocks == 1:
            x = x_ref[...]
            pad_s[:, halo:halo + GRID_W, :] = x[..., :c] * _sigmoid(x[..., c:])
        else:
            x = x_ref[j]
            pad_s[0, halo + j * GRID_W:halo + (j + 1) * GRID_W, :] = x[:, :c] * _sigmoid(x[:, c:])

    def seg_body(r, carry):
        for j in range(seg_blocks):
            acc = jnp.zeros((GRID_W, c), F32) + b_ref[...]
            for s in range(width):
                start = halo - pad + j * GRID_W + s
                acc = acc + pad_s[r, pl.ds(start, GRID_W), :] * w_ref[s:s + 1, :]
            mu = jnp.mean(acc, axis=-1, keepdims=True)
            xc = acc - mu
            var = jnp.mean(xc * xc, axis=-1, keepdims=True)
            y = xc * lax.rsqrt(var + EPS) * lg_ref[...] + lb_ref[...]
            o_ref[r * seg_blocks + j] = _silu(y).astype(o_ref.dtype)
        return carry

    lax.fori_loop(0, n_seg, seg_body, 0)


def _conformer(cv, dw_w, dw_b, ln_g, ln_b, seq_len, n_batch, row_off, latent):
    t, c2 = cv.shape
    c = c2 // 2
    cv3 = cv.reshape(t // GRID_W, GRID_W, c2)
    blocks = seq_len // GRID_W
    n_seg, seg_blocks = (blocks, 1) if latent else (1, blocks)
    halo = 2 * SUBLANE
    assert (dw_w.shape[0] - 1) // 2 <= halo
    blk0 = row_off // seq_len
    kern = functools.partial(_conformer_kernel, n_seg=n_seg, seg_blocks=seg_blocks, halo=halo)
    const = lambda b: (0, 0)
    out = pl.pallas_call(
        kern,
        out_shape=jax.ShapeDtypeStruct((n_batch * blocks, GRID_W, c), BF16),
        grid=(n_batch,),
        in_specs=[pl.BlockSpec((blocks, GRID_W, c2), lambda b: (blk0 + b, 0, 0)),
                  pl.BlockSpec(dw_w.shape, const),
                  pl.BlockSpec((1, c), const), pl.BlockSpec((1, c), const), pl.BlockSpec((1, c), const)],
        out_specs=pl.BlockSpec((blocks, GRID_W, c), lambda b: (b, 0, 0)),
        scratch_shapes=[pltpu.VMEM((n_seg, seg_blocks * GRID_W + 2 * halo, c), F32)],
        compiler_params=_cparams(("parallel",)),
        name="conformer",
    )(cv3, dw_w, dw_b, ln_g, ln_b)
    return out.reshape(n_batch * seq_len, c)


def _outproj_kernel(yfc, yfl, ydc, ydl, ycc, ycl, w_ref, x_ref, mod_ref, o_ref, *, d, n_ctx_tiles):
    is_ctx = pl.program_id(0) < n_ctx_tiles
    y = None
    k0 = 0
    for a_c, a_l in ((yfc, yfl), (ydc, ydl), (ycc, ycl)):
        kw = a_c.shape[1]
        a = jnp.where(is_ctx, a_c[...], a_l[...])
        part = _dot(a, w_ref[k0:k0 + kw, :])
        y = part if y is None else y + part
        k0 += kw
    o_ref[...] = x_ref[...] + mod_ref[:, 2 * d:3 * d] * y


def _outproj(parts, w_out, x, mod, mod_row, tm, n_ctx_tiles):
    t, d = x.shape
    kern = functools.partial(_outproj_kernel, d=d, n_ctx_tiles=n_ctx_tiles)
    in_specs = []
    args = []
    for a_c, a_l in parts:
        kw = a_c.shape[1]
        in_specs.append(pl.BlockSpec((tm, kw), lambda i: (jnp.minimum(i, n_ctx_tiles - 1), 0)))
        in_specs.append(pl.BlockSpec((tm, kw), lambda i: (jnp.maximum(i - n_ctx_tiles, 0), 0)))
        args += [a_c, a_l]
    in_specs += [pl.BlockSpec(w_out.shape, lambda i: (0, 0)),
                 pl.BlockSpec((tm, d), lambda i: (i, 0)),
                 pl.BlockSpec((None, 1, N_MOD * d), lambda i: (mod_row(i), 0, 0))]
    args += [w_out, x, mod]
    return pl.pallas_call(
        kern,
        out_shape=jax.ShapeDtypeStruct((t, d), F32),
        grid=(t // tm,),
        in_specs=in_specs,
        out_specs=pl.BlockSpec((tm, d), lambda i: (i, 0)),
        compiler_params=_cparams(("parallel",)),
        name="outproj",
    )(*args)


def _ffn_kernel(*refs, d, n_exp, moe, final_norm):
    x_ref, g_ref, mod_ref = refs[:3]
    pos = 3
    if moe:
        router_ref = refs[pos]
        pos += 1
    wg_ref, wu_ref, wd_ref = refs[pos:pos + 3]
    pos += 3
    if final_norm:
        fg_ref = refs[pos]
        pos += 1
    o_ref = refs[pos]
    pos += 1
    h_s, acc_s = refs[pos:pos + 2]
    pos += 2
    if moe:
        gate_s = refs[pos]

    e = pl.program_id(1)
    f = pl.program_id(2)

    @pl.when((e == 0) & (f == 0))
    def _():
        mod = mod_ref[...]
        h = _rms(x_ref[...]) * g_ref[...]
        h = h * (1.0 + mod[:, 4 * d:5 * d]) + mod[:, 3 * d:4 * d]
        h_s[...] = h.astype(BF16)
        acc_s[...] = jnp.zeros_like(acc_s)
        if moe:
            logits = _mm3(h, router_ref[...])
            lane = lax.broadcasted_iota(jnp.int32, logits.shape, 1)
            lg = jnp.where(lane < n_exp, logits, -jnp.inf)
            m1 = jnp.max(lg, axis=-1, keepdims=True)
            i1 = jnp.min(jnp.where(lg == m1, lane, LANE), axis=-1, keepdims=True)
            lg2 = jnp.where(lane == i1, -jnp.inf, lg)
            m2 = jnp.max(lg2, axis=-1, keepdims=True)
            i2 = jnp.min(jnp.where(lg2 == m2, lane, LANE), axis=-1, keepdims=True)
            w1 = 1.0 / (1.0 + jnp.exp(m2 - m1))
            gate_s[...] = jnp.where(lane == i1, w1, 0.0) + jnp.where(lane == i2, 1.0 - w1, 0.0)

    h = h_s[...]
    a = _silu(_dot(h, wg_ref[...].astype(BF16))) * _dot(h, wu_ref[...].astype(BF16))
    if moe:
        gate = gate_s[...]
        lane = lax.broadcasted_iota(jnp.int32, gate.shape, 1)
        a = a * jnp.sum(jnp.where(lane == e, gate, 0.0), axis=-1, keepdims=True)
    acc_s[...] += _dot(a.astype(BF16), wd_ref[...].astype(BF16))

    @pl.when((e == n_exp - 1) & (f == pl.num_programs(2) - 1))
    def _():
        o = x_ref[...] + mod_ref[:, 5 * d:6 * d] * acc_s[...]
        if final_norm:
            o = _rms(o) * fg_ref[...]
        o_ref[...] = o


def _ffn(x, gain, mod, mod_row, wg, wu, wd, router, final_gain, tm, tf):
    t, d = x.shape
    n_exp, _, ff = wg.shape
    moe = router is not None
    final_norm = final_gain is not None
    kern = functools.partial(_ffn_kernel, d=d, n_exp=n_exp, moe=moe, final_norm=final_norm)
    row = lambda i, e, f: (i, 0)
    const = lambda i, e, f: (0, 0)
    in_specs = [pl.BlockSpec((tm, d), row),
                pl.BlockSpec((1, d), const),
                pl.BlockSpec((None, 1, N_MOD * d), lambda i, e, f: (mod_row(i), 0, 0))]
    args = [x, gain, mod]
    if moe:
        in_specs.append(pl.BlockSpec(router.shape, const))
        args.append(router)
    in_specs += [pl.BlockSpec((None, d, tf), lambda i, e, f: (e, 0, f)),
                 pl.BlockSpec((None, d, tf), lambda i, e, f: (e, 0, f)),
                 pl.BlockSpec((None, tf, d), lambda i, e, f: (e, f, 0))]
    args += [wg, wu, wd]
    if final_norm:
        in_specs.append(pl.BlockSpec((1, d), const))
        args.append(final_gain)
    scratch = [pltpu.VMEM((tm, d), BF16), pltpu.VMEM((tm, d), F32)]
    if moe:
        scratch.append(pltpu.VMEM((tm, LANE), F32))
    return pl.pallas_call(
        kern,
        out_shape=jax.ShapeDtypeStruct((t, d), F32),
        grid=(t // tm, n_exp, ff // tf),
        in_specs=in_specs,
        out_specs=pl.BlockSpec((tm, d), row),
        scratch_shapes=scratch,
        compiler_params=_cparams(("parallel", "arbitrary", "arbitrary")),
        name="moe_ffn" if moe else "ffn",
    )(*args)


INFO_E, INFO_W, INFO_R = 0, 2, 4
MOE_TILE = 1024
MOE_SUB = 256


def _route_kernel(x_ref, g_ref, mod_ref, router_ref, h_o, info_o, cnt_o, cnt_s, *, d, n_exp):
    i = pl.program_id(0)

    @pl.when(i == 0)
    def _():
        cnt_s[...] = jnp.zeros_like(cnt_s)

    mod = mod_ref[...]
    h = _rms(x_ref[...]) * g_ref[...]
    h = h * (1.0 + mod[:, 4 * d:5 * d]) + mod[:, 3 * d:4 * d]
    h_o[...] = h

    logits = _mm3(h, router_ref[...])
    tm = logits.shape[0]
    lane = lax.broadcasted_iota(jnp.int32, logits.shape, 1)
    lg = jnp.where(lane < n_exp, logits, -jnp.inf)
    m1 = jnp.max(lg, axis=-1, keepdims=True)
    i1 = jnp.min(jnp.where(lg == m1, lane, LANE), axis=-1, keepdims=True)
    lg2 = jnp.where(lane == i1, -jnp.inf, lg)
    m2 = jnp.max(lg2, axis=-1, keepdims=True)
    i2 = jnp.min(jnp.where(lg2 == m2, lane, LANE), axis=-1, keepdims=True)
    w1 = 1.0 / (1.0 + jnp.exp(m2 - m1))

    sel1 = lane == i1
    sel2 = lane == i2
    member = jnp.where(sel1 | sel2, 1.0, 0.0)
    ri = lax.broadcasted_iota(jnp.int32, (tm, tm), 0)
    ci = lax.broadcasted_iota(jnp.int32, (tm, tm), 1)
    before = jnp.where(ri > ci, 1.0, 0.0)
    rank = cnt_s[...] + _mm(before, member)
    r1 = jnp.sum(jnp.where(sel1, rank, 0.0), axis=-1, keepdims=True)
    r2 = jnp.sum(jnp.where(sel2, rank, 0.0), axis=-1, keepdims=True)
    cnt_s[...] += jnp.sum(member, axis=0, keepdims=True)
    cnt_o[...] = jnp.broadcast_to(cnt_s[...], cnt_o.shape)

    info = jnp.zeros(logits.shape, F32)
    for ln, val in ((INFO_E, i1.astype(F32)), (INFO_E + 1, i2.astype(F32)), (INFO_W, w1),
                    (INFO_W + 1, 1.0 - w1), (INFO_R, r1), (INFO_R + 1, r2)):
        info = jnp.where(lane == ln, val, info)
    info_o[...] = info


def _route(x, gain, mod, mod_row, router, tm):
    t, d = x.shape
    n_exp = router.shape[1]
    router_p = jnp.zeros((d, LANE), F32).at[:, :n_exp].set(router)
    kern = functools.partial(_route_kernel, d=d, n_exp=n_exp)
    return pl.pallas_call(
        kern,
        out_shape=(jax.ShapeDtypeStruct((t, d), F32), jax.ShapeDtypeStruct((t, LANE), F32),
                   jax.ShapeDtypeStruct((SUBLANE, LANE), F32)),
        grid=(t // tm,),
        in_specs=[pl.BlockSpec((tm, d), lambda i: (i, 0)),
                  pl.BlockSpec((1, d), lambda i: (0, 0)),
                  pl.BlockSpec((None, 1, N_MOD * d), lambda i: (mod_row(i), 0, 0)),
                  pl.BlockSpec((d, LANE), lambda i: (0, 0))],
        out_specs=(pl.BlockSpec((tm, d), lambda i: (i, 0)), pl.BlockSpec((tm, LANE), lambda i: (i, 0)),
                   pl.BlockSpec((SUBLANE, LANE), lambda i: (0, 0))),
        scratch_shapes=[pltpu.VMEM((1, LANE), F32)],
        compiler_params=_cparams(("arbitrary",)),
        name="moe_route",
    )(x, gain, mod, router_p)


def _row_copy(src_hbm, row, dst, slot, sem):
    return pltpu.make_async_copy(src_hbm.at[pl.ds(row, 1), :], dst.at[pl.ds(slot, 1), :], sem)


def _experts_kernel(expert_sm, nvalid_sm, tok_ref, tok_next_ref, h_hbm, wg_ref, wu_ref, wd_ref, y_ref,
                    hbuf, h16, acc, wg16, wu16, wd16, sems, *, sub, n_f):
    del expert_sm
    i = pl.program_id(0)
    j = pl.program_id(1)
    n_i = pl.num_programs(0)
    tm = h16.shape[0]
    slot = i % 2
    nxt = 1 - slot
    nv = nvalid_sm[i]
    n_sub = (nv + sub - 1) // sub
    has_next = jnp.where(i + 1 < n_i, nvalid_sm[jnp.minimum(i + 1, n_i - 1)], 0) > 0
    per_step = -(-tm // n_f)
    base = j * per_step

    def start_row(tok, r, buf):
        _row_copy(h_hbm, tok[0, r], hbuf.at[buf], r, sems.at[buf]).start()

    def fetch_loop(tok, lo, hi, buf):
        def issue(r, carry):
            start_row(tok, r, buf)
            return carry

        lax.fori_loop(lo, hi, issue, 0)

    @pl.when(j == 0)
    def _():
        @pl.when((i == 0) & (nv > 0))
        def _():
            fetch_loop(tok_ref, 0, tm, 0)

        acc[...] = jnp.zeros_like(acc)

        @pl.when(nv > 0)
        def _():
            pltpu.make_async_copy(h_hbm.at[pl.ds(0, tm), :], hbuf.at[slot], sems.at[slot]).wait()

            def cast(s, carry):
                rows = pl.ds(pl.multiple_of(s * sub, sub), sub)
                h16[rows, :] = hbuf[slot, rows, :].astype(BF16)
                return carry

            lax.fori_loop(0, n_sub, cast, 0)

    fast = (n_sub == tm // sub) & has_next

    @pl.when(fast)
    def _():
        n_safe = tm - (n_f - 1) * per_step
        for r in range(min(per_step, n_safe)):
            start_row(tok_next_ref, base + r, nxt)
        h = h16[...]
        a = _silu(_dot(h, wg_ref[...].astype(BF16))) * _dot(h, wu_ref[...].astype(BF16))
        acc[...] += _dot(a.astype(BF16), wd_ref[...].astype(BF16))
        for r in range(n_safe, per_step):
            pl.when(base + r < tm)(functools.partial(start_row, tok_next_ref, base + r, nxt))

    @pl.when(jnp.logical_not(fast))
    def _():
        @pl.when(has_next)
        def _():
            fetch_loop(tok_next_ref, base, jnp.minimum(base + per_step, tm), nxt)

        @pl.when(n_sub > 0)
        def _():
            wg16[...] = wg_ref[...].astype(BF16)
            wu16[...] = wu_ref[...].astype(BF16)
            wd16[...] = wd_ref[...].astype(BF16)

            def block(s, carry):
                rows = pl.ds(pl.multiple_of(s * sub, sub), sub)
                h = h16[rows, :]
                a = _silu(_dot(h, wg16[...])) * _dot(h, wu16[...])
                acc[rows, :] += _dot(a.astype(BF16), wd16[...])
                return carry

            lax.fori_loop(0, n_sub, block, 0)

    @pl.when(j == n_f - 1)
    def _():
        y_ref[...] = acc[...]


def _experts(h, src_tok, tile_expert, tile_nvalid, wg, wu, wd, tm, tf, sub):
    t, d = h.shape
    n_exp, _, ff = wg.shape
    n_tiles = tile_expert.shape[0]
    n_f = ff // tf

    def w_idx(i, j, expert_sm, nvalid_sm):
        return expert_sm[i], jnp.where(nvalid_sm[i] > 0, j, n_f - 1)

    def wgu_map(i, j, expert_sm, nvalid_sm):
        e, jj = w_idx(i, j, expert_sm, nvalid_sm)
        return e, 0, jj

    def wd_map(i, j, expert_sm, nvalid_sm):
        e, jj = w_idx(i, j, expert_sm, nvalid_sm)
        return e, jj, 0

    kern = functools.partial(_experts_kernel, sub=sub, n_f=n_f)
    tok = src_tok.reshape(n_tiles, 1, tm)
    grid_spec = pltpu.PrefetchScalarGridSpec(
        num_scalar_prefetch=2,
        grid=(n_tiles, n_f),
        in_specs=[pl.BlockSpec((None, 1, tm), lambda i, j, *_: (i, 0, 0), memory_space=pltpu.SMEM),
                  pl.BlockSpec((None, 1, tm), lambda i, j, *_: (jnp.minimum(i + 1, n_tiles - 1), 0, 0),
                               memory_space=pltpu.SMEM),
                  pl.BlockSpec(memory_space=pl.ANY),
                  pl.BlockSpec((None, d, tf), wgu_map),
                  pl.BlockSpec((None, d, tf), wgu_map),
                  pl.BlockSpec((None, tf, d), wd_map)],
        out_specs=pl.BlockSpec((tm, d), lambda i, j, *_: (i, 0)),
        scratch_shapes=[pltpu.VMEM((2, tm, d), F32), pltpu.VMEM((tm, d), BF16), pltpu.VMEM((tm, d), F32),
                        pltpu.VMEM((d, tf), BF16), pltpu.VMEM((d, tf), BF16), pltpu.VMEM((tf, d), BF16),
                        pltpu.SemaphoreType.DMA((2,))])
    return pl.pallas_call(
        kern,
        out_shape=jax.ShapeDtypeStruct((n_tiles * tm, d), F32),
        grid_spec=grid_spec,
        compiler_params=_cparams(("arbitrary", "arbitrary")),
        name="moe_experts",
    )(tile_expert, tile_nvalid, tok, tok, h, wg, wu, wd)


def _combine_kernel(*refs, d, final_norm):
    pos_ref, pos_next_ref, info_ref, x_ref, mod_ref = refs[:5]
    p = 5
    if final_norm:
        fg_ref = refs[p]
        p += 1
    y_hbm, o_ref, ybuf, sems = refs[p:]
    tm = x_ref.shape[0]
    i = pl.program_id(0)
    slot = i % 2

    def fetch(pos, buf):
        def issue(r, carry):
            for k in range(TOP_K):
                _row_copy(y_hbm, pos[0, k * tm + r], ybuf.at[buf, k], r, sems.at[buf]).start()
            return carry

        lax.fori_loop(0, tm, issue, 0, unroll=8)

    @pl.when(i == 0)
    def _():
        fetch(pos_ref, 0)

    @pl.when(i + 1 < pl.num_programs(0))
    def _():
        fetch(pos_next_ref, 1 - slot)

    for k in range(TOP_K):
        pltpu.make_async_copy(y_hbm.at[pl.ds(0, tm), :], ybuf.at[slot, k], sems.at[slot]).wait()

    info = info_ref[...]
    f = None
    for k in range(TOP_K):
        term = info[:, INFO_W + k:INFO_W + k + 1] * ybuf[slot, k]
        f = term if f is None else f + term
    o = x_ref[...] + mod_ref[:, 5 * d:6 * d] * f
    if final_norm:
        o = _rms(o) * fg_ref[...]
    o_ref[...] = o


def _combine(x, info, pos, y, mod, mod_row, final_gain, tm):
    t, d = x.shape
    final_norm = final_gain is not None
    kern = functools.partial(_combine_kernel, d=d, final_norm=final_norm)
    n_tiles = t // tm
    in_specs = [pl.BlockSpec((None, 1, TOP_K * tm), lambda i: (i, 0, 0), memory_space=pltpu.SMEM),
                pl.BlockSpec((None, 1, TOP_K * tm), lambda i: (jnp.minimum(i + 1, n_tiles - 1), 0, 0),
                             memory_space=pltpu.SMEM),
                pl.BlockSpec((tm, LANE), lambda i: (i, 0)),
                pl.BlockSpec((tm, d), lambda i: (i, 0)),
                pl.BlockSpec((None, 1, N_MOD * d), lambda i: (mod_row(i), 0, 0))]
    args = [pos, pos, info, x, mod]
    if final_norm:
        in_specs.append(pl.BlockSpec((1, d), lambda i: (0, 0)))
        args.append(final_gain)
    in_specs.append(pl.BlockSpec(memory_space=pl.ANY))
    args.append(y)
    return pl.pallas_call(
        kern,
        out_shape=jax.ShapeDtypeStruct((t, d), F32),
        grid=(t // tm,),
        in_specs=in_specs,
        out_specs=pl.BlockSpec((tm, d), lambda i: (i, 0)),
        scratch_shapes=[pltpu.VMEM((2, TOP_K, tm, d), F32), pltpu.SemaphoreType.DMA((2,))],
        compiler_params=_cparams(("arbitrary",)),
        name="moe_combine",
    )(*args)


def _moe(x, gain, mod, mod_row_fn, router, wg, wu, wd, final_gain, tm_route, tm_exp, tm_comb, tf, sub):
    t, d = x.shape
    n_exp = wg.shape[0]
    h, info, cnt = _route(x, gain, mod, mod_row_fn(tm_route), router, tm_route)

    counts = cnt[0, :n_exp].astype(jnp.int32)
    tiles_per = (counts + tm_exp - 1) // tm_exp
    tile_end = jnp.cumsum(tiles_per)
    tile_start = tile_end - tiles_per
    n_tiles = (TOP_K * t) // tm_exp + n_exp
    tile_id = jnp.arange(n_tiles, dtype=jnp.int32)
    tile_expert = jnp.minimum(jnp.sum(tile_id[:, None] >= tile_end[None, :], axis=1), n_exp - 1).astype(jnp.int32)
    tile_nvalid = jnp.where(tile_id < tile_end[n_exp - 1],
                            jnp.clip(counts[tile_expert] - (tile_id - tile_start[tile_expert]) * tm_exp, 0, tm_exp),
                            0).astype(jnp.int32)
    slot_start = tile_start * tm_exp
    experts = info[:, INFO_E:INFO_E + TOP_K].astype(jnp.int32)
    slots = slot_start[experts] + info[:, INFO_R:INFO_R + TOP_K].astype(jnp.int32)
    token = jnp.broadcast_to(jnp.arange(t, dtype=jnp.int32)[:, None], slots.shape)
    src_tok = jnp.zeros((n_tiles * tm_exp,), jnp.int32).at[slots.reshape(-1)].set(token.reshape(-1))

    y = _experts(h, src_tok, tile_expert, tile_nvalid, wg, wu, wd, tm_exp, tf, sub)
    pos = slots.reshape(t // tm_comb, tm_comb, TOP_K).transpose(0, 2, 1).reshape(t // tm_comb, 1, TOP_K * tm_comb)
    return _combine(x, info, pos, y, mod, mod_row_fn(tm_comb), final_gain, tm_comb)


def _lane_row(vals, offset):
    n = vals.shape[0]
    return jnp.zeros((1, LANE), F32).at[0, offset:offset + n].set(vals.astype(F32))


def kernel(x_prompt, x_sample, state_delta, c, c_ctx, norm1, norm2, w_mod, b_mod, w_in, sc_w, sc_b, dn_a_log, dn_dt_bias, dn_norm, cv_dw_w, cv_dw_b, cv_ln_g, cv_ln_b, w_out, ffn_wg, ffn_wu, ffn_wd, moe_router, moe_wg, moe_wu, moe_wd, final_norm):
    b_ctx, l_ctx, d = x_prompt.shape
    b_lat, l_lat, _ = x_sample.shape
    depth = w_in.shape[0]
    t_ctx = b_ctx * l_ctx
    t_lat = b_lat * l_lat
    f_w = d // 4
    f_gw = f_w // F_GROUPS
    dn_w = d // 2
    dk = dn_w // DN_HEADS
    cv_w = d // 4
    qkvz_w = 4 * dn_w
    n_gate = 2 * N_DIRS * DN_HEADS
    n_exp = moe_wg.shape[1]
    assert dk == LANE and t_ctx % l_lat == 0 and b_lat + 1 <= MOD_ROWS
    assert l_ctx % CHUNK == 0 and l_lat % CHUNK == 0 and l_lat % GRID_W == 0 and CHUNK == GRID_W

    tm = _pick_tile(512, t_ctx, l_lat)
    tm_ffn = _pick_tile(1024, t_ctx, l_lat)

    def mod_row_fn(tile):
        n_ctx_tiles = t_ctx // tile
        per_seq = l_lat // tile
        return lambda i: jnp.where(i < n_ctx_tiles, 0, 1 + (i - n_ctx_tiles) // per_seq)

    x = jnp.concatenate([x_prompt.reshape(t_ctx, d), x_sample.reshape(t_lat, d)], axis=0)
    cvec = jnp.zeros((MOD_ROWS, d), F32).at[0].set(c_ctx).at[1:1 + b_lat].set(c)
    mods = _adaln(cvec, w_mod, b_mod).reshape(depth, MOD_ROWS, 1, N_MOD * d)

    cg, sg = _dft_tables(f_gw)
    eye_g = np.eye(F_GROUPS)
    ccs = jnp.asarray(np.concatenate([np.kron(eye_g, cg), np.kron(eye_g, sg)], axis=1), F32).astype(BF16)

    passes = ((l_ctx, b_ctx, 0, False), (l_lat, b_lat, t_ctx, True))
    o_q = f_w
    o_ba = f_w + qkvz_w
    o_cv = o_ba + n_gate
    new_states = []
    for l in range(depth):
        w = w_in[l]
        w_perm = jnp.concatenate(
            [w[:, :o_ba], w[:, o_cv:o_cv + 2 * cv_w], w[:, o_ba:o_cv], jnp.zeros((d, LANE - n_gate), F32)],
            axis=1).astype(BF16)
        alog = _lane_row(dn_a_log[l].reshape(-1), N_DIRS * DN_HEADS)
        dtb = _lane_row(dn_dt_bias[l].reshape(-1), N_DIRS * DN_HEADS)
        xc, xs, qkvz, cv, gates = _proj(x, norm1[l][None], mods[l], w_perm, ccs, alog, dtb,
                                        mod_row_fn(tm), tm, f_w, qkvz_w, 2 * cv_w)
        g16 = gates[:, :n_gate].reshape(-1, 2 * N_DIRS, DN_HEADS)
        gcol = g16.transpose(2, 0, 1)
        grow = g16.reshape(-1, CHUNK, 2 * N_DIRS, DN_HEADS).transpose(3, 0, 2, 1)

        parts = [[], [], []]
        for seq_len, n_batch, row_off, latent in passes:
            parts[0].append(_fourier(xc, xs, seq_len, n_batch, row_off, f_gw))
            res = _delta(qkvz, gcol, grow, sc_w[l], sc_b[l][None], dn_norm[l][None],
                         state_delta if latent else None, l, seq_len, n_batch, row_off, dk,
                         write_state=not latent)
            if latent:
                parts[1].append(res)
            else:
                parts[1].append(res[0])
                new_states.append(res[1])
            parts[2].append(_conformer(cv, cv_dw_w[l], cv_dw_b[l][None], cv_ln_g[l][None],
                                       cv_ln_b[l][None], seq_len, n_batch, row_off, latent))
        x = _outproj(parts, w_out[l].astype(BF16), x, mods[l], mod_row_fn(tm), tm, t_ctx // tm)

        fg = final_norm[None] if l == depth - 1 else None
        if l % 2 == 0:
            i = l // 2
            x = _ffn(x, norm2[l][None], mods[l], mod_row_fn(tm_ffn), ffn_wg[i][None], ffn_wu[i][None],
                     ffn_wd[i][None], None, fg, tm_ffn, _pick_tile(512, ffn_wg.shape[-1]))
        else:
            i = l // 2
            x = _moe(x, norm2[l][None], mods[l], mod_row_fn, moe_router[i], moe_wg[i], moe_wu[i], moe_wd[i], fg,
                     tm_route=tm, tm_exp=MOE_TILE, tm_comb=_pick_tile(256, t_ctx, l_lat),
                     tf=_pick_tile(512, moe_wg.shape[-1]), sub=MOE_SUB)

    y_prompt = x[:t_ctx].reshape(b_ctx, l_ctx, d)
    y_sample = x[t_ctx:].reshape(b_lat, l_lat, d)
    return y_prompt, y_sample, jnp.stack(new_states, axis=1)
```

```python
import functools
import math

import numpy as np
import jax
import jax.numpy as jnp
from jax import lax
from jax.experimental import pallas as pl
from jax.experimental.pallas import tpu as pltpu

F32 = jnp.float32
BF16 = jnp.bfloat16
EPS = 1e-6

LANE = 128
SUBLANE = 8
VMEM_LIMIT = 56 * 1024 * 1024

CHUNK = 64
GRID_W = 64
F_GROUPS = 4
DN_HEADS = 4
N_DIRS = 2
PREP_UNROLL = 8
TOP_K = 2
N_MOD = 6
MOD_ROWS = 16


def _cparams(sem):
    return pltpu.CompilerParams(dimension_semantics=sem, vmem_limit_bytes=VMEM_LIMIT)


def _pick_tile(limit, *sizes):
    t = limit
    while any(s % t for s in sizes):
        t //= 2
    return t


def _dot(a, b):
    return jnp.dot(a, b, preferred_element_type=F32)


def _mm(a, b):
    return jnp.dot(a.astype(BF16), b.astype(BF16), preferred_element_type=F32)


def _mm_nt(a, b):
    return lax.dot_general(a.astype(BF16), b.astype(BF16), (((1,), (1,)), ((), ())),
                           preferred_element_type=F32)


def _mm_tn(a, b):
    return lax.dot_general(a.astype(BF16), b.astype(BF16), (((0,), (0,)), ((), ())),
                           preferred_element_type=F32)


def _split(a):
    hi = a.astype(BF16)
    lo = (a - hi.astype(F32)).astype(BF16)
    return hi, lo


def _mm3(a, b):
    ah, al = _split(a)
    bh, bl = _split(b)
    return _dot(ah, bh) + (_dot(ah, bl) + _dot(al, bh))


def _sigmoid(x):
    return 1.0 / (1.0 + jnp.exp(-x))


def _silu(x):
    return x * _sigmoid(x)


def _rms(x):
    return x * lax.rsqrt(jnp.mean(x * x, axis=-1, keepdims=True) + EPS)


def _stream_specs(x, tm, n_ctx_tiles):
    if isinstance(x, tuple):
        d = x[0].shape[1]
        return ([pl.BlockSpec((tm, d), lambda i, *_: (jnp.minimum(i, n_ctx_tiles - 1), 0)),
                 pl.BlockSpec((tm, d), lambda i, *_: (jnp.maximum(i - n_ctx_tiles, 0), 0))], list(x))
    return [pl.BlockSpec((tm, x.shape[1]), lambda i, *_: (i, 0))], [x]


def _read_stream(x_refs, n_ctx_tiles):
    if len(x_refs) == 2:
        return jnp.where(pl.program_id(0) < n_ctx_tiles, x_refs[0][...], x_refs[1][...])
    return x_refs[0][...]


def _adaln_kernel(c_ref, w_ref, b_ref, o_ref):
    o_ref[...] = _mm(_silu(c_ref[...]), w_ref[...]) + b_ref[...]


def _adaln(cvec, w_mod, b_mod):
    depth, d, n = w_mod.shape
    tn = _pick_tile(1024, n)
    return pl.pallas_call(
        _adaln_kernel,
        out_shape=jax.ShapeDtypeStruct((depth, MOD_ROWS, n), F32),
        grid=(depth, n // tn),
        in_specs=[pl.BlockSpec((MOD_ROWS, d), lambda l, j: (0, 0)),
                  pl.BlockSpec((None, d, tn), lambda l, j: (l, 0, j)),
                  pl.BlockSpec((None, 1, tn), lambda l, j: (l, 0, j))],
        out_specs=pl.BlockSpec((None, MOD_ROWS, tn), lambda l, j: (l, 0, j)),
        compiler_params=_cparams(("parallel", "parallel")),
        name="adaln",
    )(cvec, w_mod, b_mod.reshape(depth, 1, n))


def _proj_kernel(*refs, n_x, n_ctx_tiles, d, f_w, qkvz_w, cv_w):
    g_ref, mod_ref, w_ref, ccs_ref, alog_ref, dtb_ref, xc_o, xs_o, qkvz_o, cv_o, gate_o = refs[n_x:]
    x = _read_stream(refs[:n_x], n_ctx_tiles)
    mod = mod_ref[...]
    h = _rms(x) * g_ref[...]
    h = (h * (1.0 + mod[:, d:2 * d]) + mod[:, 0:d]).astype(BF16)

    xf = _dot(h, w_ref[:, 0:f_w])
    xcs = _dot(xf.astype(BF16), ccs_ref[...])
    xc_o[...] = xcs[:, :f_w].astype(xc_o.dtype)
    xs_o[...] = xcs[:, f_w:].astype(xs_o.dtype)
    o = f_w
    step = 4 * LANE
    for n0 in range(0, qkvz_w, step):
        qkvz_o[:, n0:n0 + step] = _dot(h, w_ref[:, o + n0:o + n0 + step])
    o += qkvz_w
    cv_o[...] = _dot(h, w_ref[:, o:o + cv_w])
    o += cv_w
    ba = _dot(h, w_ref[:, o:o + LANE])

    tm = ba.shape[0]
    lane = lax.broadcasted_iota(jnp.int32, ba.shape, 1)
    row = lax.broadcasted_iota(jnp.int32, ba.shape, 0) % CHUNK
    beta = _sigmoid(ba)
    t = ba + dtb_ref[...]
    softplus = jnp.maximum(t, 0.0) + jnp.log1p(jnp.exp(-jnp.abs(t)))
    g = -jnp.exp(alog_ref[...]) * softplus
    cf = g
    cb = g
    s = 1
    while s < CHUNK:
        cf = cf + jnp.where(row >= s, pltpu.roll(cf, s, axis=0), 0.0)
        cb = cb + jnp.where(row < CHUNK - s, pltpu.roll(cb, tm - s, axis=0), 0.0)
        s *= 2
    n_beta = N_DIRS * DN_HEADS
    gate_o[...] = jnp.where(lane < n_beta, beta, jnp.where(lane < n_beta + DN_HEADS, cf, cb))


def _proj(x, gain, mod, w_perm, ccs, alog, dtb, mod_row, tm, n_ctx_tiles, f_w, qkvz_w, cv_w):
    x_specs, x_args = _stream_specs(x, tm, n_ctx_tiles)
    t = sum(a.shape[0] for a in x_args)
    d, n = w_perm.shape
    kern = functools.partial(_proj_kernel, n_x=len(x_args), n_ctx_tiles=n_ctx_tiles, d=d, f_w=f_w,
                             qkvz_w=qkvz_w, cv_w=cv_w)
    row = lambda i: (i, 0)
    const = lambda i: (0, 0)
    return pl.pallas_call(
        kern,
        out_shape=(jax.ShapeDtypeStruct((t, f_w), BF16), jax.ShapeDtypeStruct((t, f_w), BF16),
                   jax.ShapeDtypeStruct((t, qkvz_w), F32), jax.ShapeDtypeStruct((t, cv_w), F32),
                   jax.ShapeDtypeStruct((t, LANE), F32)),
        grid=(t // tm,),
        in_specs=x_specs + [
            pl.BlockSpec((1, d), const),
            pl.BlockSpec((None, 1, N_MOD * d), lambda i: (mod_row(i), 0, 0)),
            pl.BlockSpec((d, n), const),
            pl.BlockSpec(ccs.shape, const),
            pl.BlockSpec((1, LANE), const),
            pl.BlockSpec((1, LANE), const)],
        out_specs=(pl.BlockSpec((tm, f_w), row), pl.BlockSpec((tm, f_w), row),
                   pl.BlockSpec((tm, qkvz_w), row), pl.BlockSpec((tm, cv_w), row),
                   pl.BlockSpec((tm, LANE), row)),
        compiler_params=_cparams(("parallel",)),
        name="proj",
    )(*x_args, gain, mod, w_perm, ccs, alog, dtb)


def _fourier_kernel(xc_ref, xs_ref, cn_ref, sn_ref, o_ref, *, scale):
    y = _dot(cn_ref[...], xc_ref[...]) - _dot(sn_ref[...], xs_ref[...])
    o_ref[...] = (y * scale).astype(o_ref.dtype)


def _dft_tables(n):
    j = np.arange(n, dtype=np.int64)
    ang = (2.0 * np.pi / n) * ((j[:, None] * j[None, :]) % n).astype(np.float64)
    return np.cos(ang), np.sin(ang)


def _fourier(xc, xs, seq_len, n_batch, row_off, f_gw):
    f_w = xc.shape[1]
    cn, sn = _dft_tables(seq_len)
    cn = jnp.asarray(cn, F32).astype(BF16)
    sn = jnp.asarray(sn, F32).astype(BF16)
    tr = _pick_tile(1024, seq_len)
    nt = seq_len // tr
    blk0 = row_off // seq_len
    kern = functools.partial(_fourier_kernel, scale=1.0 / math.sqrt(seq_len * f_gw))
    return pl.pallas_call(
        kern,
        out_shape=jax.ShapeDtypeStruct((n_batch * seq_len, f_w), BF16),
        grid=(nt, n_batch),
        in_specs=[pl.BlockSpec((seq_len, f_w), lambda i, b: (blk0 + b, 0)),
                  pl.BlockSpec((seq_len, f_w), lambda i, b: (blk0 + b, 0)),
                  pl.BlockSpec((tr, seq_len), lambda i, b: (i, 0)),
                  pl.BlockSpec((tr, seq_len), lambda i, b: (i, 0))],
        out_specs=pl.BlockSpec((tr, f_w), lambda i, b: (b * nt + i, 0)),
        compiler_params=_cparams(("parallel", "parallel")),
        name="fourier",
    )(xc, xs, cn, sn)


def _unit_tri_inverse(mats, m16, m32off, m64off, eye):
    ps = [-(a * m16) for a in mats]
    ts = [eye + p for p in ps]
    for _ in range(3):
        ps = [_mm(p, p) for p in ps]
        ts = [t + _mm(t, p) for t, p in zip(ts, ps)]
    for off in (m32off, m64off):
        us = [_mm(t, a * off) for t, a in zip(ts, mats)]
        ts = [t - _mm(u, t) for t, u in zip(ts, us)]
    return ts


def _delta_kernel(*refs, seq_len, dk, zero_init, write_state):
    (q_ref, k_ref, v_ref, z_ref, gcol_ref, grow_ref,
     wq_ref, wk_ref, wv_ref, bq_ref, bk_ref, bv_ref, og_ref) = refs[:13]
    pos = 13
    if not zero_init:
        s0_ref = refs[pos]
        pos += 1
    y_ref = refs[pos]
    pos += 1
    if write_state:
        sfin_ref = refs[pos]
        pos += 1
    qs, ks, vs, of_s, ob_s, st_s, pq_s, n_s = refs[pos:]

    n_chunks = seq_len // CHUNK
    rows = lax.broadcasted_iota(jnp.int32, (seq_len, dk), 0)

    def conv_silu(x_ref, w_ref, b_ref):
        x = x_ref[...]
        width = w_ref.shape[0]
        pad = (width - 1) // 2
        acc = jnp.zeros_like(x) + b_ref[...]
        for s in range(width):
            o = s - pad
            if o == 0:
                xs = x
            else:
                xs = pltpu.roll(x, (-o) % seq_len, axis=0)
                xs = jnp.where((rows + o >= 0) & (rows + o < seq_len), xs, 0.0)
            acc = acc + xs * w_ref[s:s + 1, :]
        return _silu(acc)

    def l2norm(x):
        return x * lax.rsqrt(jnp.sum(x * x, axis=-1, keepdims=True) + EPS)

    qs[...] = l2norm(conv_silu(q_ref, wq_ref, bq_ref)) * (dk ** -0.5)
    ks[...] = l2norm(conv_silu(k_ref, wk_ref, bk_ref))
    vs[...] = conv_silu(v_ref, wv_ref, bv_ref)
    if zero_init:
        st_s[...] = jnp.zeros_like(st_s)
    else:
        st_s[...] = s0_ref[...]

    ri = lax.broadcasted_iota(jnp.int32, (CHUNK, CHUNK), 0)
    ci = lax.broadcasted_iota(jnp.int32, (CHUNK, CHUNK), 1)
    eye = (ri == ci).astype(F32)
    same16 = (ri // 16) == (ci // 16)
    same32 = (ri // 32) == (ci // 32)
    m16 = same16.astype(F32)
    m32off = same32.astype(F32) - m16
    m64off = 1.0 - same32.astype(F32)
    incl = (ri >= ci, ri <= ci)
    strict = (ri > ci, ri < ci)
    n_beta = N_DIRS

    o_s = (of_s, ob_s)
    last_row = (CHUNK - 1, 0)

    def prep_group(first_chunk, chunks):
        chains = [(c, d) for c in chunks for d in range(N_DIRS)]
        cidx = {c: first_chunk + c for c in chunks}
        r0 = {c: pl.multiple_of(cidx[c] * CHUNK, CHUNK) for c in chunks}
        q = {c: qs[pl.ds(r0[c], CHUNK), :] for c in chunks}
        k = {c: ks[pl.ds(r0[c], CHUNK), :] for c in chunks}
        v = {c: vs[pl.ds(r0[c], CHUNK), :] for c in chunks}
        gc4 = {c: gcol_ref[pl.ds(r0[c], CHUNK), :] for c in chunks}
        gr4 = {c: grow_ref[cidx[c]] for c in chunks}
        kq = {}
        for c in chunks:
            k16 = k[c].astype(BF16)
            kq[c] = _mm_nt(jnp.concatenate([k16, q[c].astype(BF16)], axis=0), k16)
        beta, gcl, decay, a = {}, {}, {}, []
        for c, d in chains:
            beta[c, d] = gc4[c][:, d:d + 1]
            gcl[c, d] = gc4[c][:, n_beta + d:n_beta + d + 1]
            grw = gr4[c][n_beta + d:n_beta + d + 1, :]
            decay[c, d] = jnp.exp(jnp.where(incl[d], gcl[c, d] - grw, -jnp.inf))
            a.append(jnp.where(strict[d], beta[c, d] * kq[c][:CHUNK] * decay[c, d], 0.0))
        t = dict(zip(chains, _unit_tri_inverse(a, m16, m32off, m64off, eye)))
        eg = {cd: jnp.exp(gcl[cd]) for cd in chains}
        sol = {(c, d): _mm(t[c, d], jnp.concatenate([v[c] * beta[c, d], k[c] * (beta[c, d] * eg[c, d])], axis=1))
               for c, d in chains}
        aw = {(c, d): _mm(kq[c][CHUNK:] * decay[c, d], sol[c, d]) for c, d in chains}
        kuw = {}
        for c, d in chains:
            g_last = gcl[c, d][last_row[d]:last_row[d] + 1, :]
            kuw[c, d] = _mm_tn(k[c] * jnp.exp(g_last - gcl[c, d]), sol[c, d])
        for c, d in chains:
            pq_s[d, cidx[c]] = jnp.concatenate(
                [kuw[c, d][:, dk:], q[c] * eg[c, d] - aw[c, d][:, dk:]], axis=0).astype(BF16)
            n_s[d, cidx[c]] = kuw[c, d][:, :dk]
            o_s[d][pl.ds(r0[c], CHUNK), :] = aw[c, d][:, :dk]

    unroll = math.gcd(PREP_UNROLL, n_chunks)

    def prep_body(i, carry):
        prep_group(i * unroll, range(unroll))
        return carry

    lax.fori_loop(0, n_chunks // unroll, prep_body, 0)

    def scan_body(n, carry):
        for d in range(N_DIRS):
            c = n if d == 0 else n_chunks - 1 - n
            r0 = pl.multiple_of(c * CHUNK, CHUNK)
            g_last = grow_ref[c][n_beta + d:n_beta + d + 1, last_row[d]:last_row[d] + 1]
            s = st_s[d]
            r = _dot(pq_s[d, c], s.astype(BF16))
            st_s[d] = s * jnp.exp(g_last) + n_s[d, c] - r[:dk]
            o_s[d][pl.ds(r0, CHUNK), :] += r[dk:]
        return carry

    lax.fori_loop(0, n_chunks, scan_body, 0)

    o = _rms(of_s[...] + ob_s[...]) * og_ref[...]
    y_ref[...] = (o * _silu(z_ref[...])).astype(y_ref.dtype)
    if write_state:
        sfin_ref[...] = st_s[...]


def _delta(qkvz, gcol, grow, sc_w, sc_b, o_g, s0, layer, seq_len, n_batch, row_off, dk, write_state):
    nh = DN_HEADS
    blk0 = row_off // seq_len
    n_chunks = seq_len // CHUNK
    zero_init = s0 is None
    kern = functools.partial(_delta_kernel, seq_len=seq_len, dk=dk, zero_init=zero_init,
                             write_state=write_state)

    def col(group):
        return pl.BlockSpec((seq_len, dk), lambda b, h: (blk0 + b, group * nh + h))

    def wspec(group, rows):
        return pl.BlockSpec((rows, dk), lambda b, h: (0, group * nh + h))

    width = sc_w.shape[0]
    in_specs = [col(0), col(1), col(2), col(3),
                pl.BlockSpec((None, seq_len, 2 * N_DIRS), lambda b, h: (h, blk0 + b, 0)),
                pl.BlockSpec((None, n_chunks, 2 * N_DIRS, CHUNK), lambda b, h: (h, blk0 + b, 0, 0)),
                wspec(0, width), wspec(1, width), wspec(2, width),
                wspec(0, 1), wspec(1, 1), wspec(2, 1),
                pl.BlockSpec((1, dk), lambda b, h: (0, 0))]
    args = [qkvz, qkvz, qkvz, qkvz, gcol, grow, sc_w, sc_w, sc_w, sc_b, sc_b, sc_b, o_g]
    if not zero_init:
        in_specs.append(pl.BlockSpec((None, None, N_DIRS, None, dk, dk),
                                     lambda b, h: (b, layer, 0, h, 0, 0)))
        args.append(s0)
    y_shape = jax.ShapeDtypeStruct((n_batch * seq_len, nh * dk), BF16)
    y_spec = pl.BlockSpec((seq_len, dk), lambda b, h: (b, h))
    if write_state:
        out_shape = (y_shape, jax.ShapeDtypeStruct((n_batch, N_DIRS, nh, dk, dk), F32))
        out_specs = (y_spec, pl.BlockSpec((None, N_DIRS, None, dk, dk), lambda b, h: (b, 0, h, 0, 0)))
    else:
        out_shape = y_shape
        out_specs = y_spec
    return pl.pallas_call(
        kern,
        out_shape=out_shape,
        grid=(n_batch, nh),
        in_specs=in_specs,
        out_specs=out_specs,
        scratch_shapes=[pltpu.VMEM((seq_len, dk), F32)] * 5 + [
            pltpu.VMEM((N_DIRS, dk, dk), F32),
            pltpu.VMEM((N_DIRS, n_chunks, dk + CHUNK, dk), BF16),
            pltpu.VMEM((N_DIRS, n_chunks, dk, dk), F32)],
        compiler_params=_cparams(("parallel", "parallel")),
        name="delta",
    )(*args)


def _conformer_kernel(x_ref, w_ref, b_ref, lg_ref, lb_ref, o_ref, pad_s, *, n_seg, seg_blocks, halo):
    c = o_ref.shape[-1]
    width = w_ref.shape[0]
    pad = (width - 1) // 2
    seg_len = seg_blocks * GRID_W
    pad_s[:, 0:halo, :] = jnp.zeros((n_seg, halo, c), F32)
    pad_s[:, halo + seg_len:2 * halo + seg_len, :] = jnp.zeros((n_seg, halo, c), F32)
    for j in range(seg_blocks):
        if seg_blocks == 1:
            x = x_ref[...]
            pad_s[:, halo:halo + GRID_W, :] = x[..., :c] * _sigmoid(x[..., c:])
        else:
            x = x_ref[j]
            pad_s[0, halo + j * GRID_W:halo + (j + 1) * GRID_W, :] = x[:, :c] * _sigmoid(x[:, c:])

    win = GRID_W + 2 * halo

    def seg_body(r, carry):
        for j in range(seg_blocks):
            x = pad_s[r, j * GRID_W:j * GRID_W + win, :]
            shifted = [x] + [pltpu.roll(x, win - b, axis=0) for b in range(1, SUBLANE)]
            acc = jnp.zeros((GRID_W, c), F32) + b_ref[...]
            for s in range(width):
                o = halo - pad + s
                a0 = (o // SUBLANE) * SUBLANE
                acc = acc + shifted[o % SUBLANE][a0:a0 + GRID_W, :] * w_ref[s:s + 1, :]
            mu = jnp.mean(acc, axis=-1, keepdims=True)
            xc = acc - mu
            var = jnp.mean(xc * xc, axis=-1, keepdims=True)
            y = xc * lax.rsqrt(var + EPS) * lg_ref[...] + lb_ref[...]
            o_ref[r * seg_blocks + j] = _silu(y).astype(o_ref.dtype)
        return carry

    lax.fori_loop(0, n_seg, seg_body, 0)


def _conformer(cv, dw_w, dw_b, ln_g, ln_b, seq_len, n_batch, row_off, latent):
    t, c2 = cv.shape
    c = c2 // 2
    cv3 = cv.reshape(t // GRID_W, GRID_W, c2)
    blocks = seq_len // GRID_W
    n_seg, seg_blocks = (blocks, 1) if latent else (1, blocks)
    halo = 2 * SUBLANE
    assert (dw_w.shape[0] - 1) // 2 <= halo
    blk0 = row_off // seq_len
    kern = functools.partial(_conformer_kernel, n_seg=n_seg, seg_blocks=seg_blocks, halo=halo)
    const = lambda b: (0, 0)
    out = pl.pallas_call(
        kern,
        out_shape=jax.ShapeDtypeStruct((n_batch * blocks, GRID_W, c), BF16),
        grid=(n_batch,),
        in_specs=[pl.BlockSpec((blocks, GRID_W, c2), lambda b: (blk0 + b, 0, 0)),
                  pl.BlockSpec(dw_w.shape, const),
                  pl.BlockSpec((1, c), const), pl.BlockSpec((1, c), const), pl.BlockSpec((1, c), const)],
        out_specs=pl.BlockSpec((blocks, GRID_W, c), lambda b: (b, 0, 0)),
        scratch_shapes=[pltpu.VMEM((n_seg, seg_blocks * GRID_W + 2 * halo, c), F32)],
        compiler_params=_cparams(("parallel",)),
        name="conformer",
    )(cv3, dw_w, dw_b, ln_g, ln_b)
    return out.reshape(n_batch * seq_len, c)


def _outproj_kernel(*refs, n_x, d, n_ctx_tiles):
    yfc, yfl, ydc, ydl, ycc, ycl, w_ref, mod_ref, o_ref = refs[n_x:]
    is_ctx = pl.program_id(0) < n_ctx_tiles
    y = None
    k0 = 0
    for a_c, a_l in ((yfc, yfl), (ydc, ydl), (ycc, ycl)):
        kw = a_c.shape[1]
        a = jnp.where(is_ctx, a_c[...], a_l[...])
        part = _dot(a, w_ref[k0:k0 + kw, :])
        y = part if y is None else y + part
        k0 += kw
    o_ref[...] = _read_stream(refs[:n_x], n_ctx_tiles) + mod_ref[:, 2 * d:3 * d] * y


def _outproj(parts, w_out, x, mod, mod_row, tm, n_ctx_tiles):
    in_specs, args = _stream_specs(x, tm, n_ctx_tiles)
    t = sum(a.shape[0] for a in args)
    d = w_out.shape[1]
    kern = functools.partial(_outproj_kernel, n_x=len(args), d=d, n_ctx_tiles=n_ctx_tiles)
    for a_c, a_l in parts:
        kw = a_c.shape[1]
        in_specs.append(pl.BlockSpec((tm, kw), lambda i: (jnp.minimum(i, n_ctx_tiles - 1), 0)))
        in_specs.append(pl.BlockSpec((tm, kw), lambda i: (jnp.maximum(i - n_ctx_tiles, 0), 0)))
        args += [a_c, a_l]
    in_specs += [pl.BlockSpec(w_out.shape, lambda i: (0, 0)),
                 pl.BlockSpec((None, 1, N_MOD * d), lambda i: (mod_row(i), 0, 0))]
    args += [w_out, mod]
    return pl.pallas_call(
        kern,
        out_shape=jax.ShapeDtypeStruct((t, d), F32),
        grid=(t // tm,),
        in_specs=in_specs,
        out_specs=pl.BlockSpec((tm, d), lambda i: (i, 0)),
        compiler_params=_cparams(("parallel",)),
        name="outproj",
    )(*args)


def _ffn_kernel(*refs, d, final_norm):
    x_ref, g_ref, mod_ref, wg_ref, wu_ref, wd_ref = refs[:6]
    fg_ref = refs[6] if final_norm else None
    o_ref, h_s, acc_s = refs[-3:]
    f = pl.program_id(1)

    @pl.when(f == 0)
    def _():
        mod = mod_ref[...]
        h = _rms(x_ref[...]) * g_ref[...]
        h = h * (1.0 + mod[:, 4 * d:5 * d]) + mod[:, 3 * d:4 * d]
        h_s[...] = h.astype(BF16)
        acc_s[...] = jnp.zeros_like(acc_s)

    h = h_s[...]
    a = _silu(_dot(h, wg_ref[...].astype(BF16))) * _dot(h, wu_ref[...].astype(BF16))
    acc_s[...] += _dot(a.astype(BF16), wd_ref[...].astype(BF16))

    @pl.when(f == pl.num_programs(1) - 1)
    def _():
        o = x_ref[...] + mod_ref[:, 5 * d:6 * d] * acc_s[...]
        if final_norm:
            o = _rms(o) * fg_ref[...]
        o_ref[...] = o


def _ffn(x, gain, mod, mod_row, wg, wu, wd, final_gain, tm, tf):
    t, d = x.shape
    ff = wg.shape[1]
    final_norm = final_gain is not None
    kern = functools.partial(_ffn_kernel, d=d, final_norm=final_norm)
    row = lambda i, f: (i, 0)
    const = lambda i, f: (0, 0)
    in_specs = [pl.BlockSpec((tm, d), row),
                pl.BlockSpec((1, d), const),
                pl.BlockSpec((None, 1, N_MOD * d), lambda i, f: (mod_row(i), 0, 0)),
                pl.BlockSpec((d, tf), lambda i, f: (0, f)),
                pl.BlockSpec((d, tf), lambda i, f: (0, f)),
                pl.BlockSpec((tf, d), lambda i, f: (f, 0))]
    args = [x, gain, mod, wg, wu, wd]
    if final_norm:
        in_specs.append(pl.BlockSpec((1, d), const))
        args.append(final_gain)
    return pl.pallas_call(
        kern,
        out_shape=jax.ShapeDtypeStruct((t, d), F32),
        grid=(t // tm, ff // tf),
        in_specs=in_specs,
        out_specs=pl.BlockSpec((tm, d), row),
        scratch_shapes=[pltpu.VMEM((tm, d), BF16), pltpu.VMEM((tm, d), F32)],
        compiler_params=_cparams(("parallel", "arbitrary")),
        name="ffn",
    )(*args)


INFO_E, INFO_W, INFO_R = 0, 2, 4
MOE_TILE = 1024
MOE_SUB = 256


def _route_kernel(x_ref, g_ref, mod_ref, router_ref, h_o, info_o, cnt_o, cnt_s, *, d, n_exp):
    i = pl.program_id(0)

    @pl.when(i == 0)
    def _():
        cnt_s[...] = jnp.zeros_like(cnt_s)

    mod = mod_ref[...]
    h = _rms(x_ref[...]) * g_ref[...]
    h = h * (1.0 + mod[:, 4 * d:5 * d]) + mod[:, 3 * d:4 * d]
    h_o[...] = h

    logits = _mm3(h, router_ref[...])
    tm = logits.shape[0]
    lane = lax.broadcasted_iota(jnp.int32, logits.shape, 1)
    lg = jnp.where(lane < n_exp, logits, -jnp.inf)
    m1 = jnp.max(lg, axis=-1, keepdims=True)
    i1 = jnp.min(jnp.where(lg == m1, lane, LANE), axis=-1, keepdims=True)
    lg2 = jnp.where(lane == i1, -jnp.inf, lg)
    m2 = jnp.max(lg2, axis=-1, keepdims=True)
    i2 = jnp.min(jnp.where(lg2 == m2, lane, LANE), axis=-1, keepdims=True)
    w1 = 1.0 / (1.0 + jnp.exp(m2 - m1))

    sel1 = lane == i1
    sel2 = lane == i2
    member = jnp.where(sel1 | sel2, 1.0, 0.0)
    ri = lax.broadcasted_iota(jnp.int32, (tm, tm), 0)
    ci = lax.broadcasted_iota(jnp.int32, (tm, tm), 1)
    before = jnp.where(ri > ci, 1.0, 0.0)
    rank = cnt_s[...] + _mm(before, member)
    r1 = jnp.sum(jnp.where(sel1, rank, 0.0), axis=-1, keepdims=True)
    r2 = jnp.sum(jnp.where(sel2, rank, 0.0), axis=-1, keepdims=True)
    cnt_s[...] += jnp.sum(member, axis=0, keepdims=True)
    cnt_o[...] = jnp.broadcast_to(cnt_s[...], cnt_o.shape)

    info = jnp.zeros(logits.shape, F32)
    for ln, val in ((INFO_E, i1.astype(F32)), (INFO_E + 1, i2.astype(F32)), (INFO_W, w1),
                    (INFO_W + 1, 1.0 - w1), (INFO_R, r1), (INFO_R + 1, r2)):
        info = jnp.where(lane == ln, val, info)
    info_o[...] = info


def _route(x, gain, mod, mod_row, router, tm):
    t, d = x.shape
    n_exp = router.shape[1]
    router_p = jnp.zeros((d, LANE), F32).at[:, :n_exp].set(router)
    kern = functools.partial(_route_kernel, d=d, n_exp=n_exp)
    return pl.pallas_call(
        kern,
        out_shape=(jax.ShapeDtypeStruct((t, d), F32), jax.ShapeDtypeStruct((t, LANE), F32),
                   jax.ShapeDtypeStruct((SUBLANE, LANE), F32)),
        grid=(t // tm,),
        in_specs=[pl.BlockSpec((tm, d), lambda i: (i, 0)),
                  pl.BlockSpec((1, d), lambda i: (0, 0)),
                  pl.BlockSpec((None, 1, N_MOD * d), lambda i: (mod_row(i), 0, 0)),
                  pl.BlockSpec((d, LANE), lambda i: (0, 0))],
        out_specs=(pl.BlockSpec((tm, d), lambda i: (i, 0)), pl.BlockSpec((tm, LANE), lambda i: (i, 0)),
                   pl.BlockSpec((SUBLANE, LANE), lambda i: (0, 0))),
        scratch_shapes=[pltpu.VMEM((1, LANE), F32)],
        compiler_params=_cparams(("arbitrary",)),
        name="moe_route",
    )(x, gain, mod, router_p)


def _row_copy(src_hbm, row, dst, slot, sem):
    return pltpu.make_async_copy(src_hbm.at[pl.ds(row, 1), :], dst.at[pl.ds(slot, 1), :], sem)


def _experts_kernel(expert_sm, nvalid_sm, tok_ref, tok_next_ref, h_hbm, wg_ref, wu_ref, wd_ref, y_ref,
                    hbuf, h16, acc, wg16, wu16, wd16, sems, *, sub, n_f):
    del expert_sm
    i = pl.program_id(0)
    j = pl.program_id(1)
    n_i = pl.num_programs(0)
    tm = h16.shape[0]
    slot = i % 2
    nxt = 1 - slot
    nv = nvalid_sm[i]
    n_sub = (nv + sub - 1) // sub
    has_next = jnp.where(i + 1 < n_i, nvalid_sm[jnp.minimum(i + 1, n_i - 1)], 0) > 0
    per_step = -(-tm // n_f)
    base = j * per_step

    def start_row(tok, r, buf):
        _row_copy(h_hbm, tok[0, r], hbuf.at[buf], r, sems.at[buf]).start()

    def fetch_loop(tok, lo, hi, buf):
        def issue(r, carry):
            start_row(tok, r, buf)
            return carry

        lax.fori_loop(lo, hi, issue, 0)

    @pl.when(j == 0)
    def _():
        @pl.when((i == 0) & (nv > 0))
        def _():
            fetch_loop(tok_ref, 0, tm, 0)

        acc[...] = jnp.zeros_like(acc)

        @pl.when(nv > 0)
        def _():
            pltpu.make_async_copy(h_hbm.at[pl.ds(0, tm), :], hbuf.at[slot], sems.at[slot]).wait()

            def cast(s, carry):
                rows = pl.ds(pl.multiple_of(s * sub, sub), sub)
                h16[rows, :] = hbuf[slot, rows, :].astype(BF16)
                return carry

            lax.fori_loop(0, n_sub, cast, 0)

    fast = (n_sub == tm // sub) & has_next

    @pl.when(fast)
    def _():
        n_safe = tm - (n_f - 1) * per_step
        for r in range(min(per_step, n_safe)):
            start_row(tok_next_ref, base + r, nxt)
        h = h16[...]
        a = _silu(_dot(h, wg_ref[...].astype(BF16))) * _dot(h, wu_ref[...].astype(BF16))
        acc[...] += _dot(a.astype(BF16), wd_ref[...].astype(BF16))
        for r in range(n_safe, per_step):
            pl.when(base + r < tm)(functools.partial(start_row, tok_next_ref, base + r, nxt))

    @pl.when(jnp.logical_not(fast))
    def _():
        @pl.when(has_next)
        def _():
            fetch_loop(tok_next_ref, base, jnp.minimum(base + per_step, tm), nxt)

        @pl.when(n_sub > 0)
        def _():
            wg16[...] = wg_ref[...].astype(BF16)
            wu16[...] = wu_ref[...].astype(BF16)
            wd16[...] = wd_ref[...].astype(BF16)

            def block(s, carry):
                rows = pl.ds(pl.multiple_of(s * sub, sub), sub)
                h = h16[rows, :]
                a = _silu(_dot(h, wg16[...])) * _dot(h, wu16[...])
                acc[rows, :] += _dot(a.astype(BF16), wd16[...])
                return carry

            lax.fori_loop(0, n_sub, block, 0)

    @pl.when(j == n_f - 1)
    def _():
        y_ref[...] = acc[...]


def _experts(h, src_tok, tile_expert, tile_nvalid, wg, wu, wd, tm, tf, sub):
    t, d = h.shape
    n_exp, _, ff = wg.shape
    n_tiles = tile_expert.shape[0]
    n_f = ff // tf

    def w_idx(i, j, expert_sm, nvalid_sm):
        return expert_sm[i], jnp.where(nvalid_sm[i] > 0, j, n_f - 1)

    def wgu_map(i, j, expert_sm, nvalid_sm):
        e, jj = w_idx(i, j, expert_sm, nvalid_sm)
        return e, 0, jj

    def wd_map(i, j, expert_sm, nvalid_sm):
        e, jj = w_idx(i, j, expert_sm, nvalid_sm)
        return e, jj, 0

    kern = functools.partial(_experts_kernel, sub=sub, n_f=n_f)
    tok = src_tok.reshape(n_tiles, 1, tm)
    grid_spec = pltpu.PrefetchScalarGridSpec(
        num_scalar_prefetch=2,
        grid=(n_tiles, n_f),
        in_specs=[pl.BlockSpec((None, 1, tm), lambda i, j, *_: (i, 0, 0), memory_space=pltpu.SMEM),
                  pl.BlockSpec((None, 1, tm), lambda i, j, *_: (jnp.minimum(i + 1, n_tiles - 1), 0, 0),
                               memory_space=pltpu.SMEM),
                  pl.BlockSpec(memory_space=pl.ANY),
                  pl.BlockSpec((None, d, tf), wgu_map),
                  pl.BlockSpec((None, d, tf), wgu_map),
                  pl.BlockSpec((None, tf, d), wd_map)],
        out_specs=pl.BlockSpec((tm, d), lambda i, j, *_: (i, 0)),
        scratch_shapes=[pltpu.VMEM((2, tm, d), F32), pltpu.VMEM((tm, d), BF16), pltpu.VMEM((tm, d), F32),
                        pltpu.VMEM((d, tf), BF16), pltpu.VMEM((d, tf), BF16), pltpu.VMEM((tf, d), BF16),
                        pltpu.SemaphoreType.DMA((2,))])
    return pl.pallas_call(
        kern,
        out_shape=jax.ShapeDtypeStruct((n_tiles * tm, d), F32),
        grid_spec=grid_spec,
        compiler_params=_cparams(("arbitrary", "arbitrary")),
        name="moe_experts",
    )(tile_expert, tile_nvalid, tok, tok, h, wg, wu, wd)


def _combine_kernel(*refs, d, final_norm, split_tiles):
    pos_ref, pos_next_ref, info_ref, x_ref, mod_ref = refs[:5]
    p = 5
    if final_norm:
        fg_ref = refs[p]
        p += 1
    y_hbm = refs[p]
    o_refs = refs[p + 1:-2]
    ybuf, sems = refs[-2:]
    tm = x_ref.shape[0]
    i = pl.program_id(0)
    slot = i % 2

    def fetch(pos, buf):
        def issue(r, carry):
            for k in range(TOP_K):
                _row_copy(y_hbm, pos[0, k * tm + r], ybuf.at[buf, k], r, sems.at[buf]).start()
            return carry

        lax.fori_loop(0, tm, issue, 0, unroll=8)

    @pl.when(i == 0)
    def _():
        fetch(pos_ref, 0)

    @pl.when(i + 1 < pl.num_programs(0))
    def _():
        fetch(pos_next_ref, 1 - slot)

    for k in range(TOP_K):
        pltpu.make_async_copy(y_hbm.at[pl.ds(0, tm), :], ybuf.at[slot, k], sems.at[slot]).wait()

    info = info_ref[...]
    f = None
    for k in range(TOP_K):
        term = info[:, INFO_W + k:INFO_W + k + 1] * ybuf[slot, k]
        f = term if f is None else f + term
    o = x_ref[...] + mod_ref[:, 5 * d:6 * d] * f
    if final_norm:
        o = _rms(o) * fg_ref[...]
    if split_tiles is None:
        o_refs[0][...] = o
    else:
        @pl.when(i < split_tiles)
        def _():
            o_refs[0][...] = o

        @pl.when(i >= split_tiles)
        def _():
            o_refs[1][...] = o


def _combine(x, info, pos, y, mod, mod_row, final_gain, tm, split_rows):
    t, d = x.shape
    final_norm = final_gain is not None
    split_tiles = None if split_rows is None else split_rows // tm
    kern = functools.partial(_combine_kernel, d=d, final_norm=final_norm, split_tiles=split_tiles)
    n_tiles = t // tm
    if split_tiles is None:
        out_shape = jax.ShapeDtypeStruct((t, d), F32)
        out_specs = pl.BlockSpec((tm, d), lambda i: (i, 0))
    else:
        out_shape = (jax.ShapeDtypeStruct((split_rows, d), F32), jax.ShapeDtypeStruct((t - split_rows, d), F32))
        out_specs = (pl.BlockSpec((tm, d), lambda i: (jnp.minimum(i, split_tiles - 1), 0)),
                     pl.BlockSpec((tm, d), lambda i: (jnp.maximum(i - split_tiles, 0), 0)))
    in_specs = [pl.BlockSpec((None, 1, TOP_K * tm), lambda i: (i, 0, 0), memory_space=pltpu.SMEM),
                pl.BlockSpec((None, 1, TOP_K * tm), lambda i: (jnp.minimum(i + 1, n_tiles - 1), 0, 0),
                             memory_space=pltpu.SMEM),
                pl.BlockSpec((tm, LANE), lambda i: (i, 0)),
                pl.BlockSpec((tm, d), lambda i: (i, 0)),
                pl.BlockSpec((None, 1, N_MOD * d), lambda i: (mod_row(i), 0, 0))]
    args = [pos, pos, info, x, mod]
    if final_norm:
        in_specs.append(pl.BlockSpec((1, d), lambda i: (0, 0)))
        args.append(final_gain)
    in_specs.append(pl.BlockSpec(memory_space=pl.ANY))
    args.append(y)
    return pl.pallas_call(
        kern,
        out_shape=out_shape,
        grid=(t // tm,),
        in_specs=in_specs,
        out_specs=out_specs,
        scratch_shapes=[pltpu.VMEM((2, TOP_K, tm, d), F32), pltpu.SemaphoreType.DMA((2,))],
        compiler_params=_cparams(("arbitrary",)),
        name="moe_combine",
    )(*args)


def _moe(x, gain, mod, mod_row_fn, router, wg, wu, wd, final_gain, split_rows, tm_route, tm_exp, tm_comb, tf, sub):
    t, d = x.shape
    n_exp = wg.shape[0]
    h, info, cnt = _route(x, gain, mod, mod_row_fn(tm_route), router, tm_route)

    counts = cnt[0, :n_exp].astype(jnp.int32)
    tiles_per = (counts + tm_exp - 1) // tm_exp
    tile_end = jnp.cumsum(tiles_per)
    tile_start = tile_end - tiles_per
    n_tiles = (TOP_K * t) // tm_exp + n_exp
    tile_id = jnp.arange(n_tiles, dtype=jnp.int32)
    tile_expert = jnp.minimum(jnp.sum(tile_id[:, None] >= tile_end[None, :], axis=1), n_exp - 1).astype(jnp.int32)
    tile_nvalid = jnp.where(tile_id < tile_end[n_exp - 1],
                            jnp.clip(counts[tile_expert] - (tile_id - tile_start[tile_expert]) * tm_exp, 0, tm_exp),
                            0).astype(jnp.int32)
    slot_start = tile_start * tm_exp
    experts = info[:, INFO_E:INFO_E + TOP_K].astype(jnp.int32)
    slots = slot_start[experts] + info[:, INFO_R:INFO_R + TOP_K].astype(jnp.int32)
    token = jnp.broadcast_to(jnp.arange(t, dtype=jnp.int32)[:, None], slots.shape)
    src_tok = jnp.zeros((n_tiles * tm_exp,), jnp.int32).at[slots.reshape(-1)].set(token.reshape(-1))

    y = _experts(h, src_tok, tile_expert, tile_nvalid, wg, wu, wd, tm_exp, tf, sub)
    pos = slots.reshape(t // tm_comb, tm_comb, TOP_K).transpose(0, 2, 1).reshape(t // tm_comb, 1, TOP_K * tm_comb)
    return _combine(x, info, pos, y, mod, mod_row_fn(tm_comb), final_gain, tm_comb, split_rows)


def _lane_row(vals, offset):
    n = vals.shape[0]
    return jnp.zeros((1, LANE), F32).at[0, offset:offset + n].set(vals.astype(F32))


def kernel(x_prompt, x_sample, state_delta, c, c_ctx, norm1, norm2, w_mod, b_mod, w_in, sc_w, sc_b, dn_a_log, dn_dt_bias, dn_norm, cv_dw_w, cv_dw_b, cv_ln_g, cv_ln_b, w_out, ffn_wg, ffn_wu, ffn_wd, moe_router, moe_wg, moe_wu, moe_wd, final_norm):
    b_ctx, l_ctx, d = x_prompt.shape
    b_lat, l_lat, _ = x_sample.shape
    depth = w_in.shape[0]
    t_ctx = b_ctx * l_ctx
    t_lat = b_lat * l_lat
    f_w = d // 4
    f_gw = f_w // F_GROUPS
    dn_w = d // 2
    dk = dn_w // DN_HEADS
    cv_w = d // 4
    qkvz_w = 4 * dn_w
    n_gate = 2 * N_DIRS * DN_HEADS
    n_exp = moe_wg.shape[1]
    assert dk == LANE and t_ctx % l_lat == 0 and b_lat + 1 <= MOD_ROWS
    assert l_ctx % CHUNK == 0 and l_lat % CHUNK == 0 and l_lat % GRID_W == 0 and CHUNK == GRID_W

    tm = _pick_tile(512, t_ctx, l_lat)
    tm_ffn = _pick_tile(1024, t_ctx, l_lat)

    def mod_row_fn(tile):
        n_ctx_tiles = t_ctx // tile
        per_seq = l_lat // tile
        return lambda i: jnp.where(i < n_ctx_tiles, 0, 1 + (i - n_ctx_tiles) // per_seq)

    x = (x_prompt.reshape(t_ctx, d), x_sample.reshape(t_lat, d))
    cvec = jnp.zeros((MOD_ROWS, d), F32).at[0].set(c_ctx).at[1:1 + b_lat].set(c)
    mods = _adaln(cvec, w_mod, b_mod).reshape(depth, MOD_ROWS, 1, N_MOD * d)

    cg, sg = _dft_tables(f_gw)
    eye_g = np.eye(F_GROUPS)
    ccs = jnp.asarray(np.concatenate([np.kron(eye_g, cg), np.kron(eye_g, sg)], axis=1), F32).astype(BF16)

    passes = ((l_ctx, b_ctx, 0, False), (l_lat, b_lat, t_ctx, True))
    o_q = f_w
    o_ba = f_w + qkvz_w
    o_cv = o_ba + n_gate
    new_states = []
    for l in range(depth):
        w = w_in[l]
        w_perm = jnp.concatenate(
            [w[:, :o_ba], w[:, o_cv:o_cv + 2 * cv_w], w[:, o_ba:o_cv], jnp.zeros((d, LANE - n_gate), F32)],
            axis=1).astype(BF16)
        alog = _lane_row(dn_a_log[l].reshape(-1), N_DIRS * DN_HEADS)
        dtb = _lane_row(dn_dt_bias[l].reshape(-1), N_DIRS * DN_HEADS)
        xc, xs, qkvz, cv, gates = _proj(x, norm1[l][None], mods[l], w_perm, ccs, alog, dtb,
                                        mod_row_fn(tm), tm, t_ctx // tm, f_w, qkvz_w, 2 * cv_w)
        g16 = gates[:, :n_gate].reshape(-1, 2 * N_DIRS, DN_HEADS)
        gcol = g16.transpose(2, 0, 1)
        grow = g16.reshape(-1, CHUNK, 2 * N_DIRS, DN_HEADS).transpose(3, 0, 2, 1)

        parts = [[], [], []]
        for seq_len, n_batch, row_off, latent in passes:
            parts[0].append(_fourier(xc, xs, seq_len, n_batch, row_off, f_gw))
            res = _delta(qkvz, gcol, grow, sc_w[l], sc_b[l][None], dn_norm[l][None],
                         state_delta if latent else None, l, seq_len, n_batch, row_off, dk,
                         write_state=not latent)
            if latent:
                parts[1].append(res)
            else:
                parts[1].append(res[0])
                new_states.append(res[1])
            parts[2].append(_conformer(cv, cv_dw_w[l], cv_dw_b[l][None], cv_ln_g[l][None],
                                       cv_ln_b[l][None], seq_len, n_batch, row_off, latent))
        x = _outproj(parts, w_out[l].astype(BF16), x, mods[l], mod_row_fn(tm), tm, t_ctx // tm)

        fg = final_norm[None] if l == depth - 1 else None
        if l % 2 == 0:
            i = l // 2
            x = _ffn(x, norm2[l][None], mods[l], mod_row_fn(tm_ffn), ffn_wg[i], ffn_wu[i], ffn_wd[i], fg,
                     tm_ffn, _pick_tile(512, ffn_wg.shape[-1]))
        else:
            i = l // 2
            x = _moe(x, norm2[l][None], mods[l], mod_row_fn, moe_router[i], moe_wg[i], moe_wu[i], moe_wd[i], fg,
                     split_rows=t_ctx if l == depth - 1 else None,
                     tm_route=tm, tm_exp=MOE_TILE, tm_comb=_pick_tile(256, t_ctx, l_lat),
                     tf=_pick_tile(512, moe_wg.shape[-1]), sub=MOE_SUB)

    y_ctx, y_lat = x if isinstance(x, tuple) else (x[:t_ctx], x[t_ctx:])
    y_prompt = y_ctx.reshape(b_ctx, l_ctx, d)
    y_sample = y_lat.reshape(b_lat, l_lat, d)
    return y_prompt, y_sample, jnp.stack(new_states, axis=1)
```

```python
import functools
import math

import numpy as np
import jax
import jax.numpy as jnp
from jax import lax
from jax.experimental import pallas as pl
from jax.experimental.pallas import tpu as pltpu

F32 = jnp.float32
BF16 = jnp.bfloat16
EPS = 1e-6

LANE = 128
SUBLANE = 8
VMEM_LIMIT = 56 * 1024 * 1024

CHUNK = 64
GRID_W = 64
F_GROUPS = 4
DN_HEADS = 4
N_DIRS = 2
PREP_UNROLL = 16
TOP_K = 2
N_MOD = 6
MOD_ROWS = 16


def _cparams(sem):
    return pltpu.CompilerParams(dimension_semantics=sem, vmem_limit_bytes=VMEM_LIMIT)


def _pick_tile(limit, *sizes):
    t = limit
    while any(s % t for s in sizes):
        t //= 2
    return t


def _dot(a, b):
    return jnp.dot(a, b, preferred_element_type=F32)


def _mm(a, b):
    return jnp.dot(a.astype(BF16), b.astype(BF16), preferred_element_type=F32)


def _mm_nt(a, b):
    return lax.dot_general(a.astype(BF16), b.astype(BF16), (((1,), (1,)), ((), ())),
                           preferred_element_type=F32)


def _mm_tn(a, b):
    return lax.dot_general(a.astype(BF16), b.astype(BF16), (((0,), (0,)), ((), ())),
                           preferred_element_type=F32)


def _split(a):
    hi = a.astype(BF16)
    lo = (a - hi.astype(F32)).astype(BF16)
    return hi, lo


def _mm3(a, b):
    ah, al = _split(a)
    bh, bl = _split(b)
    return _dot(ah, bh) + (_dot(ah, bl) + _dot(al, bh))


def _sigmoid(x):
    return 1.0 / (1.0 + jnp.exp(-x))


def _silu(x):
    return x * _sigmoid(x)


def _rms(x):
    return x * lax.rsqrt(jnp.mean(x * x, axis=-1, keepdims=True) + EPS)


def _stream_specs(x, tm, n_ctx_tiles):
    if isinstance(x, tuple):
        d = x[0].shape[1]
        return ([pl.BlockSpec((tm, d), lambda i, *_: (jnp.minimum(i, n_ctx_tiles - 1), 0)),
                 pl.BlockSpec((tm, d), lambda i, *_: (jnp.maximum(i - n_ctx_tiles, 0), 0))], list(x))
    return [pl.BlockSpec((tm, x.shape[1]), lambda i, *_: (i, 0))], [x]


def _read_stream(x_refs, n_ctx_tiles):
    if len(x_refs) == 2:
        return jnp.where(pl.program_id(0) < n_ctx_tiles, x_refs[0][...], x_refs[1][...])
    return x_refs[0][...]


def _adaln_kernel(c_ref, w_ref, b_ref, o_ref):
    o_ref[...] = _mm(_silu(c_ref[...]), w_ref[...]) + b_ref[...]


def _adaln(cvec, w_mod, b_mod):
    depth, d, n = w_mod.shape
    tn = _pick_tile(1024, n)
    return pl.pallas_call(
        _adaln_kernel,
        out_shape=jax.ShapeDtypeStruct((depth, MOD_ROWS, n), F32),
        grid=(depth, n // tn),
        in_specs=[pl.BlockSpec((MOD_ROWS, d), lambda l, j: (0, 0)),
                  pl.BlockSpec((None, d, tn), lambda l, j: (l, 0, j)),
                  pl.BlockSpec((None, 1, tn), lambda l, j: (l, 0, j))],
        out_specs=pl.BlockSpec((None, MOD_ROWS, tn), lambda l, j: (l, 0, j)),
        compiler_params=_cparams(("parallel", "parallel")),
        name="adaln",
    )(cvec, w_mod, b_mod.reshape(depth, 1, n))


def _proj_kernel(*refs, n_x, n_ctx_tiles, d, f_w, qkvz_w, cv_w):
    g_ref, mod_ref, w_ref, ccs_ref, alog_ref, dtb_ref, xc_o, xs_o, qkvz_o, cv_o, gate_o = refs[n_x:]
    x = _read_stream(refs[:n_x], n_ctx_tiles)
    mod = mod_ref[...]
    h = _rms(x) * g_ref[...]
    h = (h * (1.0 + mod[:, d:2 * d]) + mod[:, 0:d]).astype(BF16)

    xf = _dot(h, w_ref[:, 0:f_w])
    xcs = _dot(xf.astype(BF16), ccs_ref[...])
    xc_o[...] = xcs[:, :f_w].astype(xc_o.dtype)
    xs_o[...] = xcs[:, f_w:].astype(xs_o.dtype)
    o = f_w
    step = 4 * LANE
    for n0 in range(0, qkvz_w, step):
        qkvz_o[:, n0:n0 + step] = _dot(h, w_ref[:, o + n0:o + n0 + step])
    o += qkvz_w
    cv_o[...] = _dot(h, w_ref[:, o:o + cv_w])
    o += cv_w
    ba = _dot(h, w_ref[:, o:o + LANE])

    tm = ba.shape[0]
    lane = lax.broadcasted_iota(jnp.int32, ba.shape, 1)
    row = lax.broadcasted_iota(jnp.int32, ba.shape, 0) % CHUNK
    beta = _sigmoid(ba)
    t = ba + dtb_ref[...]
    softplus = jnp.maximum(t, 0.0) + jnp.log1p(jnp.exp(-jnp.abs(t)))
    g = -jnp.exp(alog_ref[...]) * softplus
    cf = g
    cb = g
    s = 1
    while s < CHUNK:
        cf = cf + jnp.where(row >= s, pltpu.roll(cf, s, axis=0), 0.0)
        cb = cb + jnp.where(row < CHUNK - s, pltpu.roll(cb, tm - s, axis=0), 0.0)
        s *= 2
    n_beta = N_DIRS * DN_HEADS
    gate_o[...] = jnp.where(lane < n_beta, beta, jnp.where(lane < n_beta + DN_HEADS, cf, cb))


def _proj(x, gain, mod, w_perm, ccs, alog, dtb, mod_row, tm, n_ctx_tiles, f_w, qkvz_w, cv_w):
    x_specs, x_args = _stream_specs(x, tm, n_ctx_tiles)
    t = sum(a.shape[0] for a in x_args)
    d, n = w_perm.shape
    kern = functools.partial(_proj_kernel, n_x=len(x_args), n_ctx_tiles=n_ctx_tiles, d=d, f_w=f_w,
                             qkvz_w=qkvz_w, cv_w=cv_w)
    row = lambda i: (i, 0)
    const = lambda i: (0, 0)
    return pl.pallas_call(
        kern,
        out_shape=(jax.ShapeDtypeStruct((t, f_w), BF16), jax.ShapeDtypeStruct((t, f_w), BF16),
                   jax.ShapeDtypeStruct((t, qkvz_w), F32), jax.ShapeDtypeStruct((t, cv_w), F32),
                   jax.ShapeDtypeStruct((t, LANE), F32)),
        grid=(t // tm,),
        in_specs=x_specs + [
            pl.BlockSpec((1, d), const),
            pl.BlockSpec((None, 1, N_MOD * d), lambda i: (mod_row(i), 0, 0)),
            pl.BlockSpec((d, n), const),
            pl.BlockSpec(ccs.shape, const),
            pl.BlockSpec((1, LANE), const),
            pl.BlockSpec((1, LANE), const)],
        out_specs=(pl.BlockSpec((tm, f_w), row), pl.BlockSpec((tm, f_w), row),
                   pl.BlockSpec((tm, qkvz_w), row), pl.BlockSpec((tm, cv_w), row),
                   pl.BlockSpec((tm, LANE), row)),
        compiler_params=_cparams(("parallel",)),
        name="proj",
    )(*x_args, gain, mod, w_perm, ccs, alog, dtb)


def _fourier_kernel(xc_ref, xs_ref, cn_ref, sn_ref, o_ref, *, scale):
    y = _dot(cn_ref[...], xc_ref[...]) - _dot(sn_ref[...], xs_ref[...])
    o_ref[...] = (y * scale).astype(o_ref.dtype)


def _dft_tables(n):
    j = np.arange(n, dtype=np.int64)
    ang = (2.0 * np.pi / n) * ((j[:, None] * j[None, :]) % n).astype(np.float64)
    return np.cos(ang), np.sin(ang)


def _fourier(xc, xs, seq_len, n_batch, row_off, f_gw):
    f_w = xc.shape[1]
    cn, sn = _dft_tables(seq_len)
    cn = jnp.asarray(cn, F32).astype(BF16)
    sn = jnp.asarray(sn, F32).astype(BF16)
    tr = _pick_tile(1024, seq_len)
    nt = seq_len // tr
    blk0 = row_off // seq_len
    kern = functools.partial(_fourier_kernel, scale=1.0 / math.sqrt(seq_len * f_gw))
    return pl.pallas_call(
        kern,
        out_shape=jax.ShapeDtypeStruct((n_batch * seq_len, f_w), BF16),
        grid=(nt, n_batch),
        in_specs=[pl.BlockSpec((seq_len, f_w), lambda i, b: (blk0 + b, 0)),
                  pl.BlockSpec((seq_len, f_w), lambda i, b: (blk0 + b, 0)),
                  pl.BlockSpec((tr, seq_len), lambda i, b: (i, 0)),
                  pl.BlockSpec((tr, seq_len), lambda i, b: (i, 0))],
        out_specs=pl.BlockSpec((tr, f_w), lambda i, b: (b * nt + i, 0)),
        compiler_params=_cparams(("parallel", "parallel")),
        name="fourier",
    )(xc, xs, cn, sn)


def _inverse_consts():
    ri = lax.broadcasted_iota(jnp.int32, (CHUNK, CHUNK), 0)
    ci = lax.broadcasted_iota(jnp.int32, (CHUNK, CHUNK), 1)
    eye = (ri == ci).astype(F32)
    same32 = (ri // 32) == (ci // 32)
    m16 = ((ri // 16) == (ci // 16)).astype(F32)
    return eye, m16, (same32.astype(F32) - m16, 1.0 - same32.astype(F32))


def _unit_tri_inverse(mats, consts):
    eye, m16, offs = consts
    ps = [-(a * m16) for a in mats]
    ts = [eye + p for p in ps]
    for _ in range(3):
        ps = [_mm(p, p) for p in ps]
        ts = [t + _mm(t, p) for t, p in zip(ts, ps)]
    for off in offs:
        us = [_mm(t, a * off) for t, a in zip(ts, mats)]
        ts = [t - _mm(u, t) for t, u in zip(ts, us)]
    return ts


def _delta_kernel(*refs, seq_len, dk, zero_init, write_state):
    (q_ref, k_ref, v_ref, z_ref, gcol_ref, grow_ref,
     wq_ref, wk_ref, wv_ref, bq_ref, bk_ref, bv_ref, og_ref) = refs[:13]
    pos = 13
    if not zero_init:
        s0_ref = refs[pos]
        pos += 1
    y_ref = refs[pos]
    pos += 1
    if write_state:
        sfin_ref = refs[pos]
        pos += 1
    qs, ks, vs, of_s, ob_s, st_s, pq_s, n_s = refs[pos:]

    n_chunks = seq_len // CHUNK
    rows = lax.broadcasted_iota(jnp.int32, (seq_len, dk), 0)

    def conv_silu(x_ref, w_ref, b_ref):
        x = x_ref[...]
        width = w_ref.shape[0]
        pad = (width - 1) // 2
        acc = jnp.zeros_like(x) + b_ref[...]
        for s in range(width):
            o = s - pad
            if o == 0:
                xs = x
            else:
                xs = pltpu.roll(x, (-o) % seq_len, axis=0)
                xs = jnp.where((rows + o >= 0) & (rows + o < seq_len), xs, 0.0)
            acc = acc + xs * w_ref[s:s + 1, :]
        return _silu(acc)

    def l2norm(x):
        return x * lax.rsqrt(jnp.sum(x * x, axis=-1, keepdims=True) + EPS)

    qs[...] = l2norm(conv_silu(q_ref, wq_ref, bq_ref)) * (dk ** -0.5)
    ks[...] = l2norm(conv_silu(k_ref, wk_ref, bk_ref))
    vs[...] = conv_silu(v_ref, wv_ref, bv_ref)
    if zero_init:
        st_s[...] = jnp.zeros_like(st_s)
    else:
        st_s[...] = s0_ref[...]

    ri = lax.broadcasted_iota(jnp.int32, (CHUNK, CHUNK), 0)
    ci = lax.broadcasted_iota(jnp.int32, (CHUNK, CHUNK), 1)
    incl = (ri >= ci, ri <= ci)
    unroll = math.gcd(PREP_UNROLL, n_chunks)
    inv_consts = _inverse_consts()
    strict = (ri > ci, ri < ci)
    n_beta = N_DIRS

    o_s = (of_s, ob_s)
    last_row = (CHUNK - 1, 0)

    def prep_group(first_chunk, chunks):
        chains = [(c, d) for c in chunks for d in range(N_DIRS)]
        cidx = {c: first_chunk + c for c in chunks}
        r0 = {c: pl.multiple_of(cidx[c] * CHUNK, CHUNK) for c in chunks}
        q = {c: qs[pl.ds(r0[c], CHUNK), :] for c in chunks}
        k = {c: ks[pl.ds(r0[c], CHUNK), :] for c in chunks}
        v = {c: vs[pl.ds(r0[c], CHUNK), :] for c in chunks}
        gc4 = {c: gcol_ref[pl.ds(r0[c], CHUNK), :] for c in chunks}
        gr4 = {c: grow_ref[cidx[c]] for c in chunks}
        kq = {}
        for c in chunks:
            k16 = k[c].astype(BF16)
            kq[c] = _mm_nt(jnp.concatenate([k16, q[c].astype(BF16)], axis=0), k16)
        beta, gcl, decay, a = {}, {}, {}, []
        for c, d in chains:
            beta[c, d] = gc4[c][:, d:d + 1]
            gcl[c, d] = gc4[c][:, n_beta + d:n_beta + d + 1]
            grw = gr4[c][n_beta + d:n_beta + d + 1, :]
            decay[c, d] = jnp.exp(jnp.where(incl[d], gcl[c, d] - grw, -jnp.inf))
            a.append(jnp.where(strict[d], beta[c, d] * kq[c][:CHUNK] * decay[c, d], 0.0))
        t = dict(zip(chains, _unit_tri_inverse(a, inv_consts)))
        eg = {cd: jnp.exp(gcl[cd]) for cd in chains}
        sol = {(c, d): _mm(t[c, d], jnp.concatenate([v[c] * beta[c, d], k[c] * (beta[c, d] * eg[c, d])], axis=1))
               for c, d in chains}
        aw = {(c, d): _mm(kq[c][CHUNK:] * decay[c, d], sol[c, d]) for c, d in chains}
        kuw = {}
        for c, d in chains:
            g_last = gcl[c, d][last_row[d]:last_row[d] + 1, :]
            kuw[c, d] = _mm_tn(k[c] * jnp.exp(g_last - gcl[c, d]), sol[c, d])
        for c, d in chains:
            pq_s[d, cidx[c]] = jnp.concatenate(
                [kuw[c, d][:, dk:], q[c] * eg[c, d] - aw[c, d][:, dk:]], axis=0).astype(BF16)
            n_s[d, cidx[c]] = kuw[c, d][:, :dk]
            o_s[d][pl.ds(r0[c], CHUNK), :] = aw[c, d][:, :dk]

    def prep_body(i, carry):
        prep_group(i * unroll, range(unroll))
        return carry

    lax.fori_loop(0, n_chunks // unroll, prep_body, 0)

    def scan_body(n, carry):
        for d in range(N_DIRS):
            c = n if d == 0 else n_chunks - 1 - n
            r0 = pl.multiple_of(c * CHUNK, CHUNK)
            g_last = grow_ref[c][n_beta + d:n_beta + d + 1, last_row[d]:last_row[d] + 1]
            s = st_s[d]
            r = _dot(pq_s[d, c], s.astype(BF16))
            st_s[d] = s * jnp.exp(g_last) + n_s[d, c] - r[:dk]
            o_s[d][pl.ds(r0, CHUNK), :] += r[dk:]
        return carry

    lax.fori_loop(0, n_chunks, scan_body, 0)

    o = _rms(of_s[...] + ob_s[...]) * og_ref[...]
    y_ref[...] = (o * _silu(z_ref[...])).astype(y_ref.dtype)
    if write_state:
        sfin_ref[...] = st_s[...]


def _delta(qkvz, gcol, grow, sc_w, sc_b, o_g, s0, layer, seq_len, n_batch, row_off, dk, write_state):
    nh = DN_HEADS
    blk0 = row_off // seq_len
    n_chunks = seq_len // CHUNK
    zero_init = s0 is None
    kern = functools.partial(_delta_kernel, seq_len=seq_len, dk=dk, zero_init=zero_init,
                             write_state=write_state)

    def col(group):
        return pl.BlockSpec((seq_len, dk), lambda b, h: (blk0 + b, group * nh + h))

    def wspec(group, rows):
        return pl.BlockSpec((rows, dk), lambda b, h: (0, group * nh + h))

    width = sc_w.shape[0]
    in_specs = [col(0), col(1), col(2), col(3),
                pl.BlockSpec((None, seq_len, 2 * N_DIRS), lambda b, h: (h, blk0 + b, 0)),
                pl.BlockSpec((None, n_chunks, 2 * N_DIRS, CHUNK), lambda b, h: (h, blk0 + b, 0, 0)),
                wspec(0, width), wspec(1, width), wspec(2, width),
                wspec(0, 1), wspec(1, 1), wspec(2, 1),
                pl.BlockSpec((1, dk), lambda b, h: (0, 0))]
    args = [qkvz, qkvz, qkvz, qkvz, gcol, grow, sc_w, sc_w, sc_w, sc_b, sc_b, sc_b, o_g]
    if not zero_init:
        in_specs.append(pl.BlockSpec((None, None, N_DIRS, None, dk, dk),
                                     lambda b, h: (b, layer, 0, h, 0, 0)))
        args.append(s0)
    y_shape = jax.ShapeDtypeStruct((n_batch * seq_len, nh * dk), BF16)
    y_spec = pl.BlockSpec((seq_len, dk), lambda b, h: (b, h))
    if write_state:
        out_shape = (y_shape, jax.ShapeDtypeStruct((n_batch, N_DIRS, nh, dk, dk), F32))
        out_specs = (y_spec, pl.BlockSpec((None, N_DIRS, None, dk, dk), lambda b, h: (b, 0, h, 0, 0)))
    else:
        out_shape = y_shape
        out_specs = y_spec
    return pl.pallas_call(
        kern,
        out_shape=out_shape,
        grid=(n_batch, nh),
        in_specs=in_specs,
        out_specs=out_specs,
        scratch_shapes=[pltpu.VMEM((seq_len, dk), F32)] * 5 + [
            pltpu.VMEM((N_DIRS, dk, dk), F32),
            pltpu.VMEM((N_DIRS, n_chunks, dk + CHUNK, dk), BF16),
            pltpu.VMEM((N_DIRS, n_chunks, dk, dk), F32)],
        compiler_params=_cparams(("parallel", "parallel")),
        name="delta",
    )(*args)


def _conformer_kernel(x_ref, w_ref, b_ref, lg_ref, lb_ref, o_ref, pad_s, *, n_seg, seg_blocks, halo):
    c = o_ref.shape[-1]
    width = w_ref.shape[0]
    pad = (width - 1) // 2
    seg_len = seg_blocks * GRID_W
    pad_s[:, 0:halo, :] = jnp.zeros((n_seg, halo, c), F32)
    pad_s[:, halo + seg_len:2 * halo + seg_len, :] = jnp.zeros((n_seg, halo, c), F32)
    for j in range(seg_blocks):
        if seg_blocks == 1:
            x = x_ref[...]
            pad_s[:, halo:halo + GRID_W, :] = x[..., :c] * _sigmoid(x[..., c:])
        else:
            x = x_ref[j]
            pad_s[0, halo + j * GRID_W:halo + (j + 1) * GRID_W, :] = x[:, :c] * _sigmoid(x[:, c:])

    win = GRID_W + 2 * halo

    def seg_body(r, carry):
        for j in range(seg_blocks):
            x = pad_s[r, j * GRID_W:j * GRID_W + win, :]
            shifted = [x] + [pltpu.roll(x, win - b, axis=0) for b in range(1, SUBLANE)]
            acc = jnp.zeros((GRID_W, c), F32) + b_ref[...]
            for s in range(width):
                o = halo - pad + s
                a0 = (o // SUBLANE) * SUBLANE
                acc = acc + shifted[o % SUBLANE][a0:a0 + GRID_W, :] * w_ref[s:s + 1, :]
            mu = jnp.mean(acc, axis=-1, keepdims=True)
            xc = acc - mu
            var = jnp.mean(xc * xc, axis=-1, keepdims=True)
            y = xc * lax.rsqrt(var + EPS) * lg_ref[...] + lb_ref[...]
            o_ref[r * seg_blocks + j] = _silu(y).astype(o_ref.dtype)
        return carry

    lax.fori_loop(0, n_seg, seg_body, 0)


def _conformer(cv, dw_w, dw_b, ln_g, ln_b, seq_len, n_batch, row_off, latent):
    t, c2 = cv.shape
    c = c2 // 2
    cv3 = cv.reshape(t // GRID_W, GRID_W, c2)
    blocks = seq_len // GRID_W
    n_seg, seg_blocks = (blocks, 1) if latent else (1, blocks)
    halo = 2 * SUBLANE
    assert (dw_w.shape[0] - 1) // 2 <= halo
    blk0 = row_off // seq_len
    kern = functools.partial(_conformer_kernel, n_seg=n_seg, seg_blocks=seg_blocks, halo=halo)
    const = lambda b: (0, 0)
    out = pl.pallas_call(
        kern,
        out_shape=jax.ShapeDtypeStruct((n_batch * blocks, GRID_W, c), BF16),
        grid=(n_batch,),
        in_specs=[pl.BlockSpec((blocks, GRID_W, c2), lambda b: (blk0 + b, 0, 0)),
                  pl.BlockSpec(dw_w.shape, const),
                  pl.BlockSpec((1, c), const), pl.BlockSpec((1, c), const), pl.BlockSpec((1, c), const)],
        out_specs=pl.BlockSpec((blocks, GRID_W, c), lambda b: (b, 0, 0)),
        scratch_shapes=[pltpu.VMEM((n_seg, seg_blocks * GRID_W + 2 * halo, c), F32)],
        compiler_params=_cparams(("parallel",)),
        name="conformer",
    )(cv3, dw_w, dw_b, ln_g, ln_b)
    return out.reshape(n_batch * seq_len, c)


def _outproj_kernel(*refs, n_x, d, n_ctx_tiles):
    yfc, yfl, ydc, ydl, ycc, ycl, w_ref, mod_ref, o_ref = refs[n_x:]
    is_ctx = pl.program_id(0) < n_ctx_tiles
    y = None
    k0 = 0
    for a_c, a_l in ((yfc, yfl), (ydc, ydl), (ycc, ycl)):
        kw = a_c.shape[1]
        a = jnp.where(is_ctx, a_c[...], a_l[...])
        part = _dot(a, w_ref[k0:k0 + kw, :])
        y = part if y is None else y + part
        k0 += kw
    o_ref[...] = _read_stream(refs[:n_x], n_ctx_tiles) + mod_ref[:, 2 * d:3 * d] * y


def _outproj(parts, w_out, x, mod, mod_row, tm, n_ctx_tiles):
    in_specs, args = _stream_specs(x, tm, n_ctx_tiles)
    t = sum(a.shape[0] for a in args)
    d = w_out.shape[1]
    kern = functools.partial(_outproj_kernel, n_x=len(args), d=d, n_ctx_tiles=n_ctx_tiles)
    for a_c, a_l in parts:
        kw = a_c.shape[1]
        in_specs.append(pl.BlockSpec((tm, kw), lambda i: (jnp.minimum(i, n_ctx_tiles - 1), 0)))
        in_specs.append(pl.BlockSpec((tm, kw), lambda i: (jnp.maximum(i - n_ctx_tiles, 0), 0)))
        args += [a_c, a_l]
    in_specs += [pl.BlockSpec(w_out.shape, lambda i: (0, 0)),
                 pl.BlockSpec((None, 1, N_MOD * d), lambda i: (mod_row(i), 0, 0))]
    args += [w_out, mod]
    return pl.pallas_call(
        kern,
        out_shape=jax.ShapeDtypeStruct((t, d), F32),
        grid=(t // tm,),
        in_specs=in_specs,
        out_specs=pl.BlockSpec((tm, d), lambda i: (i, 0)),
        compiler_params=_cparams(("parallel",)),
        name="outproj",
    )(*args)


def _ffn_kernel(*refs, d, final_norm):
    x_ref, g_ref, mod_ref, wg_ref, wu_ref, wd_ref = refs[:6]
    fg_ref = refs[6] if final_norm else None
    o_ref, h_s, acc_s = refs[-3:]
    f = pl.program_id(1)

    @pl.when(f == 0)
    def _():
        mod = mod_ref[...]
        h = _rms(x_ref[...]) * g_ref[...]
        h = h * (1.0 + mod[:, 4 * d:5 * d]) + mod[:, 3 * d:4 * d]
        h_s[...] = h.astype(BF16)
        acc_s[...] = jnp.zeros_like(acc_s)

    h = h_s[...]
    a = _silu(_dot(h, wg_ref[...].astype(BF16))) * _dot(h, wu_ref[...].astype(BF16))
    acc_s[...] += _dot(a.astype(BF16), wd_ref[...].astype(BF16))

    @pl.when(f == pl.num_programs(1) - 1)
    def _():
        o = x_ref[...] + mod_ref[:, 5 * d:6 * d] * acc_s[...]
        if final_norm:
            o = _rms(o) * fg_ref[...]
        o_ref[...] = o


def _ffn(x, gain, mod, mod_row, wg, wu, wd, final_gain, tm, tf):
    t, d = x.shape
    ff = wg.shape[1]
    final_norm = final_gain is not None
    kern = functools.partial(_ffn_kernel, d=d, final_norm=final_norm)
    row = lambda i, f: (i, 0)
    const = lambda i, f: (0, 0)
    in_specs = [pl.BlockSpec((tm, d), row),
                pl.BlockSpec((1, d), const),
                pl.BlockSpec((None, 1, N_MOD * d), lambda i, f: (mod_row(i), 0, 0)),
                pl.BlockSpec((d, tf), lambda i, f: (0, f)),
                pl.BlockSpec((d, tf), lambda i, f: (0, f)),
                pl.BlockSpec((tf, d), lambda i, f: (f, 0))]
    args = [x, gain, mod, wg, wu, wd]
    if final_norm:
        in_specs.append(pl.BlockSpec((1, d), const))
        args.append(final_gain)
    return pl.pallas_call(
        kern,
        out_shape=jax.ShapeDtypeStruct((t, d), F32),
        grid=(t // tm, ff // tf),
        in_specs=in_specs,
        out_specs=pl.BlockSpec((tm, d), row),
        scratch_shapes=[pltpu.VMEM((tm, d), BF16), pltpu.VMEM((tm, d), F32)],
        compiler_params=_cparams(("parallel", "arbitrary")),
        name="ffn",
    )(*args)


INFO_E, INFO_W, INFO_R = 0, 2, 4
MOE_TILE = 1024
MOE_SUB = 256


def _route_kernel(x_ref, g_ref, mod_ref, router_ref, h_o, info_o, cnt_o, cnt_s, *, d, n_exp):
    i = pl.program_id(0)

    @pl.when(i == 0)
    def _():
        cnt_s[...] = jnp.zeros_like(cnt_s)

    mod = mod_ref[...]
    h = _rms(x_ref[...]) * g_ref[...]
    h = h * (1.0 + mod[:, 4 * d:5 * d]) + mod[:, 3 * d:4 * d]
    h_o[...] = h

    logits = _mm3(h, router_ref[...])
    tm = logits.shape[0]
    lane = lax.broadcasted_iota(jnp.int32, logits.shape, 1)
    lg = jnp.where(lane < n_exp, logits, -jnp.inf)
    m1 = jnp.max(lg, axis=-1, keepdims=True)
    i1 = jnp.min(jnp.where(lg == m1, lane, LANE), axis=-1, keepdims=True)
    lg2 = jnp.where(lane == i1, -jnp.inf, lg)
    m2 = jnp.max(lg2, axis=-1, keepdims=True)
    i2 = jnp.min(jnp.where(lg2 == m2, lane, LANE), axis=-1, keepdims=True)
    w1 = 1.0 / (1.0 + jnp.exp(m2 - m1))

    sel1 = lane == i1
    sel2 = lane == i2
    member = jnp.where(sel1 | sel2, 1.0, 0.0)
    ri = lax.broadcasted_iota(jnp.int32, (tm, tm), 0)
    ci = lax.broadcasted_iota(jnp.int32, (tm, tm), 1)
    before = jnp.where(ri > ci, 1.0, 0.0)
    rank = cnt_s[...] + _mm(before, member)
    r1 = jnp.sum(jnp.where(sel1, rank, 0.0), axis=-1, keepdims=True)
    r2 = jnp.sum(jnp.where(sel2, rank, 0.0), axis=-1, keepdims=True)
    cnt_s[...] += jnp.sum(member, axis=0, keepdims=True)
    cnt_o[...] = jnp.broadcast_to(cnt_s[...], cnt_o.shape)

    info = jnp.zeros(logits.shape, F32)
    for ln, val in ((INFO_E, i1.astype(F32)), (INFO_E + 1, i2.astype(F32)), (INFO_W, w1),
                    (INFO_W + 1, 1.0 - w1), (INFO_R, r1), (INFO_R + 1, r2)):
        info = jnp.where(lane == ln, val, info)
    info_o[...] = info


def _route(x, gain, mod, mod_row, router, tm):
    t, d = x.shape
    n_exp = router.shape[1]
    router_p = jnp.zeros((d, LANE), F32).at[:, :n_exp].set(router)
    kern = functools.partial(_route_kernel, d=d, n_exp=n_exp)
    return pl.pallas_call(
        kern,
        out_shape=(jax.ShapeDtypeStruct((t, d), F32), jax.ShapeDtypeStruct((t, LANE), F32),
                   jax.ShapeDtypeStruct((SUBLANE, LANE), F32)),
        grid=(t // tm,),
        in_specs=[pl.BlockSpec((tm, d), lambda i: (i, 0)),
                  pl.BlockSpec((1, d), lambda i: (0, 0)),
                  pl.BlockSpec((None, 1, N_MOD * d), lambda i: (mod_row(i), 0, 0)),
                  pl.BlockSpec((d, LANE), lambda i: (0, 0))],
        out_specs=(pl.BlockSpec((tm, d), lambda i: (i, 0)), pl.BlockSpec((tm, LANE), lambda i: (i, 0)),
                   pl.BlockSpec((SUBLANE, LANE), lambda i: (0, 0))),
        scratch_shapes=[pltpu.VMEM((1, LANE), F32)],
        compiler_params=_cparams(("arbitrary",)),
        name="moe_route",
    )(x, gain, mod, router_p)


def _row_copy(src_hbm, row, dst, slot, sem):
    return pltpu.make_async_copy(src_hbm.at[pl.ds(row, 1), :], dst.at[pl.ds(slot, 1), :], sem)


def _experts_kernel(expert_sm, nvalid_sm, tok_ref, tok_next_ref, h_hbm, wg_ref, wu_ref, wd_ref, y_ref,
                    hbuf, h16, acc, wg16, wu16, wd16, sems, *, sub, n_f):
    del expert_sm
    i = pl.program_id(0)
    j = pl.program_id(1)
    n_i = pl.num_programs(0)
    tm = h16.shape[0]
    slot = i % 2
    nxt = 1 - slot
    nv = nvalid_sm[i]
    n_sub = (nv + sub - 1) // sub
    has_next = jnp.where(i + 1 < n_i, nvalid_sm[jnp.minimum(i + 1, n_i - 1)], 0) > 0
    per_step = -(-tm // n_f)
    base = j * per_step

    def start_row(tok, r, buf):
        _row_copy(h_hbm, tok[0, r], hbuf.at[buf], r, sems.at[buf]).start()

    def fetch_loop(tok, lo, hi, buf):
        def issue(r, carry):
            start_row(tok, r, buf)
            return carry

        lax.fori_loop(lo, hi, issue, 0)

    @pl.when(j == 0)
    def _():
        @pl.when((i == 0) & (nv > 0))
        def _():
            fetch_loop(tok_ref, 0, tm, 0)

        acc[...] = jnp.zeros_like(acc)

        @pl.when(nv > 0)
        def _():
            pltpu.make_async_copy(h_hbm.at[pl.ds(0, tm), :], hbuf.at[slot], sems.at[slot]).wait()

            def cast(s, carry):
                rows = pl.ds(pl.multiple_of(s * sub, sub), sub)
                h16[rows, :] = hbuf[slot, rows, :].astype(BF16)
                return carry

            lax.fori_loop(0, n_sub, cast, 0)

    fast = (n_sub == tm // sub) & has_next

    @pl.when(fast)
    def _():
        n_safe = tm - (n_f - 1) * per_step
        for r in range(min(per_step, n_safe)):
            start_row(tok_next_ref, base + r, nxt)
        h = h16[...]
        a = _silu(_dot(h, wg_ref[...].astype(BF16))) * _dot(h, wu_ref[...].astype(BF16))
        acc[...] += _dot(a.astype(BF16), wd_ref[...].astype(BF16))
        for r in range(n_safe, per_step):
            pl.when(base + r < tm)(functools.partial(start_row, tok_next_ref, base + r, nxt))

    @pl.when(jnp.logical_not(fast))
    def _():
        @pl.when(has_next)
        def _():
            fetch_loop(tok_next_ref, base, jnp.minimum(base + per_step, tm), nxt)

        @pl.when(n_sub > 0)
        def _():
            wg16[...] = wg_ref[...].astype(BF16)
            wu16[...] = wu_ref[...].astype(BF16)
            wd16[...] = wd_ref[...].astype(BF16)

            def block(s, carry):
                rows = pl.ds(pl.multiple_of(s * sub, sub), sub)
                h = h16[rows, :]
                a = _silu(_dot(h, wg16[...])) * _dot(h, wu16[...])
                acc[rows, :] += _dot(a.astype(BF16), wd16[...])
                return carry

            lax.fori_loop(0, n_sub, block, 0)

    @pl.when(j == n_f - 1)
    def _():
        y_ref[...] = acc[...]


def _experts(h, src_tok, tile_expert, tile_nvalid, wg, wu, wd, tm, tf, sub):
    t, d = h.shape
    n_exp, _, ff = wg.shape
    n_tiles = tile_expert.shape[0]
    n_f = ff // tf

    def w_idx(i, j, expert_sm, nvalid_sm):
        return expert_sm[i], jnp.where(nvalid_sm[i] > 0, j, n_f - 1)

    def wgu_map(i, j, expert_sm, nvalid_sm):
        e, jj = w_idx(i, j, expert_sm, nvalid_sm)
        return e, 0, jj

    def wd_map(i, j, expert_sm, nvalid_sm):
        e, jj = w_idx(i, j, expert_sm, nvalid_sm)
        return e, jj, 0

    kern = functools.partial(_experts_kernel, sub=sub, n_f=n_f)
    tok = src_tok.reshape(n_tiles, 1, tm)
    grid_spec = pltpu.PrefetchScalarGridSpec(
        num_scalar_prefetch=2,
        grid=(n_tiles, n_f),
        in_specs=[pl.BlockSpec((None, 1, tm), lambda i, j, *_: (i, 0, 0), memory_space=pltpu.SMEM),
                  pl.BlockSpec((None, 1, tm), lambda i, j, *_: (jnp.minimum(i + 1, n_tiles - 1), 0, 0),
                               memory_space=pltpu.SMEM),
                  pl.BlockSpec(memory_space=pl.ANY),
                  pl.BlockSpec((None, d, tf), wgu_map),
                  pl.BlockSpec((None, d, tf), wgu_map),
                  pl.BlockSpec((None, tf, d), wd_map)],
        out_specs=pl.BlockSpec((tm, d), lambda i, j, *_: (i, 0)),
        scratch_shapes=[pltpu.VMEM((2, tm, d), F32), pltpu.VMEM((tm, d), BF16), pltpu.VMEM((tm, d), F32),
                        pltpu.VMEM((d, tf), BF16), pltpu.VMEM((d, tf), BF16), pltpu.VMEM((tf, d), BF16),
                        pltpu.SemaphoreType.DMA((2,))])
    return pl.pallas_call(
        kern,
        out_shape=jax.ShapeDtypeStruct((n_tiles * tm, d), F32),
        grid_spec=grid_spec,
        compiler_params=_cparams(("arbitrary", "arbitrary")),
        name="moe_experts",
    )(tile_expert, tile_nvalid, tok, tok, h, wg, wu, wd)


def _combine_kernel(*refs, d, final_norm, split_tiles):
    pos_ref, pos_next_ref, info_ref, x_ref, mod_ref = refs[:5]
    p = 5
    if final_norm:
        fg_ref = refs[p]
        p += 1
    y_hbm = refs[p]
    o_refs = refs[p + 1:-2]
    ybuf, sems = refs[-2:]
    tm = x_ref.shape[0]
    i = pl.program_id(0)
    slot = i % 2

    def fetch(pos, buf):
        def issue(r, carry):
            for k in range(TOP_K):
                _row_copy(y_hbm, pos[0, k * tm + r], ybuf.at[buf, k], r, sems.at[buf]).start()
            return carry

        lax.fori_loop(0, tm, issue, 0, unroll=8)

    @pl.when(i == 0)
    def _():
        fetch(pos_ref, 0)

    @pl.when(i + 1 < pl.num_programs(0))
    def _():
        fetch(pos_next_ref, 1 - slot)

    for k in range(TOP_K):
        pltpu.make_async_copy(y_hbm.at[pl.ds(0, tm), :], ybuf.at[slot, k], sems.at[slot]).wait()

    info = info_ref[...]
    f = None
    for k in range(TOP_K):
        term = info[:, INFO_W + k:INFO_W + k + 1] * ybuf[slot, k]
        f = term if f is None else f + term
    o = x_ref[...] + mod_ref[:, 5 * d:6 * d] * f
    if final_norm:
        o = _rms(o) * fg_ref[...]
    if split_tiles is None:
        o_refs[0][...] = o
    else:
        @pl.when(i < split_tiles)
        def _():
            o_refs[0][...] = o

        @pl.when(i >= split_tiles)
        def _():
            o_refs[1][...] = o


def _combine(x, info, pos, y, mod, mod_row, final_gain, tm, split_rows):
    t, d = x.shape
    final_norm = final_gain is not None
    split_tiles = None if split_rows is None else split_rows // tm
    kern = functools.partial(_combine_kernel, d=d, final_norm=final_norm, split_tiles=split_tiles)
    n_tiles = t // tm
    if split_tiles is None:
        out_shape = jax.ShapeDtypeStruct((t, d), F32)
        out_specs = pl.BlockSpec((tm, d), lambda i: (i, 0))
    else:
        out_shape = (jax.ShapeDtypeStruct((split_rows, d), F32), jax.ShapeDtypeStruct((t - split_rows, d), F32))
        out_specs = (pl.BlockSpec((tm, d), lambda i: (jnp.minimum(i, split_tiles - 1), 0)),
                     pl.BlockSpec((tm, d), lambda i: (jnp.maximum(i - split_tiles, 0), 0)))
    in_specs = [pl.BlockSpec((None, 1, TOP_K * tm), lambda i: (i, 0, 0), memory_space=pltpu.SMEM),
                pl.BlockSpec((None, 1, TOP_K * tm), lambda i: (jnp.minimum(i + 1, n_tiles - 1), 0, 0),
                             memory_space=pltpu.SMEM),
                pl.BlockSpec((tm, LANE), lambda i: (i, 0)),
                pl.BlockSpec((tm, d), lambda i: (i, 0)),
                pl.BlockSpec((None, 1, N_MOD * d), lambda i: (mod_row(i), 0, 0))]
    args = [pos, pos, info, x, mod]
    if final_norm:
        in_specs.append(pl.BlockSpec((1, d), lambda i: (0, 0)))
        args.append(final_gain)
    in_specs.append(pl.BlockSpec(memory_space=pl.ANY))
    args.append(y)
    return pl.pallas_call(
        kern,
        out_shape=out_shape,
        grid=(t // tm,),
        in_specs=in_specs,
        out_specs=out_specs,
        scratch_shapes=[pltpu.VMEM((2, TOP_K, tm, d), F32), pltpu.SemaphoreType.DMA((2,))],
        compiler_params=_cparams(("arbitrary",)),
        name="moe_combine",
    )(*args)


def _moe(x, gain, mod, mod_row_fn, router, wg, wu, wd, final_gain, split_rows, tm_route, tm_exp, tm_comb, tf, sub):
    t, d = x.shape
    n_exp = wg.shape[0]
    h, info, cnt = _route(x, gain, mod, mod_row_fn(tm_route), router, tm_route)

    counts = cnt[0, :n_exp].astype(jnp.int32)
    tiles_per = (counts + tm_exp - 1) // tm_exp
    tile_end = jnp.cumsum(tiles_per)
    tile_start = tile_end - tiles_per
    n_tiles = (TOP_K * t) // tm_exp + n_exp
    tile_id = jnp.arange(n_tiles, dtype=jnp.int32)
    tile_expert = jnp.minimum(jnp.sum(tile_id[:, None] >= tile_end[None, :], axis=1), n_exp - 1).astype(jnp.int32)
    tile_nvalid = jnp.where(tile_id < tile_end[n_exp - 1],
                            jnp.clip(counts[tile_expert] - (tile_id - tile_start[tile_expert]) * tm_exp, 0, tm_exp),
                            0).astype(jnp.int32)
    slot_start = tile_start * tm_exp
    experts = info[:, INFO_E:INFO_E + TOP_K].astype(jnp.int32)
    slots = slot_start[experts] + info[:, INFO_R:INFO_R + TOP_K].astype(jnp.int32)
    token = jnp.broadcast_to(jnp.arange(t, dtype=jnp.int32)[:, None], slots.shape)
    src_tok = jnp.zeros((n_tiles * tm_exp,), jnp.int32).at[slots.reshape(-1)].set(
        token.reshape(-1), unique_indices=True)

    y = _experts(h, src_tok, tile_expert, tile_nvalid, wg, wu, wd, tm_exp, tf, sub)
    pos = slots.reshape(t // tm_comb, tm_comb, TOP_K).transpose(0, 2, 1).reshape(t // tm_comb, 1, TOP_K * tm_comb)
    return _combine(x, info, pos, y, mod, mod_row_fn(tm_comb), final_gain, tm_comb, split_rows)


def _lane_row(vals, offset):
    n = vals.shape[0]
    return jnp.zeros((1, LANE), F32).at[0, offset:offset + n].set(vals.astype(F32))


def kernel(x_prompt, x_sample, state_delta, c, c_ctx, norm1, norm2, w_mod, b_mod, w_in, sc_w, sc_b, dn_a_log, dn_dt_bias, dn_norm, cv_dw_w, cv_dw_b, cv_ln_g, cv_ln_b, w_out, ffn_wg, ffn_wu, ffn_wd, moe_router, moe_wg, moe_wu, moe_wd, final_norm):
    b_ctx, l_ctx, d = x_prompt.shape
    b_lat, l_lat, _ = x_sample.shape
    depth = w_in.shape[0]
    t_ctx = b_ctx * l_ctx
    t_lat = b_lat * l_lat
    f_w = d // 4
    f_gw = f_w // F_GROUPS
    dn_w = d // 2
    dk = dn_w // DN_HEADS
    cv_w = d // 4
    qkvz_w = 4 * dn_w
    n_gate = 2 * N_DIRS * DN_HEADS
    n_exp = moe_wg.shape[1]
    assert dk == LANE and t_ctx % l_lat == 0 and b_lat + 1 <= MOD_ROWS
    assert l_ctx % CHUNK == 0 and l_lat % CHUNK == 0 and l_lat % GRID_W == 0 and CHUNK == GRID_W

    tm = _pick_tile(512, t_ctx, l_lat)
    tm_ffn = _pick_tile(1024, t_ctx, l_lat)

    def mod_row_fn(tile):
        n_ctx_tiles = t_ctx // tile
        per_seq = l_lat // tile
        return lambda i: jnp.where(i < n_ctx_tiles, 0, 1 + (i - n_ctx_tiles) // per_seq)

    x = (x_prompt.reshape(t_ctx, d), x_sample.reshape(t_lat, d))
    cvec = jnp.zeros((MOD_ROWS, d), F32).at[0].set(c_ctx).at[1:1 + b_lat].set(c)
    mods = _adaln(cvec, w_mod, b_mod).reshape(depth, MOD_ROWS, 1, N_MOD * d)

    cg, sg = _dft_tables(f_gw)
    eye_g = np.eye(F_GROUPS)
    ccs = jnp.asarray(np.concatenate([np.kron(eye_g, cg), np.kron(eye_g, sg)], axis=1), F32).astype(BF16)

    passes = ((l_ctx, b_ctx, 0, False), (l_lat, b_lat, t_ctx, True))
    o_q = f_w
    o_ba = f_w + qkvz_w
    o_cv = o_ba + n_gate
    new_states = []
    for l in range(depth):
        w = w_in[l]
        w_perm = jnp.concatenate(
            [w[:, :o_ba], w[:, o_cv:o_cv + 2 * cv_w], w[:, o_ba:o_cv], jnp.zeros((d, LANE - n_gate), F32)],
            axis=1).astype(BF16)
        alog = _lane_row(dn_a_log[l].reshape(-1), N_DIRS * DN_HEADS)
        dtb = _lane_row(dn_dt_bias[l].reshape(-1), N_DIRS * DN_HEADS)
        xc, xs, qkvz, cv, gates = _proj(x, norm1[l][None], mods[l], w_perm, ccs, alog, dtb,
                                        mod_row_fn(tm), tm, t_ctx // tm, f_w, qkvz_w, 2 * cv_w)
        g16 = gates[:, :n_gate].reshape(-1, 2 * N_DIRS, DN_HEADS)
        gcol = g16.transpose(2, 0, 1)
        grow = g16.reshape(-1, CHUNK, 2 * N_DIRS, DN_HEADS).transpose(3, 0, 2, 1)

        parts = [[], [], []]
        for seq_len, n_batch, row_off, latent in passes:
            parts[0].append(_fourier(xc, xs, seq_len, n_batch, row_off, f_gw))
            res = _delta(qkvz, gcol, grow, sc_w[l], sc_b[l][None], dn_norm[l][None],
                         state_delta if latent else None, l, seq_len, n_batch, row_off, dk,
                         write_state=not latent)
            if latent:
                parts[1].append(res)
            else:
                parts[1].append(res[0])
                new_states.append(res[1])
            parts[2].append(_conformer(cv, cv_dw_w[l], cv_dw_b[l][None], cv_ln_g[l][None],
                                       cv_ln_b[l][None], seq_len, n_batch, row_off, latent))
        x = _outproj(parts, w_out[l].astype(BF16), x, mods[l], mod_row_fn(tm), tm, t_ctx // tm)

        fg = final_norm[None] if l == depth - 1 else None
        if l % 2 == 0:
            i = l // 2
            x = _ffn(x, norm2[l][None], mods[l], mod_row_fn(tm_ffn), ffn_wg[i], ffn_wu[i], ffn_wd[i], fg,
                     tm_ffn, _pick_tile(512, ffn_wg.shape[-1]))
        else:
            i = l // 2
            x = _moe(x, norm2[l][None], mods[l], mod_row_fn, moe_router[i], moe_wg[i], moe_wu[i], moe_wd[i], fg,
                     split_rows=t_ctx if l == depth - 1 else None,
                     tm_route=tm, tm_exp=MOE_TILE, tm_comb=_pick_tile(256, t_ctx, l_lat),
                     tf=_pick_tile(512, moe_wg.shape[-1]), sub=MOE_SUB)

    y_ctx, y_lat = x if isinstance(x, tuple) else (x[:t_ctx], x[t_ctx:])
    y_prompt = y_ctx.reshape(b_ctx, l_ctx, d)
    y_sample = y_lat.reshape(b_lat, l_lat, d)
    return y_prompt, y_sample, jnp.stack(new_states, axis=1)
```

```python
import functools
import math

import numpy as np
import jax
import jax.numpy as jnp
from jax import lax
from jax.experimental import pallas as pl
from jax.experimental.pallas import tpu as pltpu

F32 = jnp.float32
BF16 = jnp.bfloat16
EPS = 1e-6

LANE = 128
SUBLANE = 8
VMEM_LIMIT = 56 * 1024 * 1024

CHUNK = 64
GRID_W = 64
F_GROUPS = 4
DN_HEADS = 4
N_DIRS = 2
DN_HEADS_PER_STEP = 2
PREP_UNROLL = 16
TOP_K = 2
N_MOD = 6
MOD_ROWS = 16


def _cparams(sem):
    return pltpu.CompilerParams(dimension_semantics=sem, vmem_limit_bytes=VMEM_LIMIT)


def _pick_tile(limit, *sizes):
    t = limit
    while any(s % t for s in sizes):
        t //= 2
    return t


def _dot(a, b):
    return jnp.dot(a, b, preferred_element_type=F32)


def _mm(a, b):
    return jnp.dot(a.astype(BF16), b.astype(BF16), preferred_element_type=F32)


def _mm_nt(a, b):
    return lax.dot_general(a.astype(BF16), b.astype(BF16), (((1,), (1,)), ((), ())),
                           preferred_element_type=F32)


def _mm_tn(a, b):
    return lax.dot_general(a.astype(BF16), b.astype(BF16), (((0,), (0,)), ((), ())),
                           preferred_element_type=F32)


def _split(a):
    hi = a.astype(BF16)
    lo = (a - hi.astype(F32)).astype(BF16)
    return hi, lo


def _mm3(a, b):
    ah, al = _split(a)
    bh, bl = _split(b)
    return _dot(ah, bh) + (_dot(ah, bl) + _dot(al, bh))


def _sigmoid(x):
    return 1.0 / (1.0 + jnp.exp(-x))


def _silu(x):
    return x * _sigmoid(x)


def _rms(x):
    return x * lax.rsqrt(jnp.mean(x * x, axis=-1, keepdims=True) + EPS)


def _stream_specs(x, tm, n_ctx_tiles):
    if isinstance(x, tuple):
        d = x[0].shape[1]
        return ([pl.BlockSpec((tm, d), lambda i, *_: (jnp.minimum(i, n_ctx_tiles - 1), 0)),
                 pl.BlockSpec((tm, d), lambda i, *_: (jnp.maximum(i - n_ctx_tiles, 0), 0))], list(x))
    return [pl.BlockSpec((tm, x.shape[1]), lambda i, *_: (i, 0))], [x]


def _read_stream(x_refs, n_ctx_tiles):
    if len(x_refs) == 2:
        return jnp.where(pl.program_id(0) < n_ctx_tiles, x_refs[0][...], x_refs[1][...])
    return x_refs[0][...]


def _adaln_kernel(c_ref, w_ref, b_ref, o_ref):
    o_ref[...] = _mm(_silu(c_ref[...]), w_ref[...]) + b_ref[...]


def _adaln(cvec, w_mod, b_mod):
    depth, d, n = w_mod.shape
    tn = _pick_tile(1024, n)
    return pl.pallas_call(
        _adaln_kernel,
        out_shape=jax.ShapeDtypeStruct((depth, MOD_ROWS, n), F32),
        grid=(depth, n // tn),
        in_specs=[pl.BlockSpec((MOD_ROWS, d), lambda l, j: (0, 0)),
                  pl.BlockSpec((None, d, tn), lambda l, j: (l, 0, j)),
                  pl.BlockSpec((None, 1, tn), lambda l, j: (l, 0, j))],
        out_specs=pl.BlockSpec((None, MOD_ROWS, tn), lambda l, j: (l, 0, j)),
        compiler_params=_cparams(("parallel", "parallel")),
        name="adaln",
    )(cvec, w_mod, b_mod.reshape(depth, 1, n))


def _proj_kernel(*refs, n_x, n_ctx_tiles, d, f_w, qkvz_w, cv_w):
    g_ref, mod_ref, w_ref, ccs_ref, alog_ref, dtb_ref, xc_o, xs_o, qkvz_o, cv_o, gate_o = refs[n_x:]
    x = _read_stream(refs[:n_x], n_ctx_tiles)
    mod = mod_ref[...]
    h = _rms(x) * g_ref[...]
    h = (h * (1.0 + mod[:, d:2 * d]) + mod[:, 0:d]).astype(BF16)

    xf = _dot(h, w_ref[:, 0:f_w])
    xcs = _dot(xf.astype(BF16), ccs_ref[...])
    xc_o[...] = xcs[:, :f_w].astype(xc_o.dtype)
    xs_o[...] = xcs[:, f_w:].astype(xs_o.dtype)
    o = f_w
    step = 4 * LANE
    for n0 in range(0, qkvz_w, step):
        qkvz_o[:, n0:n0 + step] = _dot(h, w_ref[:, o + n0:o + n0 + step])
    o += qkvz_w
    cv_o[...] = _dot(h, w_ref[:, o:o + cv_w])
    o += cv_w
    ba = _dot(h, w_ref[:, o:o + LANE])

    tm = ba.shape[0]
    lane = lax.broadcasted_iota(jnp.int32, ba.shape, 1)
    row = lax.broadcasted_iota(jnp.int32, ba.shape, 0) % CHUNK
    beta = _sigmoid(ba)
    t = ba + dtb_ref[...]
    softplus = jnp.maximum(t, 0.0) + jnp.log1p(jnp.exp(-jnp.abs(t)))
    g = -jnp.exp(alog_ref[...]) * softplus
    cf = g
    cb = g
    s = 1
    while s < CHUNK:
        cf = cf + jnp.where(row >= s, pltpu.roll(cf, s, axis=0), 0.0)
        cb = cb + jnp.where(row < CHUNK - s, pltpu.roll(cb, tm - s, axis=0), 0.0)
        s *= 2
    n_beta = N_DIRS * DN_HEADS
    gate_o[...] = jnp.where(lane < n_beta, beta, jnp.where(lane < n_beta + DN_HEADS, cf, cb))


def _proj(x, gain, mod, w_perm, ccs, alog, dtb, mod_row, tm, n_ctx_tiles, f_w, qkvz_w, cv_w):
    x_specs, x_args = _stream_specs(x, tm, n_ctx_tiles)
    t = sum(a.shape[0] for a in x_args)
    d, n = w_perm.shape
    kern = functools.partial(_proj_kernel, n_x=len(x_args), n_ctx_tiles=n_ctx_tiles, d=d, f_w=f_w,
                             qkvz_w=qkvz_w, cv_w=cv_w)
    row = lambda i: (i, 0)
    const = lambda i: (0, 0)
    return pl.pallas_call(
        kern,
        out_shape=(jax.ShapeDtypeStruct((t, f_w), BF16), jax.ShapeDtypeStruct((t, f_w), BF16),
                   jax.ShapeDtypeStruct((t, qkvz_w), F32), jax.ShapeDtypeStruct((t, cv_w), F32),
                   jax.ShapeDtypeStruct((t, LANE), F32)),
        grid=(t // tm,),
        in_specs=x_specs + [
            pl.BlockSpec((1, d), const),
            pl.BlockSpec((None, 1, N_MOD * d), lambda i: (mod_row(i), 0, 0)),
            pl.BlockSpec((d, n), const),
            pl.BlockSpec(ccs.shape, const),
            pl.BlockSpec((1, LANE), const),
            pl.BlockSpec((1, LANE), const)],
        out_specs=(pl.BlockSpec((tm, f_w), row), pl.BlockSpec((tm, f_w), row),
                   pl.BlockSpec((tm, qkvz_w), row), pl.BlockSpec((tm, cv_w), row),
                   pl.BlockSpec((tm, LANE), row)),
        compiler_params=_cparams(("parallel",)),
        name="proj",
    )(*x_args, gain, mod, w_perm, ccs, alog, dtb)


def _fourier_kernel(xc_ref, xs_ref, cn_ref, sn_ref, o_ref, *, scale):
    y = _dot(cn_ref[...], xc_ref[...]) - _dot(sn_ref[...], xs_ref[...])
    o_ref[...] = (y * scale).astype(o_ref.dtype)


def _dft_tables(n):
    j = np.arange(n, dtype=np.int64)
    ang = (2.0 * np.pi / n) * ((j[:, None] * j[None, :]) % n).astype(np.float64)
    return np.cos(ang), np.sin(ang)


def _fourier(xc, xs, seq_len, n_batch, row_off, f_gw):
    f_w = xc.shape[1]
    cn, sn = _dft_tables(seq_len)
    cn = jnp.asarray(cn, F32).astype(BF16)
    sn = jnp.asarray(sn, F32).astype(BF16)
    tr = _pick_tile(1024, seq_len)
    nt = seq_len // tr
    blk0 = row_off // seq_len
    kern = functools.partial(_fourier_kernel, scale=1.0 / math.sqrt(seq_len * f_gw))
    return pl.pallas_call(
        kern,
        out_shape=jax.ShapeDtypeStruct((n_batch * seq_len, f_w), BF16),
        grid=(nt, n_batch),
        in_specs=[pl.BlockSpec((seq_len, f_w), lambda i, b: (blk0 + b, 0)),
                  pl.BlockSpec((seq_len, f_w), lambda i, b: (blk0 + b, 0)),
                  pl.BlockSpec((tr, seq_len), lambda i, b: (i, 0)),
                  pl.BlockSpec((tr, seq_len), lambda i, b: (i, 0))],
        out_specs=pl.BlockSpec((tr, f_w), lambda i, b: (b * nt + i, 0)),
        compiler_params=_cparams(("parallel", "parallel")),
        name="fourier",
    )(xc, xs, cn, sn)


def _inverse_consts():
    ri = lax.broadcasted_iota(jnp.int32, (CHUNK, CHUNK), 0)
    ci = lax.broadcasted_iota(jnp.int32, (CHUNK, CHUNK), 1)
    eye = (ri == ci).astype(F32)
    same32 = (ri // 32) == (ci // 32)
    m16 = ((ri // 16) == (ci // 16)).astype(F32)
    return eye, m16, (same32.astype(F32) - m16, 1.0 - same32.astype(F32))


def _unit_tri_inverse(mats, consts):
    eye, m16, offs = consts
    ps = [-(a * m16) for a in mats]
    ts = [eye + p for p in ps]
    for _ in range(3):
        ps = [_mm(p, p) for p in ps]
        ts = [t + _mm(t, p) for t, p in zip(ts, ps)]
    for off in offs:
        us = [_mm(t, a * off) for t, a in zip(ts, mats)]
        ts = [t - _mm(u, t) for t, u in zip(ts, us)]
    return ts


def _delta_kernel(*refs, seq_len, dk, zero_init, write_state):
    (q_ref, k_ref, v_ref, z_ref, gcol_ref, grow_ref,
     wq_ref, wk_ref, wv_ref, bq_ref, bk_ref, bv_ref, og_ref) = refs[:13]
    pos = 13
    if not zero_init:
        s0_ref = refs[pos]
        pos += 1
    y_ref = refs[pos]
    pos += 1
    if write_state:
        sfin_ref = refs[pos]
        pos += 1
    qs, ks, vs, o_s, st_s, pq_s, n_s = refs[pos:]

    n_chunks = seq_len // CHUNK
    n_heads = qs.shape[1] // dk
    heads = range(n_heads)
    rows = lax.broadcasted_iota(jnp.int32, qs.shape, 0)

    def hcols(h):
        return slice(h * dk, (h + 1) * dk)

    def conv_silu(x_ref, w_ref, b_ref):
        x = x_ref[...]
        width = w_ref.shape[0]
        pad = (width - 1) // 2
        acc = jnp.zeros_like(x) + b_ref[...]
        for s in range(width):
            o = s - pad
            if o == 0:
                xs = x
            else:
                xs = pltpu.roll(x, (-o) % seq_len, axis=0)
                xs = jnp.where((rows + o >= 0) & (rows + o < seq_len), xs, 0.0)
            acc = acc + xs * w_ref[s:s + 1, :]
        return _silu(acc)

    def l2norm(x):
        return x * lax.rsqrt(jnp.sum(x * x, axis=-1, keepdims=True) + EPS)

    qc = conv_silu(q_ref, wq_ref, bq_ref)
    kc = conv_silu(k_ref, wk_ref, bk_ref)
    for h in heads:
        qs[:, hcols(h)] = l2norm(qc[:, hcols(h)]) * (dk ** -0.5)
        ks[:, hcols(h)] = l2norm(kc[:, hcols(h)])
    vs[...] = conv_silu(v_ref, wv_ref, bv_ref)
    if zero_init:
        st_s[...] = jnp.zeros_like(st_s)
    else:
        st_s[...] = s0_ref[...]

    ri = lax.broadcasted_iota(jnp.int32, (CHUNK, CHUNK), 0)
    ci = lax.broadcasted_iota(jnp.int32, (CHUNK, CHUNK), 1)
    incl = (ri >= ci, ri <= ci)
    unroll = math.gcd(PREP_UNROLL // n_heads, n_chunks)
    inv_consts = _inverse_consts()
    strict = (ri > ci, ri < ci)
    n_beta = N_DIRS

    last_row = (CHUNK - 1, 0)

    def prep_group(first_chunk, chunks):
        blocks = [(c, h) for c in chunks for h in heads]
        chains = [(c, h, d) for c, h in blocks for d in range(N_DIRS)]
        cidx = {c: first_chunk + c for c in chunks}
        r0 = {c: pl.multiple_of(cidx[c] * CHUNK, CHUNK) for c in chunks}
        q = {(c, h): qs[pl.ds(r0[c], CHUNK), hcols(h)] for c, h in blocks}
        k = {(c, h): ks[pl.ds(r0[c], CHUNK), hcols(h)] for c, h in blocks}
        v = {(c, h): vs[pl.ds(r0[c], CHUNK), hcols(h)] for c, h in blocks}
        gc4 = {(c, h): gcol_ref[h, pl.ds(r0[c], CHUNK), :] for c, h in blocks}
        gr4 = {(c, h): grow_ref[h, cidx[c]] for c, h in blocks}
        kq = {}
        for ch in blocks:
            k16 = k[ch].astype(BF16)
            kq[ch] = _mm_nt(jnp.concatenate([k16, q[ch].astype(BF16)], axis=0), k16)
        beta, gcl, decay, a = {}, {}, {}, []
        for c, h, d in chains:
            beta[c, h, d] = gc4[c, h][:, d:d + 1]
            gcl[c, h, d] = gc4[c, h][:, n_beta + d:n_beta + d + 1]
            grw = gr4[c, h][n_beta + d:n_beta + d + 1, :]
            decay[c, h, d] = jnp.exp(jnp.where(incl[d], gcl[c, h, d] - grw, -jnp.inf))
            a.append(jnp.where(strict[d], beta[c, h, d] * kq[c, h][:CHUNK] * decay[c, h, d], 0.0))
        t = dict(zip(chains, _unit_tri_inverse(a, inv_consts)))
        eg = {chd: jnp.exp(gcl[chd]) for chd in chains}
        sol = {(c, h, d): _mm(t[c, h, d], jnp.concatenate(
            [v[c, h] * beta[c, h, d], k[c, h] * (beta[c, h, d] * eg[c, h, d])], axis=1))
            for c, h, d in chains}
        aw = {(c, h, d): _mm(kq[c, h][CHUNK:] * decay[c, h, d], sol[c, h, d])
              for c, h, d in chains}
        kuw = {}
        for c, h, d in chains:
            g_last = gcl[c, h, d][last_row[d]:last_row[d] + 1, :]
            kuw[c, h, d] = _mm_tn(k[c, h] * jnp.exp(g_last - gcl[c, h, d]), sol[c, h, d])
        for c, h, d in chains:
            pq_s[d, h, cidx[c]] = jnp.concatenate(
                [kuw[c, h, d][:, dk:], q[c, h] * eg[c, h, d] - aw[c, h, d][:, dk:]], axis=0).astype(BF16)
            n_s[d, h, cidx[c]] = kuw[c, h, d][:, :dk]
        for c, h in blocks:
            o_s[pl.ds(r0[c], CHUNK), hcols(h)] = aw[c, h, 0][:, :dk] + aw[c, h, 1][:, :dk]

    def prep_body(i, carry):
        prep_group(i * unroll, range(unroll))
        return carry

    lax.fori_loop(0, n_chunks // unroll, prep_body, 0)

    def scan_body(n, carry):
        for h in heads:
            for d in range(N_DIRS):
                c = n if d == 0 else n_chunks - 1 - n
                r0 = pl.multiple_of(c * CHUNK, CHUNK)
                g_last = grow_ref[h, c][n_beta + d:n_beta + d + 1, last_row[d]:last_row[d] + 1]
                s = st_s[d, h]
                r = _dot(pq_s[d, h, c], s.astype(BF16))
                st_s[d, h] = s * jnp.exp(g_last) + n_s[d, h, c] - r[:dk]
                o_s[pl.ds(r0, CHUNK), hcols(h)] += r[dk:]
        return carry

    lax.fori_loop(0, n_chunks, scan_body, 0)

    o = o_s[...]
    z = _silu(z_ref[...])
    for h in heads:
        y_ref[:, hcols(h)] = (_rms(o[:, hcols(h)]) * og_ref[...] * z[:, hcols(h)]).astype(y_ref.dtype)
    if write_state:
        sfin_ref[...] = st_s[...]


def _delta(qkvz, gcol, grow, sc_w, sc_b, o_g, s0, layer, seq_len, n_batch, row_off, dk, write_state):
    nh = DN_HEADS
    blk0 = row_off // seq_len
    n_chunks = seq_len // CHUNK
    zero_init = s0 is None
    kern = functools.partial(_delta_kernel, seq_len=seq_len, dk=dk, zero_init=zero_init,
                             write_state=write_state)

    hp = DN_HEADS_PER_STEP
    gw = hp * dk
    groups = nh // hp

    def col(group):
        return pl.BlockSpec((seq_len, gw), lambda b, g: (blk0 + b, group * groups + g))

    def wspec(group, rows):
        return pl.BlockSpec((rows, gw), lambda b, g: (0, group * groups + g))

    width = sc_w.shape[0]
    in_specs = [col(0), col(1), col(2), col(3),
                pl.BlockSpec((hp, seq_len, 2 * N_DIRS), lambda b, g: (g, blk0 + b, 0)),
                pl.BlockSpec((hp, n_chunks, 2 * N_DIRS, CHUNK), lambda b, g: (g, blk0 + b, 0, 0)),
                wspec(0, width), wspec(1, width), wspec(2, width),
                wspec(0, 1), wspec(1, 1), wspec(2, 1),
                pl.BlockSpec((1, dk), lambda b, g: (0, 0))]
    args = [qkvz, qkvz, qkvz, qkvz, gcol, grow, sc_w, sc_w, sc_w, sc_b, sc_b, sc_b, o_g]
    if not zero_init:
        in_specs.append(pl.BlockSpec((None, None, N_DIRS, hp, dk, dk),
                                     lambda b, g: (b, layer, 0, g, 0, 0)))
        args.append(s0)
    y_shape = jax.ShapeDtypeStruct((n_batch * seq_len, nh * dk), BF16)
    y_spec = pl.BlockSpec((seq_len, gw), lambda b, g: (b, g))
    if write_state:
        out_shape = (y_shape, jax.ShapeDtypeStruct((n_batch, N_DIRS, nh, dk, dk), F32))
        out_specs = (y_spec, pl.BlockSpec((None, N_DIRS, hp, dk, dk), lambda b, g: (b, 0, g, 0, 0)))
    else:
        out_shape = y_shape
        out_specs = y_spec
    return pl.pallas_call(
        kern,
        out_shape=out_shape,
        grid=(n_batch, groups),
        in_specs=in_specs,
        out_specs=out_specs,
        scratch_shapes=[pltpu.VMEM((seq_len, gw), F32)] * 4 + [
            pltpu.VMEM((N_DIRS, hp, dk, dk), F32),
            pltpu.VMEM((N_DIRS, hp, n_chunks, dk + CHUNK, dk), BF16),
            pltpu.VMEM((N_DIRS, hp, n_chunks, dk, dk), F32)],
        compiler_params=_cparams(("parallel", "parallel")),
        name="delta",
    )(*args)


def _conformer_kernel(x_ref, w_ref, b_ref, lg_ref, lb_ref, o_ref, pad_s, *, n_seg, seg_blocks, halo):
    c = o_ref.shape[-1]
    width = w_ref.shape[0]
    pad = (width - 1) // 2
    seg_len = seg_blocks * GRID_W
    pad_s[:, 0:halo, :] = jnp.zeros((n_seg, halo, c), F32)
    pad_s[:, halo + seg_len:2 * halo + seg_len, :] = jnp.zeros((n_seg, halo, c), F32)
    for j in range(seg_blocks):
        if seg_blocks == 1:
            x = x_ref[...]
            pad_s[:, halo:halo + GRID_W, :] = x[..., :c] * _sigmoid(x[..., c:])
        else:
            x = x_ref[j]
            pad_s[0, halo + j * GRID_W:halo + (j + 1) * GRID_W, :] = x[:, :c] * _sigmoid(x[:, c:])

    win = GRID_W + 2 * halo

    def seg_body(r, carry):
        for j in range(seg_blocks):
            x = pad_s[r, j * GRID_W:j * GRID_W + win, :]
            shifted = [x] + [pltpu.roll(x, win - b, axis=0) for b in range(1, SUBLANE)]
            acc = jnp.zeros((GRID_W, c), F32) + b_ref[...]
            for s in range(width):
                o = halo - pad + s
                a0 = (o // SUBLANE) * SUBLANE
                acc = acc + shifted[o % SUBLANE][a0:a0 + GRID_W, :] * w_ref[s:s + 1, :]
            mu = jnp.mean(acc, axis=-1, keepdims=True)
            xc = acc - mu
            var = jnp.mean(xc * xc, axis=-1, keepdims=True)
            y = xc * lax.rsqrt(var + EPS) * lg_ref[...] + lb_ref[...]
            o_ref[r * seg_blocks + j] = _silu(y).astype(o_ref.dtype)
        return carry

    lax.fori_loop(0, n_seg, seg_body, 0)


def _conformer(cv, dw_w, dw_b, ln_g, ln_b, seq_len, n_batch, row_off, latent):
    t, c2 = cv.shape
    c = c2 // 2
    cv3 = cv.reshape(t // GRID_W, GRID_W, c2)
    blocks = seq_len // GRID_W
    n_seg, seg_blocks = (blocks, 1) if latent else (1, blocks)
    halo = 2 * SUBLANE
    assert (dw_w.shape[0] - 1) // 2 <= halo
    blk0 = row_off // seq_len
    kern = functools.partial(_conformer_kernel, n_seg=n_seg, seg_blocks=seg_blocks, halo=halo)
    const = lambda b: (0, 0)
    out = pl.pallas_call(
        kern,
        out_shape=jax.ShapeDtypeStruct((n_batch * blocks, GRID_W, c), BF16),
        grid=(n_batch,),
        in_specs=[pl.BlockSpec((blocks, GRID_W, c2), lambda b: (blk0 + b, 0, 0)),
                  pl.BlockSpec(dw_w.shape, const),
                  pl.BlockSpec((1, c), const), pl.BlockSpec((1, c), const), pl.BlockSpec((1, c), const)],
        out_specs=pl.BlockSpec((blocks, GRID_W, c), lambda b: (b, 0, 0)),
        scratch_shapes=[pltpu.VMEM((n_seg, seg_blocks * GRID_W + 2 * halo, c), F32)],
        compiler_params=_cparams(("parallel",)),
        name="conformer",
    )(cv3, dw_w, dw_b, ln_g, ln_b)
    return out.reshape(n_batch * seq_len, c)


def _outproj_kernel(*refs, n_x, d, n_ctx_tiles):
    yfc, yfl, ydc, ydl, ycc, ycl, w_ref, mod_ref, o_ref = refs[n_x:]
    is_ctx = pl.program_id(0) < n_ctx_tiles
    y = None
    k0 = 0
    for a_c, a_l in ((yfc, yfl), (ydc, ydl), (ycc, ycl)):
        kw = a_c.shape[1]
        a = jnp.where(is_ctx, a_c[...], a_l[...])
        part = _dot(a, w_ref[k0:k0 + kw, :])
        y = part if y is None else y + part
        k0 += kw
    o_ref[...] = _read_stream(refs[:n_x], n_ctx_tiles) + mod_ref[:, 2 * d:3 * d] * y


def _outproj(parts, w_out, x, mod, mod_row, tm, n_ctx_tiles):
    in_specs, args = _stream_specs(x, tm, n_ctx_tiles)
    t = sum(a.shape[0] for a in args)
    d = w_out.shape[1]
    kern = functools.partial(_outproj_kernel, n_x=len(args), d=d, n_ctx_tiles=n_ctx_tiles)
    for a_c, a_l in parts:
        kw = a_c.shape[1]
        in_specs.append(pl.BlockSpec((tm, kw), lambda i: (jnp.minimum(i, n_ctx_tiles - 1), 0)))
        in_specs.append(pl.BlockSpec((tm, kw), lambda i: (jnp.maximum(i - n_ctx_tiles, 0), 0)))
        args += [a_c, a_l]
    in_specs += [pl.BlockSpec(w_out.shape, lambda i: (0, 0)),
                 pl.BlockSpec((None, 1, N_MOD * d), lambda i: (mod_row(i), 0, 0))]
    args += [w_out, mod]
    return pl.pallas_call(
        kern,
        out_shape=jax.ShapeDtypeStruct((t, d), F32),
        grid=(t // tm,),
        in_specs=in_specs,
        out_specs=pl.BlockSpec((tm, d), lambda i: (i, 0)),
        compiler_params=_cparams(("parallel",)),
        name="outproj",
    )(*args)


def _ffn_kernel(*refs, d, final_norm):
    x_ref, g_ref, mod_ref, wg_ref, wu_ref, wd_ref = refs[:6]
    fg_ref = refs[6] if final_norm else None
    o_ref, h_s, acc_s = refs[-3:]
    f = pl.program_id(1)

    @pl.when(f == 0)
    def _():
        mod = mod_ref[...]
        h = _rms(x_ref[...]) * g_ref[...]
        h = h * (1.0 + mod[:, 4 * d:5 * d]) + mod[:, 3 * d:4 * d]
        h_s[...] = h.astype(BF16)
        acc_s[...] = jnp.zeros_like(acc_s)

    h = h_s[...]
    a = _silu(_dot(h, wg_ref[...].astype(BF16))) * _dot(h, wu_ref[...].astype(BF16))
    acc_s[...] += _dot(a.astype(BF16), wd_ref[...].astype(BF16))

    @pl.when(f == pl.num_programs(1) - 1)
    def _():
        o = x_ref[...] + mod_ref[:, 5 * d:6 * d] * acc_s[...]
        if final_norm:
            o = _rms(o) * fg_ref[...]
        o_ref[...] = o


def _ffn(x, gain, mod, mod_row, wg, wu, wd, final_gain, tm, tf):
    t, d = x.shape
    ff = wg.shape[1]
    final_norm = final_gain is not None
    kern = functools.partial(_ffn_kernel, d=d, final_norm=final_norm)
    row = lambda i, f: (i, 0)
    const = lambda i, f: (0, 0)
    in_specs = [pl.BlockSpec((tm, d), row),
                pl.BlockSpec((1, d), const),
                pl.BlockSpec((None, 1, N_MOD * d), lambda i, f: (mod_row(i), 0, 0)),
                pl.BlockSpec((d, tf), lambda i, f: (0, f)),
                pl.BlockSpec((d, tf), lambda i, f: (0, f)),
                pl.BlockSpec((tf, d), lambda i, f: (f, 0))]
    args = [x, gain, mod, wg, wu, wd]
    if final_norm:
        in_specs.append(pl.BlockSpec((1, d), const))
        args.append(final_gain)
    return pl.pallas_call(
        kern,
        out_shape=jax.ShapeDtypeStruct((t, d), F32),
        grid=(t // tm, ff // tf),
        in_specs=in_specs,
        out_specs=pl.BlockSpec((tm, d), row),
        scratch_shapes=[pltpu.VMEM((tm, d), BF16), pltpu.VMEM((tm, d), F32)],
        compiler_params=_cparams(("parallel", "arbitrary")),
        name="ffn",
    )(*args)


INFO_E, INFO_W, INFO_R = 0, 2, 4
MOE_TILE = 1024
MOE_SUB = 256


def _route_kernel(x_ref, g_ref, mod_ref, router_ref, h_o, info_o, cnt_o, cnt_s, *, d, n_exp):
    i = pl.program_id(0)

    @pl.when(i == 0)
    def _():
        cnt_s[...] = jnp.zeros_like(cnt_s)

    mod = mod_ref[...]
    h = _rms(x_ref[...]) * g_ref[...]
    h = h * (1.0 + mod[:, 4 * d:5 * d]) + mod[:, 3 * d:4 * d]
    h_o[...] = h

    logits = _mm3(h, router_ref[...])
    tm = logits.shape[0]
    lane = lax.broadcasted_iota(jnp.int32, logits.shape, 1)
    lg = jnp.where(lane < n_exp, logits, -jnp.inf)
    m1 = jnp.max(lg, axis=-1, keepdims=True)
    i1 = jnp.min(jnp.where(lg == m1, lane, LANE), axis=-1, keepdims=True)
    lg2 = jnp.where(lane == i1, -jnp.inf, lg)
    m2 = jnp.max(lg2, axis=-1, keepdims=True)
    i2 = jnp.min(jnp.where(lg2 == m2, lane, LANE), axis=-1, keepdims=True)
    w1 = 1.0 / (1.0 + jnp.exp(m2 - m1))

    sel1 = lane == i1
    sel2 = lane == i2
    member = jnp.where(sel1 | sel2, 1.0, 0.0)
    ri = lax.broadcasted_iota(jnp.int32, (tm, tm), 0)
    ci = lax.broadcasted_iota(jnp.int32, (tm, tm), 1)
    before = jnp.where(ri > ci, 1.0, 0.0)
    rank = cnt_s[...] + _mm(before, member)
    r1 = jnp.sum(jnp.where(sel1, rank, 0.0), axis=-1, keepdims=True)
    r2 = jnp.sum(jnp.where(sel2, rank, 0.0), axis=-1, keepdims=True)
    cnt_s[...] += jnp.sum(member, axis=0, keepdims=True)
    cnt_o[...] = jnp.broadcast_to(cnt_s[...], cnt_o.shape)

    info = jnp.zeros(logits.shape, F32)
    for ln, val in ((INFO_E, i1.astype(F32)), (INFO_E + 1, i2.astype(F32)), (INFO_W, w1),
                    (INFO_W + 1, 1.0 - w1), (INFO_R, r1), (INFO_R + 1, r2)):
        info = jnp.where(lane == ln, val, info)
    info_o[...] = info


def _route(x, gain, mod, mod_row, router, tm):
    t, d = x.shape
    n_exp = router.shape[1]
    router_p = jnp.zeros((d, LANE), F32).at[:, :n_exp].set(router)
    kern = functools.partial(_route_kernel, d=d, n_exp=n_exp)
    return pl.pallas_call(
        kern,
        out_shape=(jax.ShapeDtypeStruct((t, d), F32), jax.ShapeDtypeStruct((t, LANE), F32),
                   jax.ShapeDtypeStruct((SUBLANE, LANE), F32)),
        grid=(t // tm,),
        in_specs=[pl.BlockSpec((tm, d), lambda i: (i, 0)),
                  pl.BlockSpec((1, d), lambda i: (0, 0)),
                  pl.BlockSpec((None, 1, N_MOD * d), lambda i: (mod_row(i), 0, 0)),
                  pl.BlockSpec((d, LANE), lambda i: (0, 0))],
        out_specs=(pl.BlockSpec((tm, d), lambda i: (i, 0)), pl.BlockSpec((tm, LANE), lambda i: (i, 0)),
                   pl.BlockSpec((SUBLANE, LANE), lambda i: (0, 0))),
        scratch_shapes=[pltpu.VMEM((1, LANE), F32)],
        compiler_params=_cparams(("arbitrary",)),
        name="moe_route",
    )(x, gain, mod, router_p)


def _row_copy(src_hbm, row, dst, slot, sem):
    return pltpu.make_async_copy(src_hbm.at[pl.ds(row, 1), :], dst.at[pl.ds(slot, 1), :], sem)


def _experts_kernel(expert_sm, nvalid_sm, tok_ref, tok_next_ref, h_hbm, wg_ref, wu_ref, wd_ref, y_ref,
                    hbuf, h16, acc, wg16, wu16, wd16, sems, *, sub, n_f):
    del expert_sm
    i = pl.program_id(0)
    j = pl.program_id(1)
    n_i = pl.num_programs(0)
    tm = h16.shape[0]
    slot = i % 2
    nxt = 1 - slot
    nv = nvalid_sm[i]
    n_sub = (nv + sub - 1) // sub
    has_next = jnp.where(i + 1 < n_i, nvalid_sm[jnp.minimum(i + 1, n_i - 1)], 0) > 0
    per_step = -(-tm // n_f)
    base = j * per_step

    def start_row(tok, r, buf):
        _row_copy(h_hbm, tok[0, r], hbuf.at[buf], r, sems.at[buf]).start()

    def fetch_loop(tok, lo, hi, buf):
        def issue(r, carry):
            start_row(tok, r, buf)
            return carry

        lax.fori_loop(lo, hi, issue, 0)

    @pl.when(j == 0)
    def _():
        @pl.when((i == 0) & (nv > 0))
        def _():
            fetch_loop(tok_ref, 0, tm, 0)

        acc[...] = jnp.zeros_like(acc)

        @pl.when(nv > 0)
        def _():
            pltpu.make_async_copy(h_hbm.at[pl.ds(0, tm), :], hbuf.at[slot], sems.at[slot]).wait()

            def cast(s, carry):
                rows = pl.ds(pl.multiple_of(s * sub, sub), sub)
                h16[rows, :] = hbuf[slot, rows, :].astype(BF16)
                return carry

            lax.fori_loop(0, n_sub, cast, 0)

    fast = (n_sub == tm // sub) & has_next

    @pl.when(fast)
    def _():
        n_safe = tm - (n_f - 1) * per_step
        for r in range(min(per_step, n_safe)):
            start_row(tok_next_ref, base + r, nxt)
        h = h16[...]
        a = _silu(_dot(h, wg_ref[...].astype(BF16))) * _dot(h, wu_ref[...].astype(BF16))
        acc[...] += _dot(a.astype(BF16), wd_ref[...].astype(BF16))
        for r in range(n_safe, per_step):
            pl.when(base + r < tm)(functools.partial(start_row, tok_next_ref, base + r, nxt))

    @pl.when(jnp.logical_not(fast))
    def _():
        @pl.when(has_next)
        def _():
            fetch_loop(tok_next_ref, base, jnp.minimum(base + per_step, tm), nxt)

        @pl.when(n_sub > 0)
        def _():
            wg16[...] = wg_ref[...].astype(BF16)
            wu16[...] = wu_ref[...].astype(BF16)
            wd16[...] = wd_ref[...].astype(BF16)

            def block(s, carry):
                rows = pl.ds(pl.multiple_of(s * sub, sub), sub)
                h = h16[rows, :]
                a = _silu(_dot(h, wg16[...])) * _dot(h, wu16[...])
                acc[rows, :] += _dot(a.astype(BF16), wd16[...])
                return carry

            lax.fori_loop(0, n_sub, block, 0)

    @pl.when(j == n_f - 1)
    def _():
        y_ref[...] = acc[...]


def _experts(h, src_tok, tile_expert, tile_nvalid, wg, wu, wd, tm, tf, sub):
    t, d = h.shape
    n_exp, _, ff = wg.shape
    n_tiles = tile_expert.shape[0]
    n_f = ff // tf

    def w_idx(i, j, expert_sm, nvalid_sm):
        return expert_sm[i], jnp.where(nvalid_sm[i] > 0, j, n_f - 1)

    def wgu_map(i, j, expert_sm, nvalid_sm):
        e, jj = w_idx(i, j, expert_sm, nvalid_sm)
        return e, 0, jj

    def wd_map(i, j, expert_sm, nvalid_sm):
        e, jj = w_idx(i, j, expert_sm, nvalid_sm)
        return e, jj, 0

    kern = functools.partial(_experts_kernel, sub=sub, n_f=n_f)
    tok = src_tok.reshape(n_tiles, 1, tm)
    grid_spec = pltpu.PrefetchScalarGridSpec(
        num_scalar_prefetch=2,
        grid=(n_tiles, n_f),
        in_specs=[pl.BlockSpec((None, 1, tm), lambda i, j, *_: (i, 0, 0), memory_space=pltpu.SMEM),
                  pl.BlockSpec((None, 1, tm), lambda i, j, *_: (jnp.minimum(i + 1, n_tiles - 1), 0, 0),
                               memory_space=pltpu.SMEM),
                  pl.BlockSpec(memory_space=pl.ANY),
                  pl.BlockSpec((None, d, tf), wgu_map),
                  pl.BlockSpec((None, d, tf), wgu_map),
                  pl.BlockSpec((None, tf, d), wd_map)],
        out_specs=pl.BlockSpec((tm, d), lambda i, j, *_: (i, 0)),
        scratch_shapes=[pltpu.VMEM((2, tm, d), F32), pltpu.VMEM((tm, d), BF16), pltpu.VMEM((tm, d), F32),
                        pltpu.VMEM((d, tf), BF16), pltpu.VMEM((d, tf), BF16), pltpu.VMEM((tf, d), BF16),
                        pltpu.SemaphoreType.DMA((2,))])
    return pl.pallas_call(
        kern,
        out_shape=jax.ShapeDtypeStruct((n_tiles * tm, d), F32),
        grid_spec=grid_spec,
        compiler_params=_cparams(("arbitrary", "arbitrary")),
        name="moe_experts",
    )(tile_expert, tile_nvalid, tok, tok, h, wg, wu, wd)


def _combine_kernel(*refs, d, final_norm, split_tiles):
    pos_ref, pos_next_ref, info_ref, x_ref, mod_ref = refs[:5]
    p = 5
    if final_norm:
        fg_ref = refs[p]
        p += 1
    y_hbm = refs[p]
    o_refs = refs[p + 1:-2]
    ybuf, sems = refs[-2:]
    tm = x_ref.shape[0]
    i = pl.program_id(0)
    slot = i % 2

    def fetch(pos, buf):
        def issue(r, carry):
            for k in range(TOP_K):
                _row_copy(y_hbm, pos[0, k * tm + r], ybuf.at[buf, k], r, sems.at[buf]).start()
            return carry

        lax.fori_loop(0, tm, issue, 0, unroll=8)

    @pl.when(i == 0)
    def _():
        fetch(pos_ref, 0)

    @pl.when(i + 1 < pl.num_programs(0))
    def _():
        fetch(pos_next_ref, 1 - slot)

    for k in range(TOP_K):
        pltpu.make_async_copy(y_hbm.at[pl.ds(0, tm), :], ybuf.at[slot, k], sems.at[slot]).wait()

    info = info_ref[...]
    f = None
    for k in range(TOP_K):
        term = info[:, INFO_W + k:INFO_W + k + 1] * ybuf[slot, k]
        f = term if f is None else f + term
    o = x_ref[...] + mod_ref[:, 5 * d:6 * d] * f
    if final_norm:
        o = _rms(o) * fg_ref[...]
    if split_tiles is None:
        o_refs[0][...] = o
    else:
        @pl.when(i < split_tiles)
        def _():
            o_refs[0][...] = o

        @pl.when(i >= split_tiles)
        def _():
            o_refs[1][...] = o


def _combine(x, info, pos, y, mod, mod_row, final_gain, tm, split_rows):
    t, d = x.shape
    final_norm = final_gain is not None
    split_tiles = None if split_rows is None else split_rows // tm
    kern = functools.partial(_combine_kernel, d=d, final_norm=final_norm, split_tiles=split_tiles)
    n_tiles = t // tm
    if split_tiles is None:
        out_shape = jax.ShapeDtypeStruct((t, d), F32)
        out_specs = pl.BlockSpec((tm, d), lambda i: (i, 0))
    else:
        out_shape = (jax.ShapeDtypeStruct((split_rows, d), F32), jax.ShapeDtypeStruct((t - split_rows, d), F32))
        out_specs = (pl.BlockSpec((tm, d), lambda i: (jnp.minimum(i, split_tiles - 1), 0)),
                     pl.BlockSpec((tm, d), lambda i: (jnp.maximum(i - split_tiles, 0), 0)))
    in_specs = [pl.BlockSpec((None, 1, TOP_K * tm), lambda i: (i, 0, 0), memory_space=pltpu.SMEM),
                pl.BlockSpec((None, 1, TOP_K * tm), lambda i: (jnp.minimum(i + 1, n_tiles - 1), 0, 0),
                             memory_space=pltpu.SMEM),
                pl.BlockSpec((tm, LANE), lambda i: (i, 0)),
                pl.BlockSpec((tm, d), lambda i: (i, 0)),
                pl.BlockSpec((None, 1, N_MOD * d), lambda i: (mod_row(i), 0, 0))]
    args = [pos, pos, info, x, mod]
    if final_norm:
        in_specs.append(pl.BlockSpec((1, d), lambda i: (0, 0)))
        args.append(final_gain)
    in_specs.append(pl.BlockSpec(memory_space=pl.ANY))
    args.append(y)
    return pl.pallas_call(
        kern,
        out_shape=out_shape,
        grid=(t // tm,),
        in_specs=in_specs,
        out_specs=out_specs,
        scratch_shapes=[pltpu.VMEM((2, TOP_K, tm, d), F32), pltpu.SemaphoreType.DMA((2,))],
        compiler_params=_cparams(("arbitrary",)),
        name="moe_combine",
    )(*args)


def _moe(x, gain, mod, mod_row_fn, router, wg, wu, wd, final_gain, split_rows, tm_route, tm_exp, tm_comb, tf, sub):
    t, d = x.shape
    n_exp = wg.shape[0]
    h, info, cnt = _route(x, gain, mod, mod_row_fn(tm_route), router, tm_route)

    counts = cnt[0, :n_exp].astype(jnp.int32)
    tiles_per = (counts + tm_exp - 1) // tm_exp
    tile_end = jnp.cumsum(tiles_per)
    tile_start = tile_end - tiles_per
    n_tiles = (TOP_K * t) // tm_exp + n_exp
    tile_id = jnp.arange(n_tiles, dtype=jnp.int32)
    tile_expert = jnp.minimum(jnp.sum(tile_id[:, None] >= tile_end[None, :], axis=1), n_exp - 1).astype(jnp.int32)
    tile_nvalid = jnp.where(tile_id < tile_end[n_exp - 1],
                            jnp.clip(counts[tile_expert] - (tile_id - tile_start[tile_expert]) * tm_exp, 0, tm_exp),
                            0).astype(jnp.int32)
    slot_start = tile_start * tm_exp
    experts = info[:, INFO_E:INFO_E + TOP_K].astype(jnp.int32)
    slots = slot_start[experts] + info[:, INFO_R:INFO_R + TOP_K].astype(jnp.int32)
    token = jnp.broadcast_to(jnp.arange(t, dtype=jnp.int32)[:, None], slots.shape)
    src_tok = jnp.zeros((n_tiles * tm_exp,), jnp.int32).at[slots.reshape(-1)].set(
        token.reshape(-1), unique_indices=True)

    y = _experts(h, src_tok, tile_expert, tile_nvalid, wg, wu, wd, tm_exp, tf, sub)
    pos = slots.reshape(t // tm_comb, tm_comb, TOP_K).transpose(0, 2, 1).reshape(t // tm_comb, 1, TOP_K * tm_comb)
    return _combine(x, info, pos, y, mod, mod_row_fn(tm_comb), final_gain, tm_comb, split_rows)


def _lane_row(vals, offset):
    n = vals.shape[0]
    return jnp.zeros((1, LANE), F32).at[0, offset:offset + n].set(vals.astype(F32))


def kernel(x_prompt, x_sample, state_delta, c, c_ctx, norm1, norm2, w_mod, b_mod, w_in, sc_w, sc_b, dn_a_log, dn_dt_bias, dn_norm, cv_dw_w, cv_dw_b, cv_ln_g, cv_ln_b, w_out, ffn_wg, ffn_wu, ffn_wd, moe_router, moe_wg, moe_wu, moe_wd, final_norm):
    b_ctx, l_ctx, d = x_prompt.shape
    b_lat, l_lat, _ = x_sample.shape
    depth = w_in.shape[0]
    t_ctx = b_ctx * l_ctx
    t_lat = b_lat * l_lat
    f_w = d // 4
    f_gw = f_w // F_GROUPS
    dn_w = d // 2
    dk = dn_w // DN_HEADS
    cv_w = d // 4
    qkvz_w = 4 * dn_w
    n_gate = 2 * N_DIRS * DN_HEADS
    n_exp = moe_wg.shape[1]
    assert dk == LANE and t_ctx % l_lat == 0 and b_lat + 1 <= MOD_ROWS
    assert l_ctx % CHUNK == 0 and l_lat % CHUNK == 0 and l_lat % GRID_W == 0 and CHUNK == GRID_W

    tm = _pick_tile(512, t_ctx, l_lat)
    tm_ffn = _pick_tile(1024, t_ctx, l_lat)

    def mod_row_fn(tile):
        n_ctx_tiles = t_ctx // tile
        per_seq = l_lat // tile
        return lambda i: jnp.where(i < n_ctx_tiles, 0, 1 + (i - n_ctx_tiles) // per_seq)

    x = (x_prompt.reshape(t_ctx, d), x_sample.reshape(t_lat, d))
    cvec = jnp.zeros((MOD_ROWS, d), F32).at[0].set(c_ctx).at[1:1 + b_lat].set(c)
    mods = _adaln(cvec, w_mod, b_mod).reshape(depth, MOD_ROWS, 1, N_MOD * d)

    cg, sg = _dft_tables(f_gw)
    eye_g = np.eye(F_GROUPS)
    ccs = jnp.asarray(np.concatenate([np.kron(eye_g, cg), np.kron(eye_g, sg)], axis=1), F32).astype(BF16)

    passes = ((l_ctx, b_ctx, 0, False), (l_lat, b_lat, t_ctx, True))
    o_q = f_w
    o_ba = f_w + qkvz_w
    o_cv = o_ba + n_gate
    new_states = []
    for l in range(depth):
        w = w_in[l]
        w_perm = jnp.concatenate(
            [w[:, :o_ba], w[:, o_cv:o_cv + 2 * cv_w], w[:, o_ba:o_cv], jnp.zeros((d, LANE - n_gate), F32)],
            axis=1).astype(BF16)
        alog = _lane_row(dn_a_log[l].reshape(-1), N_DIRS * DN_HEADS)
        dtb = _lane_row(dn_dt_bias[l].reshape(-1), N_DIRS * DN_HEADS)
        xc, xs, qkvz, cv, gates = _proj(x, norm1[l][None], mods[l], w_perm, ccs, alog, dtb,
                                        mod_row_fn(tm), tm, t_ctx // tm, f_w, qkvz_w, 2 * cv_w)
        g16 = gates[:, :n_gate].reshape(-1, 2 * N_DIRS, DN_HEADS)
        gcol = g16.transpose(2, 0, 1)
        grow = g16.reshape(-1, CHUNK, 2 * N_DIRS, DN_HEADS).transpose(3, 0, 2, 1)

        parts = [[], [], []]
        for seq_len, n_batch, row_off, latent in passes:
            parts[0].append(_fourier(xc, xs, seq_len, n_batch, row_off, f_gw))
            res = _delta(qkvz, gcol, grow, sc_w[l], sc_b[l][None], dn_norm[l][None],
                         state_delta if latent else None, l, seq_len, n_batch, row_off, dk,
                         write_state=not latent)
            if latent:
                parts[1].append(res)
            else:
                parts[1].append(res[0])
                new_states.append(res[1])
            parts[2].append(_conformer(cv, cv_dw_w[l], cv_dw_b[l][None], cv_ln_g[l][None],
                                       cv_ln_b[l][None], seq_len, n_batch, row_off, latent))
        x = _outproj(parts, w_out[l].astype(BF16), x, mods[l], mod_row_fn(tm), tm, t_ctx // tm)

        fg = final_norm[None] if l == depth - 1 else None
        if l % 2 == 0:
            i = l // 2
            x = _ffn(x, norm2[l][None], mods[l], mod_row_fn(tm_ffn), ffn_wg[i], ffn_wu[i], ffn_wd[i], fg,
                     tm_ffn, _pick_tile(512, ffn_wg.shape[-1]))
        else:
            i = l // 2
            x = _moe(x, norm2[l][None], mods[l], mod_row_fn, moe_router[i], moe_wg[i], moe_wu[i], moe_wd[i], fg,
                     split_rows=t_ctx if l == depth - 1 else None,
                     tm_route=tm, tm_exp=MOE_TILE, tm_comb=_pick_tile(256, t_ctx, l_lat),
                     tf=_pick_tile(512, moe_wg.shape[-1]), sub=MOE_SUB)

    y_ctx, y_lat = x if isinstance(x, tuple) else (x[:t_ctx], x[t_ctx:])
    y_prompt = y_ctx.reshape(b_ctx, l_ctx, d)
    y_sample = y_lat.reshape(b_lat, l_lat, d)
    return y_prompt, y_sample, jnp.stack(new_states, axis=1)
```

```python
import functools
import math

import numpy as np
import jax
import jax.numpy as jnp
from jax import lax
from jax.experimental import pallas as pl
from jax.experimental.pallas import tpu as pltpu

F32 = jnp.float32
BF16 = jnp.bfloat16
EPS = 1e-6

LANE = 128
SUBLANE = 8
VMEM_LIMIT = 56 * 1024 * 1024

CHUNK = 64
GRID_W = 64
F_GROUPS = 4
DN_HEADS = 4
N_DIRS = 2
DELTA_VMEM_BUDGET = VMEM_LIMIT - 12 * 1024 * 1024
PREP_UNROLL = 16
TOP_K = 2
N_MOD = 6
MOD_ROWS = 16


def _cparams(sem):
    return pltpu.CompilerParams(dimension_semantics=sem, vmem_limit_bytes=VMEM_LIMIT)


def _pick_tile(limit, *sizes):
    t = limit
    while any(s % t for s in sizes):
        t //= 2
    return t


def _dot(a, b):
    return jnp.dot(a, b, preferred_element_type=F32)


def _mm(a, b):
    return jnp.dot(a.astype(BF16), b.astype(BF16), preferred_element_type=F32)


def _mm_nt(a, b):
    return lax.dot_general(a.astype(BF16), b.astype(BF16), (((1,), (1,)), ((), ())),
                           preferred_element_type=F32)


def _mm_tn(a, b):
    return lax.dot_general(a.astype(BF16), b.astype(BF16), (((0,), (0,)), ((), ())),
                           preferred_element_type=F32)


def _split(a):
    hi = a.astype(BF16)
    lo = (a - hi.astype(F32)).astype(BF16)
    return hi, lo


def _mm3(a, b):
    ah, al = _split(a)
    bh, bl = _split(b)
    return _dot(ah, bh) + (_dot(ah, bl) + _dot(al, bh))


def _sigmoid(x):
    return 1.0 / (1.0 + jnp.exp(-x))


def _silu(x):
    return x * _sigmoid(x)


def _rms(x):
    return x * lax.rsqrt(jnp.mean(x * x, axis=-1, keepdims=True) + EPS)


def _stream_specs(x, tm, n_ctx_tiles):
    if isinstance(x, tuple):
        d = x[0].shape[1]
        return ([pl.BlockSpec((tm, d), lambda i, *_: (jnp.minimum(i, n_ctx_tiles - 1), 0)),
                 pl.BlockSpec((tm, d), lambda i, *_: (jnp.maximum(i - n_ctx_tiles, 0), 0))], list(x))
    return [pl.BlockSpec((tm, x.shape[1]), lambda i, *_: (i, 0))], [x]


def _read_stream(x_refs, n_ctx_tiles):
    if len(x_refs) == 2:
        return jnp.where(pl.program_id(0) < n_ctx_tiles, x_refs[0][...], x_refs[1][...])
    return x_refs[0][...]


def _adaln_kernel(c_ref, w_ref, b_ref, o_ref):
    o_ref[...] = _mm(_silu(c_ref[...]), w_ref[...]) + b_ref[...]


def _adaln(cvec, w_mod, b_mod):
    depth, d, n = w_mod.shape
    tn = _pick_tile(1024, n)
    return pl.pallas_call(
        _adaln_kernel,
        out_shape=jax.ShapeDtypeStruct((depth, MOD_ROWS, n), F32),
        grid=(depth, n // tn),
        in_specs=[pl.BlockSpec((MOD_ROWS, d), lambda l, j: (0, 0)),
                  pl.BlockSpec((None, d, tn), lambda l, j: (l, 0, j)),
                  pl.BlockSpec((None, 1, tn), lambda l, j: (l, 0, j))],
        out_specs=pl.BlockSpec((None, MOD_ROWS, tn), lambda l, j: (l, 0, j)),
        compiler_params=_cparams(("parallel", "parallel")),
        name="adaln",
    )(cvec, w_mod, b_mod.reshape(depth, 1, n))


def _proj_kernel(*refs, n_x, n_ctx_tiles, d, f_w, qkvz_w, cv_w):
    g_ref, mod_ref, w_ref, ccs_ref, alog_ref, dtb_ref, xc_o, xs_o, qkvz_o, cv_o, gate_o = refs[n_x:]
    x = _read_stream(refs[:n_x], n_ctx_tiles)
    mod = mod_ref[...]
    h = _rms(x) * g_ref[...]
    h = (h * (1.0 + mod[:, d:2 * d]) + mod[:, 0:d]).astype(BF16)

    xf = _dot(h, w_ref[:, 0:f_w])
    xcs = _dot(xf.astype(BF16), ccs_ref[...])
    xc_o[...] = xcs[:, :f_w].astype(xc_o.dtype)
    xs_o[...] = xcs[:, f_w:].astype(xs_o.dtype)
    o = f_w
    step = 4 * LANE
    for n0 in range(0, qkvz_w, step):
        qkvz_o[:, n0:n0 + step] = _dot(h, w_ref[:, o + n0:o + n0 + step])
    o += qkvz_w
    cv_o[...] = _dot(h, w_ref[:, o:o + cv_w])
    o += cv_w
    ba = _dot(h, w_ref[:, o:o + LANE])

    tm = ba.shape[0]
    lane = lax.broadcasted_iota(jnp.int32, ba.shape, 1)
    row = lax.broadcasted_iota(jnp.int32, ba.shape, 0) % CHUNK
    beta = _sigmoid(ba)
    t = ba + dtb_ref[...]
    softplus = jnp.maximum(t, 0.0) + jnp.log1p(jnp.exp(-jnp.abs(t)))
    g = -jnp.exp(alog_ref[...]) * softplus
    cf = g
    cb = g
    s = 1
    while s < CHUNK:
        cf = cf + jnp.where(row >= s, pltpu.roll(cf, s, axis=0), 0.0)
        cb = cb + jnp.where(row < CHUNK - s, pltpu.roll(cb, tm - s, axis=0), 0.0)
        s *= 2
    n_beta = N_DIRS * DN_HEADS
    gate_o[...] = jnp.where(lane < n_beta, beta, jnp.where(lane < n_beta + DN_HEADS, cf, cb))


def _proj(x, gain, mod, w_perm, ccs, alog, dtb, mod_row, tm, n_ctx_tiles, f_w, qkvz_w, cv_w):
    x_specs, x_args = _stream_specs(x, tm, n_ctx_tiles)
    t = sum(a.shape[0] for a in x_args)
    d, n = w_perm.shape
    kern = functools.partial(_proj_kernel, n_x=len(x_args), n_ctx_tiles=n_ctx_tiles, d=d, f_w=f_w,
                             qkvz_w=qkvz_w, cv_w=cv_w)
    row = lambda i: (i, 0)
    const = lambda i: (0, 0)
    return pl.pallas_call(
        kern,
        out_shape=(jax.ShapeDtypeStruct((t, f_w), BF16), jax.ShapeDtypeStruct((t, f_w), BF16),
                   jax.ShapeDtypeStruct((t, qkvz_w), F32), jax.ShapeDtypeStruct((t, cv_w), F32),
                   jax.ShapeDtypeStruct((t, LANE), F32)),
        grid=(t // tm,),
        in_specs=x_specs + [
            pl.BlockSpec((1, d), const),
            pl.BlockSpec((None, 1, N_MOD * d), lambda i: (mod_row(i), 0, 0)),
            pl.BlockSpec((d, n), const),
            pl.BlockSpec(ccs.shape, const),
            pl.BlockSpec((1, LANE), const),
            pl.BlockSpec((1, LANE), const)],
        out_specs=(pl.BlockSpec((tm, f_w), row), pl.BlockSpec((tm, f_w), row),
                   pl.BlockSpec((tm, qkvz_w), row), pl.BlockSpec((tm, cv_w), row),
                   pl.BlockSpec((tm, LANE), row)),
        compiler_params=_cparams(("parallel",)),
        name="proj",
    )(*x_args, gain, mod, w_perm, ccs, alog, dtb)


def _fourier_kernel(xc_ref, xs_ref, cn_ref, sn_ref, o_ref, *, scale):
    y = _dot(cn_ref[...], xc_ref[...]) - _dot(sn_ref[...], xs_ref[...])
    o_ref[...] = (y * scale).astype(o_ref.dtype)


def _dft_tables(n):
    j = np.arange(n, dtype=np.int64)
    ang = (2.0 * np.pi / n) * ((j[:, None] * j[None, :]) % n).astype(np.float64)
    return np.cos(ang), np.sin(ang)


def _fourier(xc, xs, seq_len, n_batch, row_off, f_gw):
    f_w = xc.shape[1]
    cn, sn = _dft_tables(seq_len)
    cn = jnp.asarray(cn, F32).astype(BF16)
    sn = jnp.asarray(sn, F32).astype(BF16)
    tr = _pick_tile(1024, seq_len)
    nt = seq_len // tr
    blk0 = row_off // seq_len
    kern = functools.partial(_fourier_kernel, scale=1.0 / math.sqrt(seq_len * f_gw))
    return pl.pallas_call(
        kern,
        out_shape=jax.ShapeDtypeStruct((n_batch * seq_len, f_w), BF16),
        grid=(nt, n_batch),
        in_specs=[pl.BlockSpec((seq_len, f_w), lambda i, b: (blk0 + b, 0)),
                  pl.BlockSpec((seq_len, f_w), lambda i, b: (blk0 + b, 0)),
                  pl.BlockSpec((tr, seq_len), lambda i, b: (i, 0)),
                  pl.BlockSpec((tr, seq_len), lambda i, b: (i, 0))],
        out_specs=pl.BlockSpec((tr, f_w), lambda i, b: (b * nt + i, 0)),
        compiler_params=_cparams(("parallel", "parallel")),
        name="fourier",
    )(xc, xs, cn, sn)


def _inverse_consts():
    ri = lax.broadcasted_iota(jnp.int32, (CHUNK, CHUNK), 0)
    ci = lax.broadcasted_iota(jnp.int32, (CHUNK, CHUNK), 1)
    eye = (ri == ci).astype(F32)
    same32 = (ri // 32) == (ci // 32)
    m16 = ((ri // 16) == (ci // 16)).astype(F32)
    return eye, m16, (same32.astype(F32) - m16, 1.0 - same32.astype(F32))


def _unit_tri_inverse(mats, consts):
    eye, m16, offs = consts
    ps = [-(a * m16) for a in mats]
    ts = [eye + p for p in ps]
    for _ in range(3):
        ps = [_mm(p, p) for p in ps]
        ts = [t + _mm(t, p) for t, p in zip(ts, ps)]
    for off in offs:
        us = [_mm(t, a * off) for t, a in zip(ts, mats)]
        ts = [t - _mm(u, t) for t, u in zip(ts, us)]
    return ts


def _delta_kernel(*refs, seq_len, dk, zero_init, write_state):
    (q_ref, k_ref, v_ref, z_ref, gcol_ref, grow_ref,
     wq_ref, wk_ref, wv_ref, bq_ref, bk_ref, bv_ref, og_ref) = refs[:13]
    pos = 13
    if not zero_init:
        s0_ref = refs[pos]
        pos += 1
    y_ref = refs[pos]
    pos += 1
    if write_state:
        sfin_ref = refs[pos]
        pos += 1
    qs, ks, vs, o_s, st_s, pq_s, n_s = refs[pos:]

    n_chunks = seq_len // CHUNK
    n_heads = qs.shape[1] // dk
    heads = range(n_heads)
    rows = lax.broadcasted_iota(jnp.int32, qs.shape, 0)

    def hcols(h):
        return slice(h * dk, (h + 1) * dk)

    def conv_silu(x_ref, w_ref, b_ref):
        x = x_ref[...]
        width = w_ref.shape[0]
        pad = (width - 1) // 2
        acc = jnp.zeros_like(x) + b_ref[...]
        for s in range(width):
            o = s - pad
            if o == 0:
                xs = x
            else:
                xs = pltpu.roll(x, (-o) % seq_len, axis=0)
                xs = jnp.where((rows + o >= 0) & (rows + o < seq_len), xs, 0.0)
            acc = acc + xs * w_ref[s:s + 1, :]
        return _silu(acc)

    def l2norm(x):
        return x * lax.rsqrt(jnp.sum(x * x, axis=-1, keepdims=True) + EPS)

    qc = conv_silu(q_ref, wq_ref, bq_ref)
    kc = conv_silu(k_ref, wk_ref, bk_ref)
    for h in heads:
        qs[:, hcols(h)] = l2norm(qc[:, hcols(h)]) * (dk ** -0.5)
        ks[:, hcols(h)] = l2norm(kc[:, hcols(h)])
    vs[...] = conv_silu(v_ref, wv_ref, bv_ref)
    if zero_init:
        st_s[...] = jnp.zeros_like(st_s)
    else:
        st_s[...] = s0_ref[...]

    ri = lax.broadcasted_iota(jnp.int32, (CHUNK, CHUNK), 0)
    ci = lax.broadcasted_iota(jnp.int32, (CHUNK, CHUNK), 1)
    incl = (ri >= ci, ri <= ci)
    unroll = math.gcd(PREP_UNROLL // n_heads, n_chunks)
    inv_consts = _inverse_consts()
    strict = (ri > ci, ri < ci)
    n_beta = N_DIRS

    last_row = (CHUNK - 1, 0)

    def prep_group(first_chunk, chunks):
        blocks = [(c, h) for c in chunks for h in heads]
        chains = [(c, h, d) for c, h in blocks for d in range(N_DIRS)]
        cidx = {c: first_chunk + c for c in chunks}
        r0 = {c: pl.multiple_of(cidx[c] * CHUNK, CHUNK) for c in chunks}
        q = {(c, h): qs[pl.ds(r0[c], CHUNK), hcols(h)] for c, h in blocks}
        k = {(c, h): ks[pl.ds(r0[c], CHUNK), hcols(h)] for c, h in blocks}
        v = {(c, h): vs[pl.ds(r0[c], CHUNK), hcols(h)] for c, h in blocks}
        gc4 = {(c, h): gcol_ref[h, pl.ds(r0[c], CHUNK), :] for c, h in blocks}
        gr4 = {(c, h): grow_ref[h, cidx[c]] for c, h in blocks}
        kq = {}
        for ch in blocks:
            k16 = k[ch].astype(BF16)
            kq[ch] = _mm_nt(jnp.concatenate([k16, q[ch].astype(BF16)], axis=0), k16)
        beta, gcl, decay, a = {}, {}, {}, []
        for c, h, d in chains:
            beta[c, h, d] = gc4[c, h][:, d:d + 1]
            gcl[c, h, d] = gc4[c, h][:, n_beta + d:n_beta + d + 1]
            grw = gr4[c, h][n_beta + d:n_beta + d + 1, :]
            decay[c, h, d] = jnp.exp(jnp.where(incl[d], gcl[c, h, d] - grw, -jnp.inf))
            a.append(jnp.where(strict[d], beta[c, h, d] * kq[c, h][:CHUNK] * decay[c, h, d], 0.0))
        t = dict(zip(chains, _unit_tri_inverse(a, inv_consts)))
        eg = {chd: jnp.exp(gcl[chd]) for chd in chains}
        sol = {(c, h, d): _mm(t[c, h, d], jnp.concatenate(
            [v[c, h] * beta[c, h, d], k[c, h] * (beta[c, h, d] * eg[c, h, d])], axis=1))
            for c, h, d in chains}
        aw = {(c, h, d): _mm(kq[c, h][CHUNK:] * decay[c, h, d], sol[c, h, d])
              for c, h, d in chains}
        kuw = {}
        for c, h, d in chains:
            g_last = gcl[c, h, d][last_row[d]:last_row[d] + 1, :]
            kuw[c, h, d] = _mm_tn(k[c, h] * jnp.exp(g_last - gcl[c, h, d]), sol[c, h, d])
        for c, h, d in chains:
            pq_s[d, h, cidx[c]] = jnp.concatenate(
                [kuw[c, h, d][:, dk:], q[c, h] * eg[c, h, d] - aw[c, h, d][:, dk:]], axis=0).astype(BF16)
            n_s[d, h, cidx[c]] = kuw[c, h, d][:, :dk]
        for c, h in blocks:
            o_s[pl.ds(r0[c], CHUNK), hcols(h)] = aw[c, h, 0][:, :dk] + aw[c, h, 1][:, :dk]

    def prep_body(i, carry):
        prep_group(i * unroll, range(unroll))
        return carry

    lax.fori_loop(0, n_chunks // unroll, prep_body, 0)

    def scan_body(n, carry):
        for h in heads:
            for d in range(N_DIRS):
                c = n if d == 0 else n_chunks - 1 - n
                r0 = pl.multiple_of(c * CHUNK, CHUNK)
                g_last = grow_ref[h, c][n_beta + d:n_beta + d + 1, last_row[d]:last_row[d] + 1]
                s = st_s[d, h]
                r = _dot(pq_s[d, h, c], s.astype(BF16))
                st_s[d, h] = s * jnp.exp(g_last) + n_s[d, h, c] - r[:dk]
                o_s[pl.ds(r0, CHUNK), hcols(h)] += r[dk:]
        return carry

    lax.fori_loop(0, n_chunks, scan_body, 0)

    o = o_s[...]
    z = _silu(z_ref[...])
    for h in heads:
        y_ref[:, hcols(h)] = (_rms(o[:, hcols(h)]) * og_ref[...] * z[:, hcols(h)]).astype(y_ref.dtype)
    if write_state:
        sfin_ref[...] = st_s[...]


def _delta(qkvz, gcol, grow, sc_w, sc_b, o_g, s0, layer, seq_len, n_batch, row_off, dk, write_state):
    nh = DN_HEADS
    blk0 = row_off // seq_len
    n_chunks = seq_len // CHUNK
    zero_init = s0 is None
    kern = functools.partial(_delta_kernel, seq_len=seq_len, dk=dk, zero_init=zero_init,
                             write_state=write_state)

    def vmem_bytes(hp):
        rows = seq_len * hp * dk * 4
        per_chunk = N_DIRS * hp * n_chunks * dk * ((dk + CHUNK) * 2 + dk * 4)
        gate_cols = 2 * hp * seq_len * LANE * 4
        return 2 * 4 * rows + 4 * rows + per_chunk + gate_cols

    hp = max(h for h in (1, 2, 4) if nh % h == 0 and (h == 1 or vmem_bytes(h) <= DELTA_VMEM_BUDGET))
    gw = hp * dk
    groups = nh // hp

    def col(group):
        return pl.BlockSpec((seq_len, gw), lambda b, g: (blk0 + b, group * groups + g))

    def wspec(group, rows):
        return pl.BlockSpec((rows, gw), lambda b, g: (0, group * groups + g))

    width = sc_w.shape[0]
    in_specs = [col(0), col(1), col(2), col(3),
                pl.BlockSpec((hp, seq_len, 2 * N_DIRS), lambda b, g: (g, blk0 + b, 0)),
                pl.BlockSpec((hp, n_chunks, 2 * N_DIRS, CHUNK), lambda b, g: (g, blk0 + b, 0, 0)),
                wspec(0, width), wspec(1, width), wspec(2, width),
                wspec(0, 1), wspec(1, 1), wspec(2, 1),
                pl.BlockSpec((1, dk), lambda b, g: (0, 0))]
    args = [qkvz, qkvz, qkvz, qkvz, gcol, grow, sc_w, sc_w, sc_w, sc_b, sc_b, sc_b, o_g]
    if not zero_init:
        in_specs.append(pl.BlockSpec((None, None, N_DIRS, hp, dk, dk),
                                     lambda b, g: (b, layer, 0, g, 0, 0)))
        args.append(s0)
    y_shape = jax.ShapeDtypeStruct((n_batch * seq_len, nh * dk), BF16)
    y_spec = pl.BlockSpec((seq_len, gw), lambda b, g: (b, g))
    if write_state:
        out_shape = (y_shape, jax.ShapeDtypeStruct((n_batch, N_DIRS, nh, dk, dk), F32))
        out_specs = (y_spec, pl.BlockSpec((None, N_DIRS, hp, dk, dk), lambda b, g: (b, 0, g, 0, 0)))
    else:
        out_shape = y_shape
        out_specs = y_spec
    return pl.pallas_call(
        kern,
        out_shape=out_shape,
        grid=(n_batch, groups),
        in_specs=in_specs,
        out_specs=out_specs,
        scratch_shapes=[pltpu.VMEM((seq_len, gw), F32)] * 4 + [
            pltpu.VMEM((N_DIRS, hp, dk, dk), F32),
            pltpu.VMEM((N_DIRS, hp, n_chunks, dk + CHUNK, dk), BF16),
            pltpu.VMEM((N_DIRS, hp, n_chunks, dk, dk), F32)],
        compiler_params=_cparams(("parallel", "parallel")),
        name="delta",
    )(*args)


def _conformer_kernel(x_ref, w_ref, b_ref, lg_ref, lb_ref, o_ref, pad_s, *, n_seg, seg_blocks, halo):
    c = o_ref.shape[-1]
    width = w_ref.shape[0]
    pad = (width - 1) // 2
    seg_len = seg_blocks * GRID_W
    pad_s[:, 0:halo, :] = jnp.zeros((n_seg, halo, c), F32)
    pad_s[:, halo + seg_len:2 * halo + seg_len, :] = jnp.zeros((n_seg, halo, c), F32)
    for j in range(seg_blocks):
        if seg_blocks == 1:
            x = x_ref[...]
            pad_s[:, halo:halo + GRID_W, :] = x[..., :c] * _sigmoid(x[..., c:])
        else:
            x = x_ref[j]
            pad_s[0, halo + j * GRID_W:halo + (j + 1) * GRID_W, :] = x[:, :c] * _sigmoid(x[:, c:])

    win = GRID_W + 2 * halo

    def seg_body(r, carry):
        for j in range(seg_blocks):
            x = pad_s[r, j * GRID_W:j * GRID_W + win, :]
            shifted = [x] + [pltpu.roll(x, win - b, axis=0) for b in range(1, SUBLANE)]
            acc = jnp.zeros((GRID_W, c), F32) + b_ref[...]
            for s in range(width):
                o = halo - pad + s
                a0 = (o // SUBLANE) * SUBLANE
                acc = acc + shifted[o % SUBLANE][a0:a0 + GRID_W, :] * w_ref[s:s + 1, :]
            mu = jnp.mean(acc, axis=-1, keepdims=True)
            xc = acc - mu
            var = jnp.mean(xc * xc, axis=-1, keepdims=True)
            y = xc * lax.rsqrt(var + EPS) * lg_ref[...] + lb_ref[...]
            o_ref[r * seg_blocks + j] = _silu(y).astype(o_ref.dtype)
        return carry

    lax.fori_loop(0, n_seg, seg_body, 0)


def _conformer(cv, dw_w, dw_b, ln_g, ln_b, seq_len, n_batch, row_off, latent):
    t, c2 = cv.shape
    c = c2 // 2
    cv3 = cv.reshape(t // GRID_W, GRID_W, c2)
    blocks = seq_len // GRID_W
    n_seg, seg_blocks = (blocks, 1) if latent else (1, blocks)
    halo = 2 * SUBLANE
    assert (dw_w.shape[0] - 1) // 2 <= halo
    blk0 = row_off // seq_len
    kern = functools.partial(_conformer_kernel, n_seg=n_seg, seg_blocks=seg_blocks, halo=halo)
    const = lambda b: (0, 0)
    out = pl.pallas_call(
        kern,
        out_shape=jax.ShapeDtypeStruct((n_batch * blocks, GRID_W, c), BF16),
        grid=(n_batch,),
        in_specs=[pl.BlockSpec((blocks, GRID_W, c2), lambda b: (blk0 + b, 0, 0)),
                  pl.BlockSpec(dw_w.shape, const),
                  pl.BlockSpec((1, c), const), pl.BlockSpec((1, c), const), pl.BlockSpec((1, c), const)],
        out_specs=pl.BlockSpec((blocks, GRID_W, c), lambda b: (b, 0, 0)),
        scratch_shapes=[pltpu.VMEM((n_seg, seg_blocks * GRID_W + 2 * halo, c), F32)],
        compiler_params=_cparams(("parallel",)),
        name="conformer",
    )(cv3, dw_w, dw_b, ln_g, ln_b)
    return out.reshape(n_batch * seq_len, c)


def _outproj_kernel(*refs, n_x, d, n_ctx_tiles):
    yfc, yfl, ydc, ydl, ycc, ycl, w_ref, mod_ref, o_ref = refs[n_x:]
    is_ctx = pl.program_id(0) < n_ctx_tiles
    y = None
    k0 = 0
    for a_c, a_l in ((yfc, yfl), (ydc, ydl), (ycc, ycl)):
        kw = a_c.shape[1]
        a = jnp.where(is_ctx, a_c[...], a_l[...])
        part = _dot(a, w_ref[k0:k0 + kw, :])
        y = part if y is None else y + part
        k0 += kw
    o_ref[...] = _read_stream(refs[:n_x], n_ctx_tiles) + mod_ref[:, 2 * d:3 * d] * y


def _outproj(parts, w_out, x, mod, mod_row, tm, n_ctx_tiles):
    in_specs, args = _stream_specs(x, tm, n_ctx_tiles)
    t = sum(a.shape[0] for a in args)
    d = w_out.shape[1]
    kern = functools.partial(_outproj_kernel, n_x=len(args), d=d, n_ctx_tiles=n_ctx_tiles)
    for a_c, a_l in parts:
        kw = a_c.shape[1]
        in_specs.append(pl.BlockSpec((tm, kw), lambda i: (jnp.minimum(i, n_ctx_tiles - 1), 0)))
        in_specs.append(pl.BlockSpec((tm, kw), lambda i: (jnp.maximum(i - n_ctx_tiles, 0), 0)))
        args += [a_c, a_l]
    in_specs += [pl.BlockSpec(w_out.shape, lambda i: (0, 0)),
                 pl.BlockSpec((None, 1, N_MOD * d), lambda i: (mod_row(i), 0, 0))]
    args += [w_out, mod]
    return pl.pallas_call(
        kern,
        out_shape=jax.ShapeDtypeStruct((t, d), F32),
        grid=(t // tm,),
        in_specs=in_specs,
        out_specs=pl.BlockSpec((tm, d), lambda i: (i, 0)),
        compiler_params=_cparams(("parallel",)),
        name="outproj",
    )(*args)


def _ffn_kernel(*refs, d, final_norm):
    x_ref, g_ref, mod_ref, wg_ref, wu_ref, wd_ref = refs[:6]
    fg_ref = refs[6] if final_norm else None
    o_ref, h_s, acc_s = refs[-3:]
    f = pl.program_id(1)

    @pl.when(f == 0)
    def _():
        mod = mod_ref[...]
        h = _rms(x_ref[...]) * g_ref[...]
        h = h * (1.0 + mod[:, 4 * d:5 * d]) + mod[:, 3 * d:4 * d]
        h_s[...] = h.astype(BF16)
        acc_s[...] = jnp.zeros_like(acc_s)

    h = h_s[...]
    a = _silu(_dot(h, wg_ref[...].astype(BF16))) * _dot(h, wu_ref[...].astype(BF16))
    acc_s[...] += _dot(a.astype(BF16), wd_ref[...].astype(BF16))

    @pl.when(f == pl.num_programs(1) - 1)
    def _():
        o = x_ref[...] + mod_ref[:, 5 * d:6 * d] * acc_s[...]
        if final_norm:
            o = _rms(o) * fg_ref[...]
        o_ref[...] = o


def _ffn(x, gain, mod, mod_row, wg, wu, wd, final_gain, tm, tf):
    t, d = x.shape
    ff = wg.shape[1]
    final_norm = final_gain is not None
    kern = functools.partial(_ffn_kernel, d=d, final_norm=final_norm)
    row = lambda i, f: (i, 0)
    const = lambda i, f: (0, 0)
    in_specs = [pl.BlockSpec((tm, d), row),
                pl.BlockSpec((1, d), const),
                pl.BlockSpec((None, 1, N_MOD * d), lambda i, f: (mod_row(i), 0, 0)),
                pl.BlockSpec((d, tf), lambda i, f: (0, f)),
                pl.BlockSpec((d, tf), lambda i, f: (0, f)),
                pl.BlockSpec((tf, d), lambda i, f: (f, 0))]
    args = [x, gain, mod, wg, wu, wd]
    if final_norm:
        in_specs.append(pl.BlockSpec((1, d), const))
        args.append(final_gain)
    return pl.pallas_call(
        kern,
        out_shape=jax.ShapeDtypeStruct((t, d), F32),
        grid=(t // tm, ff // tf),
        in_specs=in_specs,
        out_specs=pl.BlockSpec((tm, d), row),
        scratch_shapes=[pltpu.VMEM((tm, d), BF16), pltpu.VMEM((tm, d), F32)],
        compiler_params=_cparams(("parallel", "arbitrary")),
        name="ffn",
    )(*args)


INFO_E, INFO_W, INFO_R = 0, 2, 4
MOE_TILE = 1024
MOE_SUB = 256


def _route_kernel(x_ref, g_ref, mod_ref, router_ref, h_o, info_o, cnt_o, cnt_s, *, d, n_exp):
    i = pl.program_id(0)

    @pl.when(i == 0)
    def _():
        cnt_s[...] = jnp.zeros_like(cnt_s)

    mod = mod_ref[...]
    h = _rms(x_ref[...]) * g_ref[...]
    h = h * (1.0 + mod[:, 4 * d:5 * d]) + mod[:, 3 * d:4 * d]
    h_o[...] = h

    logits = _mm3(h, router_ref[...])
    tm = logits.shape[0]
    lane = lax.broadcasted_iota(jnp.int32, logits.shape, 1)
    lg = jnp.where(lane < n_exp, logits, -jnp.inf)
    m1 = jnp.max(lg, axis=-1, keepdims=True)
    i1 = jnp.min(jnp.where(lg == m1, lane, LANE), axis=-1, keepdims=True)
    lg2 = jnp.where(lane == i1, -jnp.inf, lg)
    m2 = jnp.max(lg2, axis=-1, keepdims=True)
    i2 = jnp.min(jnp.where(lg2 == m2, lane, LANE), axis=-1, keepdims=True)
    w1 = 1.0 / (1.0 + jnp.exp(m2 - m1))

    sel1 = lane == i1
    sel2 = lane == i2
    member = jnp.where(sel1 | sel2, 1.0, 0.0)
    ri = lax.broadcasted_iota(jnp.int32, (tm, tm), 0)
    ci = lax.broadcasted_iota(jnp.int32, (tm, tm), 1)
    before = jnp.where(ri > ci, 1.0, 0.0)
    rank = cnt_s[...] + _mm(before, member)
    r1 = jnp.sum(jnp.where(sel1, rank, 0.0), axis=-1, keepdims=True)
    r2 = jnp.sum(jnp.where(sel2, rank, 0.0), axis=-1, keepdims=True)
    cnt_s[...] += jnp.sum(member, axis=0, keepdims=True)
    cnt_o[...] = jnp.broadcast_to(cnt_s[...], cnt_o.shape)

    info = jnp.zeros(logits.shape, F32)
    for ln, val in ((INFO_E, i1.astype(F32)), (INFO_E + 1, i2.astype(F32)), (INFO_W, w1),
                    (INFO_W + 1, 1.0 - w1), (INFO_R, r1), (INFO_R + 1, r2)):
        info = jnp.where(lane == ln, val, info)
    info_o[...] = info


def _route(x, gain, mod, mod_row, router, tm):
    t, d = x.shape
    n_exp = router.shape[1]
    router_p = jnp.zeros((d, LANE), F32).at[:, :n_exp].set(router)
    kern = functools.partial(_route_kernel, d=d, n_exp=n_exp)
    return pl.pallas_call(
        kern,
        out_shape=(jax.ShapeDtypeStruct((t, d), F32), jax.ShapeDtypeStruct((t, LANE), F32),
                   jax.ShapeDtypeStruct((SUBLANE, LANE), F32)),
        grid=(t // tm,),
        in_specs=[pl.BlockSpec((tm, d), lambda i: (i, 0)),
                  pl.BlockSpec((1, d), lambda i: (0, 0)),
                  pl.BlockSpec((None, 1, N_MOD * d), lambda i: (mod_row(i), 0, 0)),
                  pl.BlockSpec((d, LANE), lambda i: (0, 0))],
        out_specs=(pl.BlockSpec((tm, d), lambda i: (i, 0)), pl.BlockSpec((tm, LANE), lambda i: (i, 0)),
                   pl.BlockSpec((SUBLANE, LANE), lambda i: (0, 0))),
        scratch_shapes=[pltpu.VMEM((1, LANE), F32)],
        compiler_params=_cparams(("arbitrary",)),
        name="moe_route",
    )(x, gain, mod, router_p)


def _row_copy(src_hbm, row, dst, slot, sem):
    return pltpu.make_async_copy(src_hbm.at[pl.ds(row, 1), :], dst.at[pl.ds(slot, 1), :], sem)


def _experts_kernel(expert_sm, nvalid_sm, tok_ref, tok_next_ref, h_hbm, wg_ref, wu_ref, wd_ref, y_ref,
                    hbuf, h16, acc, wg16, wu16, wd16, sems, *, sub, n_f):
    del expert_sm
    i = pl.program_id(0)
    j = pl.program_id(1)
    n_i = pl.num_programs(0)
    tm = h16.shape[0]
    slot = i % 2
    nxt = 1 - slot
    nv = nvalid_sm[i]
    n_sub = (nv + sub - 1) // sub
    has_next = jnp.where(i + 1 < n_i, nvalid_sm[jnp.minimum(i + 1, n_i - 1)], 0) > 0
    per_step = -(-tm // n_f)
    base = j * per_step

    def start_row(tok, r, buf):
        _row_copy(h_hbm, tok[0, r], hbuf.at[buf], r, sems.at[buf]).start()

    def fetch_loop(tok, lo, hi, buf):
        def issue(r, carry):
            start_row(tok, r, buf)
            return carry

        lax.fori_loop(lo, hi, issue, 0)

    @pl.when(j == 0)
    def _():
        @pl.when((i == 0) & (nv > 0))
        def _():
            fetch_loop(tok_ref, 0, tm, 0)

        acc[...] = jnp.zeros_like(acc)

        @pl.when(nv > 0)
        def _():
            pltpu.make_async_copy(h_hbm.at[pl.ds(0, tm), :], hbuf.at[slot], sems.at[slot]).wait()

            def cast(s, carry):
                rows = pl.ds(pl.multiple_of(s * sub, sub), sub)
                h16[rows, :] = hbuf[slot, rows, :].astype(BF16)
                return carry

            lax.fori_loop(0, n_sub, cast, 0)

    fast = (n_sub == tm // sub) & has_next

    @pl.when(fast)
    def _():
        n_safe = tm - (n_f - 1) * per_step
        for r in range(min(per_step, n_safe)):
            start_row(tok_next_ref, base + r, nxt)
        h = h16[...]
        a = _silu(_dot(h, wg_ref[...].astype(BF16))) * _dot(h, wu_ref[...].astype(BF16))
        acc[...] += _dot(a.astype(BF16), wd_ref[...].astype(BF16))
        for r in range(n_safe, per_step):
            pl.when(base + r < tm)(functools.partial(start_row, tok_next_ref, base + r, nxt))

    @pl.when(jnp.logical_not(fast))
    def _():
        @pl.when(has_next)
        def _():
            fetch_loop(tok_next_ref, base, jnp.minimum(base + per_step, tm), nxt)

        @pl.when(n_sub > 0)
        def _():
            wg16[...] = wg_ref[...].astype(BF16)
            wu16[...] = wu_ref[...].astype(BF16)
            wd16[...] = wd_ref[...].astype(BF16)

            def block(s, carry):
                rows = pl.ds(pl.multiple_of(s * sub, sub), sub)
                h = h16[rows, :]
                a = _silu(_dot(h, wg16[...])) * _dot(h, wu16[...])
                acc[rows, :] += _dot(a.astype(BF16), wd16[...])
                return carry

            lax.fori_loop(0, n_sub, block, 0)

    @pl.when(j == n_f - 1)
    def _():
        y_ref[...] = acc[...]


def _experts(h, src_tok, tile_expert, tile_nvalid, wg, wu, wd, tm, tf, sub):
    t, d = h.shape
    n_exp, _, ff = wg.shape
    n_tiles = tile_expert.shape[0]
    n_f = ff // tf

    def w_idx(i, j, expert_sm, nvalid_sm):
        return expert_sm[i], jnp.where(nvalid_sm[i] > 0, j, n_f - 1)

    def wgu_map(i, j, expert_sm, nvalid_sm):
        e, jj = w_idx(i, j, expert_sm, nvalid_sm)
        return e, 0, jj

    def wd_map(i, j, expert_sm, nvalid_sm):
        e, jj = w_idx(i, j, expert_sm, nvalid_sm)
        return e, jj, 0

    kern = functools.partial(_experts_kernel, sub=sub, n_f=n_f)
    tok = src_tok.reshape(n_tiles, 1, tm)
    grid_spec = pltpu.PrefetchScalarGridSpec(
        num_scalar_prefetch=2,
        grid=(n_tiles, n_f),
        in_specs=[pl.BlockSpec((None, 1, tm), lambda i, j, *_: (i, 0, 0), memory_space=pltpu.SMEM),
                  pl.BlockSpec((None, 1, tm), lambda i, j, *_: (jnp.minimum(i + 1, n_tiles - 1), 0, 0),
                               memory_space=pltpu.SMEM),
                  pl.BlockSpec(memory_space=pl.ANY),
                  pl.BlockSpec((None, d, tf), wgu_map),
                  pl.BlockSpec((None, d, tf), wgu_map),
                  pl.BlockSpec((None, tf, d), wd_map)],
        out_specs=pl.BlockSpec((tm, d), lambda i, j, *_: (i, 0)),
        scratch_shapes=[pltpu.VMEM((2, tm, d), F32), pltpu.VMEM((tm, d), BF16), pltpu.VMEM((tm, d), F32),
                        pltpu.VMEM((d, tf), BF16), pltpu.VMEM((d, tf), BF16), pltpu.VMEM((tf, d), BF16),
                        pltpu.SemaphoreType.DMA((2,))])
    return pl.pallas_call(
        kern,
        out_shape=jax.ShapeDtypeStruct((n_tiles * tm, d), F32),
        grid_spec=grid_spec,
        compiler_params=_cparams(("arbitrary", "arbitrary")),
        name="moe_experts",
    )(tile_expert, tile_nvalid, tok, tok, h, wg, wu, wd)


def _combine_kernel(*refs, d, final_norm, split_tiles):
    pos_ref, pos_next_ref, info_ref, x_ref, mod_ref = refs[:5]
    p = 5
    if final_norm:
        fg_ref = refs[p]
        p += 1
    y_hbm = refs[p]
    o_refs = refs[p + 1:-2]
    ybuf, sems = refs[-2:]
    tm = x_ref.shape[0]
    i = pl.program_id(0)
    slot = i % 2

    def fetch(pos, buf):
        def issue(r, carry):
            for k in range(TOP_K):
                _row_copy(y_hbm, pos[0, k * tm + r], ybuf.at[buf, k], r, sems.at[buf]).start()
            return carry

        lax.fori_loop(0, tm, issue, 0, unroll=8)

    @pl.when(i == 0)
    def _():
        fetch(pos_ref, 0)

    @pl.when(i + 1 < pl.num_programs(0))
    def _():
        fetch(pos_next_ref, 1 - slot)

    for k in range(TOP_K):
        pltpu.make_async_copy(y_hbm.at[pl.ds(0, tm), :], ybuf.at[slot, k], sems.at[slot]).wait()

    info = info_ref[...]
    f = None
    for k in range(TOP_K):
        term = info[:, INFO_W + k:INFO_W + k + 1] * ybuf[slot, k]
        f = term if f is None else f + term
    o = x_ref[...] + mod_ref[:, 5 * d:6 * d] * f
    if final_norm:
        o = _rms(o) * fg_ref[...]
    if split_tiles is None:
        o_refs[0][...] = o
    else:
        @pl.when(i < split_tiles)
        def _():
            o_refs[0][...] = o

        @pl.when(i >= split_tiles)
        def _():
            o_refs[1][...] = o


def _combine(x, info, pos, y, mod, mod_row, final_gain, tm, split_rows):
    t, d = x.shape
    final_norm = final_gain is not None
    split_tiles = None if split_rows is None else split_rows // tm
    kern = functools.partial(_combine_kernel, d=d, final_norm=final_norm, split_tiles=split_tiles)
    n_tiles = t // tm
    if split_tiles is None:
        out_shape = jax.ShapeDtypeStruct((t, d), F32)
        out_specs = pl.BlockSpec((tm, d), lambda i: (i, 0))
    else:
        out_shape = (jax.ShapeDtypeStruct((split_rows, d), F32), jax.ShapeDtypeStruct((t - split_rows, d), F32))
        out_specs = (pl.BlockSpec((tm, d), lambda i: (jnp.minimum(i, split_tiles - 1), 0)),
                     pl.BlockSpec((tm, d), lambda i: (jnp.maximum(i - split_tiles, 0), 0)))
    in_specs = [pl.BlockSpec((None, 1, TOP_K * tm), lambda i: (i, 0, 0), memory_space=pltpu.SMEM),
                pl.BlockSpec((None, 1, TOP_K * tm), lambda i: (jnp.minimum(i + 1, n_tiles - 1), 0, 0),
                             memory_space=pltpu.SMEM),
                pl.BlockSpec((tm, LANE), lambda i: (i, 0)),
                pl.BlockSpec((tm, d), lambda i: (i, 0)),
                pl.BlockSpec((None, 1, N_MOD * d), lambda i: (mod_row(i), 0, 0))]
    args = [pos, pos, info, x, mod]
    if final_norm:
        in_specs.append(pl.BlockSpec((1, d), lambda i: (0, 0)))
        args.append(final_gain)
    in_specs.append(pl.BlockSpec(memory_space=pl.ANY))
    args.append(y)
    return pl.pallas_call(
        kern,
        out_shape=out_shape,
        grid=(t // tm,),
        in_specs=in_specs,
        out_specs=out_specs,
        scratch_shapes=[pltpu.VMEM((2, TOP_K, tm, d), F32), pltpu.SemaphoreType.DMA((2,))],
        compiler_params=_cparams(("arbitrary",)),
        name="moe_combine",
    )(*args)


def _moe(x, gain, mod, mod_row_fn, router, wg, wu, wd, final_gain, split_rows, tm_route, tm_exp, tm_comb, tf, sub):
    t, d = x.shape
    n_exp = wg.shape[0]
    h, info, cnt = _route(x, gain, mod, mod_row_fn(tm_route), router, tm_route)

    counts = cnt[0, :n_exp].astype(jnp.int32)
    tiles_per = (counts + tm_exp - 1) // tm_exp
    tile_end = jnp.cumsum(tiles_per)
    tile_start = tile_end - tiles_per
    n_tiles = (TOP_K * t) // tm_exp + n_exp
    tile_id = jnp.arange(n_tiles, dtype=jnp.int32)
    tile_expert = jnp.minimum(jnp.sum(tile_id[:, None] >= tile_end[None, :], axis=1), n_exp - 1).astype(jnp.int32)
    tile_nvalid = jnp.where(tile_id < tile_end[n_exp - 1],
                            jnp.clip(counts[tile_expert] - (tile_id - tile_start[tile_expert]) * tm_exp, 0, tm_exp),
                            0).astype(jnp.int32)
    slot_start = tile_start * tm_exp
    experts = info[:, INFO_E:INFO_E + TOP_K].astype(jnp.int32)
    slots = slot_start[experts] + info[:, INFO_R:INFO_R + TOP_K].astype(jnp.int32)
    token = jnp.broadcast_to(jnp.arange(t, dtype=jnp.int32)[:, None], slots.shape)
    src_tok = jnp.zeros((n_tiles * tm_exp,), jnp.int32).at[slots.reshape(-1)].set(
        token.reshape(-1), unique_indices=True)

    y = _experts(h, src_tok, tile_expert, tile_nvalid, wg, wu, wd, tm_exp, tf, sub)
    pos = slots.reshape(t // tm_comb, tm_comb, TOP_K).transpose(0, 2, 1).reshape(t // tm_comb, 1, TOP_K * tm_comb)
    return _combine(x, info, pos, y, mod, mod_row_fn(tm_comb), final_gain, tm_comb, split_rows)


def _lane_row(vals, offset):
    n = vals.shape[0]
    return jnp.zeros((1, LANE), F32).at[0, offset:offset + n].set(vals.astype(F32))


def kernel(x_prompt, x_sample, state_delta, c, c_ctx, norm1, norm2, w_mod, b_mod, w_in, sc_w, sc_b, dn_a_log, dn_dt_bias, dn_norm, cv_dw_w, cv_dw_b, cv_ln_g, cv_ln_b, w_out, ffn_wg, ffn_wu, ffn_wd, moe_router, moe_wg, moe_wu, moe_wd, final_norm):
    b_ctx, l_ctx, d = x_prompt.shape
    b_lat, l_lat, _ = x_sample.shape
    depth = w_in.shape[0]
    t_ctx = b_ctx * l_ctx
    t_lat = b_lat * l_lat
    f_w = d // 4
    f_gw = f_w // F_GROUPS
    dn_w = d // 2
    dk = dn_w // DN_HEADS
    cv_w = d // 4
    qkvz_w = 4 * dn_w
    n_gate = 2 * N_DIRS * DN_HEADS
    n_exp = moe_wg.shape[1]
    assert dk == LANE and t_ctx % l_lat == 0 and b_lat + 1 <= MOD_ROWS
    assert l_ctx % CHUNK == 0 and l_lat % CHUNK == 0 and l_lat % GRID_W == 0 and CHUNK == GRID_W

    tm = _pick_tile(512, t_ctx, l_lat)
    tm_ffn = _pick_tile(1024, t_ctx, l_lat)

    def mod_row_fn(tile):
        n_ctx_tiles = t_ctx // tile
        per_seq = l_lat // tile
        return lambda i: jnp.where(i < n_ctx_tiles, 0, 1 + (i - n_ctx_tiles) // per_seq)

    x = (x_prompt.reshape(t_ctx, d), x_sample.reshape(t_lat, d))
    cvec = jnp.zeros((MOD_ROWS, d), F32).at[0].set(c_ctx).at[1:1 + b_lat].set(c)
    mods = _adaln(cvec, w_mod, b_mod).reshape(depth, MOD_ROWS, 1, N_MOD * d)

    cg, sg = _dft_tables(f_gw)
    eye_g = np.eye(F_GROUPS)
    ccs = jnp.asarray(np.concatenate([np.kron(eye_g, cg), np.kron(eye_g, sg)], axis=1), F32).astype(BF16)

    passes = ((l_ctx, b_ctx, 0, False), (l_lat, b_lat, t_ctx, True))
    o_q = f_w
    o_ba = f_w + qkvz_w
    o_cv = o_ba + n_gate
    new_states = []
    for l in range(depth):
        w = w_in[l]
        w_perm = jnp.concatenate(
            [w[:, :o_ba], w[:, o_cv:o_cv + 2 * cv_w], w[:, o_ba:o_cv], jnp.zeros((d, LANE - n_gate), F32)],
            axis=1).astype(BF16)
        alog = _lane_row(dn_a_log[l].reshape(-1), N_DIRS * DN_HEADS)
        dtb = _lane_row(dn_dt_bias[l].reshape(-1), N_DIRS * DN_HEADS)
        xc, xs, qkvz, cv, gates = _proj(x, norm1[l][None], mods[l], w_perm, ccs, alog, dtb,
                                        mod_row_fn(tm), tm, t_ctx // tm, f_w, qkvz_w, 2 * cv_w)
        g16 = gates[:, :n_gate].reshape(-1, 2 * N_DIRS, DN_HEADS)
        gcol = g16.transpose(2, 0, 1)
        grow = g16.reshape(-1, CHUNK, 2 * N_DIRS, DN_HEADS).transpose(3, 0, 2, 1)

        parts = [[], [], []]
        for seq_len, n_batch, row_off, latent in passes:
            parts[0].append(_fourier(xc, xs, seq_len, n_batch, row_off, f_gw))
            res = _delta(qkvz, gcol, grow, sc_w[l], sc_b[l][None], dn_norm[l][None],
                         state_delta if latent else None, l, seq_len, n_batch, row_off, dk,
                         write_state=not latent)
            if latent:
                parts[1].append(res)
            else:
                parts[1].append(res[0])
                new_states.append(res[1])
            parts[2].append(_conformer(cv, cv_dw_w[l], cv_dw_b[l][None], cv_ln_g[l][None],
                                       cv_ln_b[l][None], seq_len, n_batch, row_off, latent))
        x = _outproj(parts, w_out[l].astype(BF16), x, mods[l], mod_row_fn(tm), tm, t_ctx // tm)

        fg = final_norm[None] if l == depth - 1 else None
        if l % 2 == 0:
            i = l // 2
            x = _ffn(x, norm2[l][None], mods[l], mod_row_fn(tm_ffn), ffn_wg[i], ffn_wu[i], ffn_wd[i], fg,
                     tm_ffn, _pick_tile(512, ffn_wg.shape[-1]))
        else:
            i = l // 2
            x = _moe(x, norm2[l][None], mods[l], mod_row_fn, moe_router[i], moe_wg[i], moe_wu[i], moe_wd[i], fg,
                     split_rows=t_ctx if l == depth - 1 else None,
                     tm_route=tm, tm_exp=MOE_TILE, tm_comb=_pick_tile(256, t_ctx, l_lat),
                     tf=_pick_tile(512, moe_wg.shape[-1]), sub=MOE_SUB)

    y_ctx, y_lat = x if isinstance(x, tuple) else (x[:t_ctx], x[t_ctx:])
    y_prompt = y_ctx.reshape(b_ctx, l_ctx, d)
    y_sample = y_lat.reshape(b_lat, l_lat, d)
    return y_prompt, y_sample, jnp.stack(new_states, axis=1)
```

```python
import functools
import math

import numpy as np
import jax
import jax.numpy as jnp
from jax import lax
from jax.experimental import pallas as pl
from jax.experimental.pallas import tpu as pltpu

F32 = jnp.float32
BF16 = jnp.bfloat16
EPS = 1e-6

LANE = 128
SUBLANE = 8
VMEM_LIMIT = 56 * 1024 * 1024

CHUNK = 64
GRID_W = 64
F_GROUPS = 4
DN_HEADS = 4
N_DIRS = 2
DELTA_VMEM_BUDGET = VMEM_LIMIT - 12 * 1024 * 1024
FFN_TILE_MAX = 1408
PREP_UNROLL = 16
TOP_K = 2
N_MOD = 6
MOD_ROWS = 16


def _cparams(sem):
    return pltpu.CompilerParams(dimension_semantics=sem, vmem_limit_bytes=VMEM_LIMIT)


def _pick_tile(limit, *sizes):
    t = limit
    while any(s % t for s in sizes):
        t //= 2
    return t


def _dot(a, b):
    return jnp.dot(a, b, preferred_element_type=F32)


def _mm(a, b):
    return jnp.dot(a.astype(BF16), b.astype(BF16), preferred_element_type=F32)


def _mm_nt(a, b):
    return lax.dot_general(a.astype(BF16), b.astype(BF16), (((1,), (1,)), ((), ())),
                           preferred_element_type=F32)


def _mm_tn(a, b):
    return lax.dot_general(a.astype(BF16), b.astype(BF16), (((0,), (0,)), ((), ())),
                           preferred_element_type=F32)


def _split(a):
    hi = a.astype(BF16)
    lo = (a - hi.astype(F32)).astype(BF16)
    return hi, lo


def _mm3(a, b):
    ah, al = _split(a)
    bh, bl = _split(b)
    return _dot(ah, bh) + (_dot(ah, bl) + _dot(al, bh))


def _sigmoid(x):
    return 1.0 / (1.0 + jnp.exp(-x))


def _silu(x):
    return x * _sigmoid(x)


def _rms(x):
    return x * lax.rsqrt(jnp.mean(x * x, axis=-1, keepdims=True) + EPS)


def _stream_specs(x, tm, n_ctx_tiles):
    if isinstance(x, tuple):
        d = x[0].shape[1]
        return ([pl.BlockSpec((tm, d), lambda i, *_: (jnp.minimum(i, n_ctx_tiles - 1), 0)),
                 pl.BlockSpec((tm, d), lambda i, *_: (jnp.maximum(i - n_ctx_tiles, 0), 0))], list(x))
    return [pl.BlockSpec((tm, x.shape[1]), lambda i, *_: (i, 0))], [x]


def _read_stream(x_refs, n_ctx_tiles):
    if len(x_refs) == 2:
        return jnp.where(pl.program_id(0) < n_ctx_tiles, x_refs[0][...], x_refs[1][...])
    return x_refs[0][...]


def _adaln_kernel(c_ref, w_ref, b_ref, o_ref):
    o_ref[...] = _mm(_silu(c_ref[...]), w_ref[...]) + b_ref[...]


def _adaln(cvec, w_mod, b_mod):
    depth, d, n = w_mod.shape
    tn = _pick_tile(1024, n)
    return pl.pallas_call(
        _adaln_kernel,
        out_shape=jax.ShapeDtypeStruct((depth, MOD_ROWS, n), F32),
        grid=(depth, n // tn),
        in_specs=[pl.BlockSpec((MOD_ROWS, d), lambda l, j: (0, 0)),
                  pl.BlockSpec((None, d, tn), lambda l, j: (l, 0, j)),
                  pl.BlockSpec((None, 1, tn), lambda l, j: (l, 0, j))],
        out_specs=pl.BlockSpec((None, MOD_ROWS, tn), lambda l, j: (l, 0, j)),
        compiler_params=_cparams(("parallel", "parallel")),
        name="adaln",
    )(cvec, w_mod, b_mod.reshape(depth, 1, n))


def _proj_kernel(*refs, n_x, n_ctx_tiles, d, f_w, qkvz_w, cv_w):
    g_ref, mod_ref, w_ref, ccs_ref, alog_ref, dtb_ref, xc_o, xs_o, qkvz_o, cv_o, gate_o = refs[n_x:]
    x = _read_stream(refs[:n_x], n_ctx_tiles)
    mod = mod_ref[...]
    h = _rms(x) * g_ref[...]
    h = (h * (1.0 + mod[:, d:2 * d]) + mod[:, 0:d]).astype(BF16)

    xf = _dot(h, w_ref[:, 0:f_w])
    xcs = _dot(xf.astype(BF16), ccs_ref[...])
    xc_o[...] = xcs[:, :f_w].astype(xc_o.dtype)
    xs_o[...] = xcs[:, f_w:].astype(xs_o.dtype)
    o = f_w
    step = 4 * LANE
    for n0 in range(0, qkvz_w, step):
        qkvz_o[:, n0:n0 + step] = _dot(h, w_ref[:, o + n0:o + n0 + step])
    o += qkvz_w
    cv_o[...] = _dot(h, w_ref[:, o:o + cv_w])
    o += cv_w
    ba = _dot(h, w_ref[:, o:o + LANE])

    tm = ba.shape[0]
    lane = lax.broadcasted_iota(jnp.int32, ba.shape, 1)
    row = lax.broadcasted_iota(jnp.int32, ba.shape, 0) % CHUNK
    beta = _sigmoid(ba)
    t = ba + dtb_ref[...]
    softplus = jnp.maximum(t, 0.0) + jnp.log1p(jnp.exp(-jnp.abs(t)))
    g = -jnp.exp(alog_ref[...]) * softplus
    cf = g
    cb = g
    s = 1
    while s < CHUNK:
        cf = cf + jnp.where(row >= s, pltpu.roll(cf, s, axis=0), 0.0)
        cb = cb + jnp.where(row < CHUNK - s, pltpu.roll(cb, tm - s, axis=0), 0.0)
        s *= 2
    n_beta = N_DIRS * DN_HEADS
    gate_o[...] = jnp.where(lane < n_beta, beta, jnp.where(lane < n_beta + DN_HEADS, cf, cb))


def _proj(x, gain, mod, w_perm, ccs, alog, dtb, mod_row, tm, n_ctx_tiles, f_w, qkvz_w, cv_w):
    x_specs, x_args = _stream_specs(x, tm, n_ctx_tiles)
    t = sum(a.shape[0] for a in x_args)
    d, n = w_perm.shape
    kern = functools.partial(_proj_kernel, n_x=len(x_args), n_ctx_tiles=n_ctx_tiles, d=d, f_w=f_w,
                             qkvz_w=qkvz_w, cv_w=cv_w)
    row = lambda i: (i, 0)
    const = lambda i: (0, 0)
    return pl.pallas_call(
        kern,
        out_shape=(jax.ShapeDtypeStruct((t, f_w), BF16), jax.ShapeDtypeStruct((t, f_w), BF16),
                   jax.ShapeDtypeStruct((t, qkvz_w), F32), jax.ShapeDtypeStruct((t, cv_w), F32),
                   jax.ShapeDtypeStruct((t, LANE), F32)),
        grid=(t // tm,),
        in_specs=x_specs + [
            pl.BlockSpec((1, d), const),
            pl.BlockSpec((None, 1, N_MOD * d), lambda i: (mod_row(i), 0, 0)),
            pl.BlockSpec((d, n), const),
            pl.BlockSpec(ccs.shape, const),
            pl.BlockSpec((1, LANE), const),
            pl.BlockSpec((1, LANE), const)],
        out_specs=(pl.BlockSpec((tm, f_w), row), pl.BlockSpec((tm, f_w), row),
                   pl.BlockSpec((tm, qkvz_w), row), pl.BlockSpec((tm, cv_w), row),
                   pl.BlockSpec((tm, LANE), row)),
        compiler_params=_cparams(("parallel",)),
        name="proj",
    )(*x_args, gain, mod, w_perm, ccs, alog, dtb)


def _fourier_kernel(xc_ref, xs_ref, cn_ref, sn_ref, o_ref, *, scale):
    y = _dot(cn_ref[...], xc_ref[...]) - _dot(sn_ref[...], xs_ref[...])
    o_ref[...] = (y * scale).astype(o_ref.dtype)


def _dft_tables(n):
    j = np.arange(n, dtype=np.int64)
    ang = (2.0 * np.pi / n) * ((j[:, None] * j[None, :]) % n).astype(np.float64)
    return np.cos(ang), np.sin(ang)


def _fourier(xc, xs, seq_len, n_batch, row_off, f_gw):
    f_w = xc.shape[1]
    cn, sn = _dft_tables(seq_len)
    cn = jnp.asarray(cn, F32).astype(BF16)
    sn = jnp.asarray(sn, F32).astype(BF16)
    tr = _pick_tile(1024, seq_len)
    nt = seq_len // tr
    blk0 = row_off // seq_len
    kern = functools.partial(_fourier_kernel, scale=1.0 / math.sqrt(seq_len * f_gw))
    return pl.pallas_call(
        kern,
        out_shape=jax.ShapeDtypeStruct((n_batch * seq_len, f_w), BF16),
        grid=(nt, n_batch),
        in_specs=[pl.BlockSpec((seq_len, f_w), lambda i, b: (blk0 + b, 0)),
                  pl.BlockSpec((seq_len, f_w), lambda i, b: (blk0 + b, 0)),
                  pl.BlockSpec((tr, seq_len), lambda i, b: (i, 0)),
                  pl.BlockSpec((tr, seq_len), lambda i, b: (i, 0))],
        out_specs=pl.BlockSpec((tr, f_w), lambda i, b: (b * nt + i, 0)),
        compiler_params=_cparams(("parallel", "parallel")),
        name="fourier",
    )(xc, xs, cn, sn)


def _inverse_consts():
    ri = lax.broadcasted_iota(jnp.int32, (CHUNK, CHUNK), 0)
    ci = lax.broadcasted_iota(jnp.int32, (CHUNK, CHUNK), 1)
    eye = (ri == ci).astype(F32)
    same32 = (ri // 32) == (ci // 32)
    m16 = ((ri // 16) == (ci // 16)).astype(F32)
    return eye, m16, (same32.astype(F32) - m16, 1.0 - same32.astype(F32))


def _unit_tri_inverse(mats, consts):
    eye, m16, offs = consts
    ps = [-(a * m16) for a in mats]
    ts = [eye + p for p in ps]
    for _ in range(3):
        ps = [_mm(p, p) for p in ps]
        ts = [t + _mm(t, p) for t, p in zip(ts, ps)]
    for off in offs:
        us = [_mm(t, a * off) for t, a in zip(ts, mats)]
        ts = [t - _mm(u, t) for t, u in zip(ts, us)]
    return ts


def _delta_kernel(*refs, seq_len, dk, zero_init, write_state):
    (q_ref, k_ref, v_ref, z_ref, gcol_ref, grow_ref,
     wq_ref, wk_ref, wv_ref, bq_ref, bk_ref, bv_ref, og_ref) = refs[:13]
    pos = 13
    if not zero_init:
        s0_ref = refs[pos]
        pos += 1
    y_ref = refs[pos]
    pos += 1
    if write_state:
        sfin_ref = refs[pos]
        pos += 1
    qs, ks, vs, o_s, st_s, pq_s, n_s = refs[pos:]

    n_chunks = seq_len // CHUNK
    n_heads = qs.shape[1] // dk
    heads = range(n_heads)
    rows = lax.broadcasted_iota(jnp.int32, qs.shape, 0)

    def hcols(h):
        return slice(h * dk, (h + 1) * dk)

    def conv_silu(x_ref, w_ref, b_ref):
        x = x_ref[...]
        width = w_ref.shape[0]
        pad = (width - 1) // 2
        acc = jnp.zeros_like(x) + b_ref[...]
        for s in range(width):
            o = s - pad
            if o == 0:
                xs = x
            else:
                xs = pltpu.roll(x, (-o) % seq_len, axis=0)
                xs = jnp.where((rows + o >= 0) & (rows + o < seq_len), xs, 0.0)
            acc = acc + xs * w_ref[s:s + 1, :]
        return _silu(acc)

    def l2norm(x):
        return x * lax.rsqrt(jnp.sum(x * x, axis=-1, keepdims=True) + EPS)

    qc = conv_silu(q_ref, wq_ref, bq_ref)
    kc = conv_silu(k_ref, wk_ref, bk_ref)
    for h in heads:
        qs[:, hcols(h)] = l2norm(qc[:, hcols(h)]) * (dk ** -0.5)
        ks[:, hcols(h)] = l2norm(kc[:, hcols(h)])
    vs[...] = conv_silu(v_ref, wv_ref, bv_ref)
    if zero_init:
        st_s[...] = jnp.zeros_like(st_s)
    else:
        st_s[...] = s0_ref[...]

    ri = lax.broadcasted_iota(jnp.int32, (CHUNK, CHUNK), 0)
    ci = lax.broadcasted_iota(jnp.int32, (CHUNK, CHUNK), 1)
    incl = (ri >= ci, ri <= ci)
    unroll = math.gcd(PREP_UNROLL // n_heads, n_chunks)
    inv_consts = _inverse_consts()
    strict = (ri > ci, ri < ci)
    n_beta = N_DIRS

    last_row = (CHUNK - 1, 0)

    def prep_group(first_chunk, chunks):
        blocks = [(c, h) for c in chunks for h in heads]
        chains = [(c, h, d) for c, h in blocks for d in range(N_DIRS)]
        cidx = {c: first_chunk + c for c in chunks}
        r0 = {c: pl.multiple_of(cidx[c] * CHUNK, CHUNK) for c in chunks}
        q = {(c, h): qs[pl.ds(r0[c], CHUNK), hcols(h)] for c, h in blocks}
        k = {(c, h): ks[pl.ds(r0[c], CHUNK), hcols(h)] for c, h in blocks}
        v = {(c, h): vs[pl.ds(r0[c], CHUNK), hcols(h)] for c, h in blocks}
        gc4 = {(c, h): gcol_ref[h, pl.ds(r0[c], CHUNK), :] for c, h in blocks}
        gr4 = {(c, h): grow_ref[h, cidx[c]] for c, h in blocks}
        kq = {}
        for ch in blocks:
            k16 = k[ch].astype(BF16)
            kq[ch] = _mm_nt(jnp.concatenate([k16, q[ch].astype(BF16)], axis=0), k16)
        beta, gcl, decay, a = {}, {}, {}, []
        for c, h, d in chains:
            beta[c, h, d] = gc4[c, h][:, d:d + 1]
            gcl[c, h, d] = gc4[c, h][:, n_beta + d:n_beta + d + 1]
            grw = gr4[c, h][n_beta + d:n_beta + d + 1, :]
            decay[c, h, d] = jnp.exp(jnp.where(incl[d], gcl[c, h, d] - grw, -jnp.inf))
            a.append(jnp.where(strict[d], beta[c, h, d] * kq[c, h][:CHUNK] * decay[c, h, d], 0.0))
        t = dict(zip(chains, _unit_tri_inverse(a, inv_consts)))
        eg = {chd: jnp.exp(gcl[chd]) for chd in chains}
        sol = {(c, h, d): _mm(t[c, h, d], jnp.concatenate(
            [v[c, h] * beta[c, h, d], k[c, h] * (beta[c, h, d] * eg[c, h, d])], axis=1))
            for c, h, d in chains}
        aw = {(c, h, d): _mm(kq[c, h][CHUNK:] * decay[c, h, d], sol[c, h, d])
              for c, h, d in chains}
        kuw = {}
        for c, h, d in chains:
            g_last = gcl[c, h, d][last_row[d]:last_row[d] + 1, :]
            kuw[c, h, d] = _mm_tn(k[c, h] * jnp.exp(g_last - gcl[c, h, d]), sol[c, h, d])
        for c, h, d in chains:
            pq_s[d, h, cidx[c]] = jnp.concatenate(
                [kuw[c, h, d][:, dk:], q[c, h] * eg[c, h, d] - aw[c, h, d][:, dk:]], axis=0).astype(BF16)
            n_s[d, h, cidx[c]] = kuw[c, h, d][:, :dk]
        for c, h in blocks:
            o_s[pl.ds(r0[c], CHUNK), hcols(h)] = aw[c, h, 0][:, :dk] + aw[c, h, 1][:, :dk]

    def prep_body(i, carry):
        prep_group(i * unroll, range(unroll))
        return carry

    lax.fori_loop(0, n_chunks // unroll, prep_body, 0)

    def scan_body(n, carry):
        for h in heads:
            for d in range(N_DIRS):
                c = n if d == 0 else n_chunks - 1 - n
                r0 = pl.multiple_of(c * CHUNK, CHUNK)
                g_last = grow_ref[h, c][n_beta + d:n_beta + d + 1, last_row[d]:last_row[d] + 1]
                s = st_s[d, h]
                r = _dot(pq_s[d, h, c], s.astype(BF16))
                st_s[d, h] = s * jnp.exp(g_last) + n_s[d, h, c] - r[:dk]
                o_s[pl.ds(r0, CHUNK), hcols(h)] += r[dk:]
        return carry

    lax.fori_loop(0, n_chunks, scan_body, 0)

    o = o_s[...]
    z = _silu(z_ref[...])
    for h in heads:
        y_ref[:, hcols(h)] = (_rms(o[:, hcols(h)]) * og_ref[...] * z[:, hcols(h)]).astype(y_ref.dtype)
    if write_state:
        sfin_ref[...] = st_s[...]


def _delta(qkvz, gcol, grow, sc_w, sc_b, o_g, s0, layer, seq_len, n_batch, row_off, dk, write_state):
    nh = DN_HEADS
    blk0 = row_off // seq_len
    n_chunks = seq_len // CHUNK
    zero_init = s0 is None
    kern = functools.partial(_delta_kernel, seq_len=seq_len, dk=dk, zero_init=zero_init,
                             write_state=write_state)

    def vmem_bytes(hp):
        rows = seq_len * hp * dk * 4
        per_chunk = N_DIRS * hp * n_chunks * dk * ((dk + CHUNK) * 2 + dk * 4)
        gate_cols = 2 * hp * seq_len * LANE * 4
        return 2 * 4 * rows + 4 * rows + per_chunk + gate_cols

    hp = max(h for h in (1, 2, 4) if nh % h == 0 and (h == 1 or vmem_bytes(h) <= DELTA_VMEM_BUDGET))
    gw = hp * dk
    groups = nh // hp

    def col(group):
        return pl.BlockSpec((seq_len, gw), lambda b, g: (blk0 + b, group * groups + g))

    def wspec(group, rows):
        return pl.BlockSpec((rows, gw), lambda b, g: (0, group * groups + g))

    width = sc_w.shape[0]
    in_specs = [col(0), col(1), col(2), col(3),
                pl.BlockSpec((hp, seq_len, 2 * N_DIRS), lambda b, g: (g, blk0 + b, 0)),
                pl.BlockSpec((hp, n_chunks, 2 * N_DIRS, CHUNK), lambda b, g: (g, blk0 + b, 0, 0)),
                wspec(0, width), wspec(1, width), wspec(2, width),
                wspec(0, 1), wspec(1, 1), wspec(2, 1),
                pl.BlockSpec((1, dk), lambda b, g: (0, 0))]
    args = [qkvz, qkvz, qkvz, qkvz, gcol, grow, sc_w, sc_w, sc_w, sc_b, sc_b, sc_b, o_g]
    if not zero_init:
        in_specs.append(pl.BlockSpec((None, None, N_DIRS, hp, dk, dk),
                                     lambda b, g: (b, layer, 0, g, 0, 0)))
        args.append(s0)
    y_shape = jax.ShapeDtypeStruct((n_batch * seq_len, nh * dk), BF16)
    y_spec = pl.BlockSpec((seq_len, gw), lambda b, g: (b, g))
    if write_state:
        out_shape = (y_shape, jax.ShapeDtypeStruct((n_batch, N_DIRS, nh, dk, dk), F32))
        out_specs = (y_spec, pl.BlockSpec((None, N_DIRS, hp, dk, dk), lambda b, g: (b, 0, g, 0, 0)))
    else:
        out_shape = y_shape
        out_specs = y_spec
    return pl.pallas_call(
        kern,
        out_shape=out_shape,
        grid=(n_batch, groups),
        in_specs=in_specs,
        out_specs=out_specs,
        scratch_shapes=[pltpu.VMEM((seq_len, gw), F32)] * 4 + [
            pltpu.VMEM((N_DIRS, hp, dk, dk), F32),
            pltpu.VMEM((N_DIRS, hp, n_chunks, dk + CHUNK, dk), BF16),
            pltpu.VMEM((N_DIRS, hp, n_chunks, dk, dk), F32)],
        compiler_params=_cparams(("parallel", "parallel")),
        name="delta",
    )(*args)


def _conformer_kernel(x_ref, w_ref, b_ref, lg_ref, lb_ref, o_ref, pad_s, *, n_seg, seg_blocks, halo):
    c = o_ref.shape[-1]
    width = w_ref.shape[0]
    pad = (width - 1) // 2
    seg_len = seg_blocks * GRID_W
    pad_s[:, 0:halo, :] = jnp.zeros((n_seg, halo, c), F32)
    pad_s[:, halo + seg_len:2 * halo + seg_len, :] = jnp.zeros((n_seg, halo, c), F32)
    for j in range(seg_blocks):
        if seg_blocks == 1:
            x = x_ref[...]
            pad_s[:, halo:halo + GRID_W, :] = x[..., :c] * _sigmoid(x[..., c:])
        else:
            x = x_ref[j]
            pad_s[0, halo + j * GRID_W:halo + (j + 1) * GRID_W, :] = x[:, :c] * _sigmoid(x[:, c:])

    win = GRID_W + 2 * halo

    def seg_body(r, carry):
        for j in range(seg_blocks):
            x = pad_s[r, j * GRID_W:j * GRID_W + win, :]
            shifted = [x] + [pltpu.roll(x, win - b, axis=0) for b in range(1, SUBLANE)]
            acc = jnp.zeros((GRID_W, c), F32) + b_ref[...]
            for s in range(width):
                o = halo - pad + s
                a0 = (o // SUBLANE) * SUBLANE
                acc = acc + shifted[o % SUBLANE][a0:a0 + GRID_W, :] * w_ref[s:s + 1, :]
            mu = jnp.mean(acc, axis=-1, keepdims=True)
            xc = acc - mu
            var = jnp.mean(xc * xc, axis=-1, keepdims=True)
            y = xc * lax.rsqrt(var + EPS) * lg_ref[...] + lb_ref[...]
            o_ref[r * seg_blocks + j] = _silu(y).astype(o_ref.dtype)
        return carry

    lax.fori_loop(0, n_seg, seg_body, 0)


def _conformer(cv, dw_w, dw_b, ln_g, ln_b, seq_len, n_batch, row_off, latent):
    t, c2 = cv.shape
    c = c2 // 2
    cv3 = cv.reshape(t // GRID_W, GRID_W, c2)
    blocks = seq_len // GRID_W
    n_seg, seg_blocks = (blocks, 1) if latent else (1, blocks)
    halo = 2 * SUBLANE
    assert (dw_w.shape[0] - 1) // 2 <= halo
    blk0 = row_off // seq_len
    kern = functools.partial(_conformer_kernel, n_seg=n_seg, seg_blocks=seg_blocks, halo=halo)
    const = lambda b: (0, 0)
    out = pl.pallas_call(
        kern,
        out_shape=jax.ShapeDtypeStruct((n_batch * blocks, GRID_W, c), BF16),
        grid=(n_batch,),
        in_specs=[pl.BlockSpec((blocks, GRID_W, c2), lambda b: (blk0 + b, 0, 0)),
                  pl.BlockSpec(dw_w.shape, const),
                  pl.BlockSpec((1, c), const), pl.BlockSpec((1, c), const), pl.BlockSpec((1, c), const)],
        out_specs=pl.BlockSpec((blocks, GRID_W, c), lambda b: (b, 0, 0)),
        scratch_shapes=[pltpu.VMEM((n_seg, seg_blocks * GRID_W + 2 * halo, c), F32)],
        compiler_params=_cparams(("parallel",)),
        name="conformer",
    )(cv3, dw_w, dw_b, ln_g, ln_b)
    return out.reshape(n_batch * seq_len, c)


def _outproj_kernel(*refs, n_x, d, n_ctx_tiles):
    yfc, yfl, ydc, ydl, ycc, ycl, w_ref, mod_ref, o_ref = refs[n_x:]
    is_ctx = pl.program_id(0) < n_ctx_tiles
    y = None
    k0 = 0
    for a_c, a_l in ((yfc, yfl), (ydc, ydl), (ycc, ycl)):
        kw = a_c.shape[1]
        a = jnp.where(is_ctx, a_c[...], a_l[...])
        part = _dot(a, w_ref[k0:k0 + kw, :])
        y = part if y is None else y + part
        k0 += kw
    o_ref[...] = _read_stream(refs[:n_x], n_ctx_tiles) + mod_ref[:, 2 * d:3 * d] * y


def _outproj(parts, w_out, x, mod, mod_row, tm, n_ctx_tiles):
    in_specs, args = _stream_specs(x, tm, n_ctx_tiles)
    t = sum(a.shape[0] for a in args)
    d = w_out.shape[1]
    kern = functools.partial(_outproj_kernel, n_x=len(args), d=d, n_ctx_tiles=n_ctx_tiles)
    for a_c, a_l in parts:
        kw = a_c.shape[1]
        in_specs.append(pl.BlockSpec((tm, kw), lambda i: (jnp.minimum(i, n_ctx_tiles - 1), 0)))
        in_specs.append(pl.BlockSpec((tm, kw), lambda i: (jnp.maximum(i - n_ctx_tiles, 0), 0)))
        args += [a_c, a_l]
    in_specs += [pl.BlockSpec(w_out.shape, lambda i: (0, 0)),
                 pl.BlockSpec((None, 1, N_MOD * d), lambda i: (mod_row(i), 0, 0))]
    args += [w_out, mod]
    return pl.pallas_call(
        kern,
        out_shape=jax.ShapeDtypeStruct((t, d), F32),
        grid=(t // tm,),
        in_specs=in_specs,
        out_specs=pl.BlockSpec((tm, d), lambda i: (i, 0)),
        compiler_params=_cparams(("parallel",)),
        name="outproj",
    )(*args)


def _ffn_kernel(*refs, d, final_norm):
    x_ref, g_ref, mod_ref, wg_ref, wu_ref, wd_ref = refs[:6]
    fg_ref = refs[6] if final_norm else None
    o_ref, h_s, acc_s = refs[-3:]
    f = pl.program_id(1)

    @pl.when(f == 0)
    def _():
        mod = mod_ref[...]
        h = _rms(x_ref[...]) * g_ref[...]
        h = h * (1.0 + mod[:, 4 * d:5 * d]) + mod[:, 3 * d:4 * d]
        h_s[...] = h.astype(BF16)
        acc_s[...] = jnp.zeros_like(acc_s)

    h = h_s[...]
    a = _silu(_dot(h, wg_ref[...].astype(BF16))) * _dot(h, wu_ref[...].astype(BF16))
    acc_s[...] += _dot(a.astype(BF16), wd_ref[...].astype(BF16))

    @pl.when(f == pl.num_programs(1) - 1)
    def _():
        o = x_ref[...] + mod_ref[:, 5 * d:6 * d] * acc_s[...]
        if final_norm:
            o = _rms(o) * fg_ref[...]
        o_ref[...] = o


def _ffn(x, gain, mod, mod_row, wg, wu, wd, final_gain, tm, tf):
    t, d = x.shape
    ff = wg.shape[1]
    final_norm = final_gain is not None
    kern = functools.partial(_ffn_kernel, d=d, final_norm=final_norm)
    row = lambda i, f: (i, 0)
    const = lambda i, f: (0, 0)
    in_specs = [pl.BlockSpec((tm, d), row),
                pl.BlockSpec((1, d), const),
                pl.BlockSpec((None, 1, N_MOD * d), lambda i, f: (mod_row(i), 0, 0)),
                pl.BlockSpec((d, tf), lambda i, f: (0, f)),
                pl.BlockSpec((d, tf), lambda i, f: (0, f)),
                pl.BlockSpec((tf, d), lambda i, f: (f, 0))]
    args = [x, gain, mod, wg, wu, wd]
    if final_norm:
        in_specs.append(pl.BlockSpec((1, d), const))
        args.append(final_gain)
    return pl.pallas_call(
        kern,
        out_shape=jax.ShapeDtypeStruct((t, d), F32),
        grid=(t // tm, ff // tf),
        in_specs=in_specs,
        out_specs=pl.BlockSpec((tm, d), row),
        scratch_shapes=[pltpu.VMEM((tm, d), BF16), pltpu.VMEM((tm, d), F32)],
        compiler_params=_cparams(("parallel", "arbitrary")),
        name="ffn",
    )(*args)


INFO_E, INFO_W, INFO_R = 0, 2, 4
MOE_TILE = 1024
MOE_SUB = 256


def _route_kernel(x_ref, g_ref, mod_ref, router_ref, h_o, info_o, cnt_o, cnt_s, *, d, n_exp):
    i = pl.program_id(0)

    @pl.when(i == 0)
    def _():
        cnt_s[...] = jnp.zeros_like(cnt_s)

    mod = mod_ref[...]
    h = _rms(x_ref[...]) * g_ref[...]
    h = h * (1.0 + mod[:, 4 * d:5 * d]) + mod[:, 3 * d:4 * d]
    h_o[...] = h

    logits = _mm3(h, router_ref[...])
    tm = logits.shape[0]
    lane = lax.broadcasted_iota(jnp.int32, logits.shape, 1)
    lg = jnp.where(lane < n_exp, logits, -jnp.inf)
    m1 = jnp.max(lg, axis=-1, keepdims=True)
    i1 = jnp.min(jnp.where(lg == m1, lane, LANE), axis=-1, keepdims=True)
    lg2 = jnp.where(lane == i1, -jnp.inf, lg)
    m2 = jnp.max(lg2, axis=-1, keepdims=True)
    i2 = jnp.min(jnp.where(lg2 == m2, lane, LANE), axis=-1, keepdims=True)
    w1 = 1.0 / (1.0 + jnp.exp(m2 - m1))

    sel1 = lane == i1
    sel2 = lane == i2
    member = jnp.where(sel1 | sel2, 1.0, 0.0)
    ri = lax.broadcasted_iota(jnp.int32, (tm, tm), 0)
    ci = lax.broadcasted_iota(jnp.int32, (tm, tm), 1)
    before = jnp.where(ri > ci, 1.0, 0.0)
    rank = cnt_s[...] + _mm(before, member)
    r1 = jnp.sum(jnp.where(sel1, rank, 0.0), axis=-1, keepdims=True)
    r2 = jnp.sum(jnp.where(sel2, rank, 0.0), axis=-1, keepdims=True)
    cnt_s[...] += jnp.sum(member, axis=0, keepdims=True)
    cnt_o[...] = jnp.broadcast_to(cnt_s[...], cnt_o.shape)

    info = jnp.zeros(logits.shape, F32)
    for ln, val in ((INFO_E, i1.astype(F32)), (INFO_E + 1, i2.astype(F32)), (INFO_W, w1),
                    (INFO_W + 1, 1.0 - w1), (INFO_R, r1), (INFO_R + 1, r2)):
        info = jnp.where(lane == ln, val, info)
    info_o[...] = info


def _route(x, gain, mod, mod_row, router, tm):
    t, d = x.shape
    n_exp = router.shape[1]
    router_p = jnp.zeros((d, LANE), F32).at[:, :n_exp].set(router)
    kern = functools.partial(_route_kernel, d=d, n_exp=n_exp)
    return pl.pallas_call(
        kern,
        out_shape=(jax.ShapeDtypeStruct((t, d), F32), jax.ShapeDtypeStruct((t, LANE), F32),
                   jax.ShapeDtypeStruct((SUBLANE, LANE), F32)),
        grid=(t // tm,),
        in_specs=[pl.BlockSpec((tm, d), lambda i: (i, 0)),
                  pl.BlockSpec((1, d), lambda i: (0, 0)),
                  pl.BlockSpec((None, 1, N_MOD * d), lambda i: (mod_row(i), 0, 0)),
                  pl.BlockSpec((d, LANE), lambda i: (0, 0))],
        out_specs=(pl.BlockSpec((tm, d), lambda i: (i, 0)), pl.BlockSpec((tm, LANE), lambda i: (i, 0)),
                   pl.BlockSpec((SUBLANE, LANE), lambda i: (0, 0))),
        scratch_shapes=[pltpu.VMEM((1, LANE), F32)],
        compiler_params=_cparams(("arbitrary",)),
        name="moe_route",
    )(x, gain, mod, router_p)


def _row_copy(src_hbm, row, dst, slot, sem):
    return pltpu.make_async_copy(src_hbm.at[pl.ds(row, 1), :], dst.at[pl.ds(slot, 1), :], sem)


def _experts_kernel(expert_sm, nvalid_sm, tok_ref, tok_next_ref, h_hbm, wg_ref, wu_ref, wd_ref, y_ref,
                    hbuf, h16, acc, wg16, wu16, wd16, sems, *, sub, n_f):
    del expert_sm
    i = pl.program_id(0)
    j = pl.program_id(1)
    n_i = pl.num_programs(0)
    tm = h16.shape[0]
    slot = i % 2
    nxt = 1 - slot
    nv = nvalid_sm[i]
    n_sub = (nv + sub - 1) // sub
    has_next = jnp.where(i + 1 < n_i, nvalid_sm[jnp.minimum(i + 1, n_i - 1)], 0) > 0
    per_step = -(-tm // n_f)
    base = j * per_step

    def start_row(tok, r, buf):
        _row_copy(h_hbm, tok[0, r], hbuf.at[buf], r, sems.at[buf]).start()

    def fetch_loop(tok, lo, hi, buf):
        def issue(r, carry):
            start_row(tok, r, buf)
            return carry

        lax.fori_loop(lo, hi, issue, 0)

    @pl.when(j == 0)
    def _():
        @pl.when((i == 0) & (nv > 0))
        def _():
            fetch_loop(tok_ref, 0, tm, 0)

        acc[...] = jnp.zeros_like(acc)

        @pl.when(nv > 0)
        def _():
            pltpu.make_async_copy(h_hbm.at[pl.ds(0, tm), :], hbuf.at[slot], sems.at[slot]).wait()

            def cast(s, carry):
                rows = pl.ds(pl.multiple_of(s * sub, sub), sub)
                h16[rows, :] = hbuf[slot, rows, :].astype(BF16)
                return carry

            lax.fori_loop(0, n_sub, cast, 0)

    fast = (n_sub == tm // sub) & has_next

    @pl.when(fast)
    def _():
        n_safe = tm - (n_f - 1) * per_step
        for r in range(min(per_step, n_safe)):
            start_row(tok_next_ref, base + r, nxt)
        h = h16[...]
        a = _silu(_dot(h, wg_ref[...].astype(BF16))) * _dot(h, wu_ref[...].astype(BF16))
        acc[...] += _dot(a.astype(BF16), wd_ref[...].astype(BF16))
        for r in range(n_safe, per_step):
            pl.when(base + r < tm)(functools.partial(start_row, tok_next_ref, base + r, nxt))

    @pl.when(jnp.logical_not(fast))
    def _():
        @pl.when(has_next)
        def _():
            fetch_loop(tok_next_ref, base, jnp.minimum(base + per_step, tm), nxt)

        @pl.when(n_sub > 0)
        def _():
            wg16[...] = wg_ref[...].astype(BF16)
            wu16[...] = wu_ref[...].astype(BF16)
            wd16[...] = wd_ref[...].astype(BF16)

            def block(s, carry):
                rows = pl.ds(pl.multiple_of(s * sub, sub), sub)
                h = h16[rows, :]
                a = _silu(_dot(h, wg16[...])) * _dot(h, wu16[...])
                acc[rows, :] += _dot(a.astype(BF16), wd16[...])
                return carry

            lax.fori_loop(0, n_sub, block, 0)

    @pl.when(j == n_f - 1)
    def _():
        y_ref[...] = acc[...]


def _experts(h, src_tok, tile_expert, tile_nvalid, wg, wu, wd, tm, tf, sub):
    t, d = h.shape
    n_exp, _, ff = wg.shape
    n_tiles = tile_expert.shape[0]
    n_f = ff // tf

    def w_idx(i, j, expert_sm, nvalid_sm):
        return expert_sm[i], jnp.where(nvalid_sm[i] > 0, j, n_f - 1)

    def wgu_map(i, j, expert_sm, nvalid_sm):
        e, jj = w_idx(i, j, expert_sm, nvalid_sm)
        return e, 0, jj

    def wd_map(i, j, expert_sm, nvalid_sm):
        e, jj = w_idx(i, j, expert_sm, nvalid_sm)
        return e, jj, 0

    kern = functools.partial(_experts_kernel, sub=sub, n_f=n_f)
    tok = src_tok.reshape(n_tiles, 1, tm)
    grid_spec = pltpu.PrefetchScalarGridSpec(
        num_scalar_prefetch=2,
        grid=(n_tiles, n_f),
        in_specs=[pl.BlockSpec((None, 1, tm), lambda i, j, *_: (i, 0, 0), memory_space=pltpu.SMEM),
                  pl.BlockSpec((None, 1, tm), lambda i, j, *_: (jnp.minimum(i + 1, n_tiles - 1), 0, 0),
                               memory_space=pltpu.SMEM),
                  pl.BlockSpec(memory_space=pl.ANY),
                  pl.BlockSpec((None, d, tf), wgu_map),
                  pl.BlockSpec((None, d, tf), wgu_map),
                  pl.BlockSpec((None, tf, d), wd_map)],
        out_specs=pl.BlockSpec((tm, d), lambda i, j, *_: (i, 0)),
        scratch_shapes=[pltpu.VMEM((2, tm, d), F32), pltpu.VMEM((tm, d), BF16), pltpu.VMEM((tm, d), F32),
                        pltpu.VMEM((d, tf), BF16), pltpu.VMEM((d, tf), BF16), pltpu.VMEM((tf, d), BF16),
                        pltpu.SemaphoreType.DMA((2,))])
    return pl.pallas_call(
        kern,
        out_shape=jax.ShapeDtypeStruct((n_tiles * tm, d), F32),
        grid_spec=grid_spec,
        compiler_params=_cparams(("arbitrary", "arbitrary")),
        name="moe_experts",
    )(tile_expert, tile_nvalid, tok, tok, h, wg, wu, wd)


def _combine_kernel(*refs, d, final_norm, split_tiles):
    pos_ref, pos_next_ref, info_ref, x_ref, mod_ref = refs[:5]
    p = 5
    if final_norm:
        fg_ref = refs[p]
        p += 1
    y_hbm = refs[p]
    o_refs = refs[p + 1:-2]
    ybuf, sems = refs[-2:]
    tm = x_ref.shape[0]
    i = pl.program_id(0)
    slot = i % 2

    def fetch(pos, buf):
        def issue(r, carry):
            for k in range(TOP_K):
                _row_copy(y_hbm, pos[0, k * tm + r], ybuf.at[buf, k], r, sems.at[buf]).start()
            return carry

        lax.fori_loop(0, tm, issue, 0, unroll=8)

    @pl.when(i == 0)
    def _():
        fetch(pos_ref, 0)

    @pl.when(i + 1 < pl.num_programs(0))
    def _():
        fetch(pos_next_ref, 1 - slot)

    for k in range(TOP_K):
        pltpu.make_async_copy(y_hbm.at[pl.ds(0, tm), :], ybuf.at[slot, k], sems.at[slot]).wait()

    info = info_ref[...]
    f = None
    for k in range(TOP_K):
        term = info[:, INFO_W + k:INFO_W + k + 1] * ybuf[slot, k]
        f = term if f is None else f + term
    o = x_ref[...] + mod_ref[:, 5 * d:6 * d] * f
    if final_norm:
        o = _rms(o) * fg_ref[...]
    if split_tiles is None:
        o_refs[0][...] = o
    else:
        @pl.when(i < split_tiles)
        def _():
            o_refs[0][...] = o

        @pl.when(i >= split_tiles)
        def _():
            o_refs[1][...] = o


def _combine(x, info, pos, y, mod, mod_row, final_gain, tm, split_rows):
    t, d = x.shape
    final_norm = final_gain is not None
    split_tiles = None if split_rows is None else split_rows // tm
    kern = functools.partial(_combine_kernel, d=d, final_norm=final_norm, split_tiles=split_tiles)
    n_tiles = t // tm
    if split_tiles is None:
        out_shape = jax.ShapeDtypeStruct((t, d), F32)
        out_specs = pl.BlockSpec((tm, d), lambda i: (i, 0))
    else:
        out_shape = (jax.ShapeDtypeStruct((split_rows, d), F32), jax.ShapeDtypeStruct((t - split_rows, d), F32))
        out_specs = (pl.BlockSpec((tm, d), lambda i: (jnp.minimum(i, split_tiles - 1), 0)),
                     pl.BlockSpec((tm, d), lambda i: (jnp.maximum(i - split_tiles, 0), 0)))
    in_specs = [pl.BlockSpec((None, 1, TOP_K * tm), lambda i: (i, 0, 0), memory_space=pltpu.SMEM),
                pl.BlockSpec((None, 1, TOP_K * tm), lambda i: (jnp.minimum(i + 1, n_tiles - 1), 0, 0),
                             memory_space=pltpu.SMEM),
                pl.BlockSpec((tm, LANE), lambda i: (i, 0)),
                pl.BlockSpec((tm, d), lambda i: (i, 0)),
                pl.BlockSpec((None, 1, N_MOD * d), lambda i: (mod_row(i), 0, 0))]
    args = [pos, pos, info, x, mod]
    if final_norm:
        in_specs.append(pl.BlockSpec((1, d), lambda i: (0, 0)))
        args.append(final_gain)
    in_specs.append(pl.BlockSpec(memory_space=pl.ANY))
    args.append(y)
    return pl.pallas_call(
        kern,
        out_shape=out_shape,
        grid=(t // tm,),
        in_specs=in_specs,
        out_specs=out_specs,
        scratch_shapes=[pltpu.VMEM((2, TOP_K, tm, d), F32), pltpu.SemaphoreType.DMA((2,))],
        compiler_params=_cparams(("arbitrary",)),
        name="moe_combine",
    )(*args)


def _moe(x, gain, mod, mod_row_fn, router, wg, wu, wd, final_gain, split_rows, tm_route, tm_exp, tm_comb, tf, sub):
    t, d = x.shape
    n_exp = wg.shape[0]
    h, info, cnt = _route(x, gain, mod, mod_row_fn(tm_route), router, tm_route)

    counts = cnt[0, :n_exp].astype(jnp.int32)
    tiles_per = (counts + tm_exp - 1) // tm_exp
    tile_end = jnp.cumsum(tiles_per)
    tile_start = tile_end - tiles_per
    n_tiles = (TOP_K * t) // tm_exp + n_exp
    tile_id = jnp.arange(n_tiles, dtype=jnp.int32)
    tile_expert = jnp.minimum(jnp.sum(tile_id[:, None] >= tile_end[None, :], axis=1), n_exp - 1).astype(jnp.int32)
    tile_nvalid = jnp.where(tile_id < tile_end[n_exp - 1],
                            jnp.clip(counts[tile_expert] - (tile_id - tile_start[tile_expert]) * tm_exp, 0, tm_exp),
                            0).astype(jnp.int32)
    slot_start = tile_start * tm_exp
    experts = info[:, INFO_E:INFO_E + TOP_K].astype(jnp.int32)
    slots = slot_start[experts] + info[:, INFO_R:INFO_R + TOP_K].astype(jnp.int32)
    token = jnp.broadcast_to(jnp.arange(t, dtype=jnp.int32)[:, None], slots.shape)
    src_tok = jnp.zeros((n_tiles * tm_exp,), jnp.int32).at[slots.reshape(-1)].set(
        token.reshape(-1), unique_indices=True)

    y = _experts(h, src_tok, tile_expert, tile_nvalid, wg, wu, wd, tm_exp, tf, sub)
    pos = slots.reshape(t // tm_comb, tm_comb, TOP_K).transpose(0, 2, 1).reshape(t // tm_comb, 1, TOP_K * tm_comb)
    return _combine(x, info, pos, y, mod, mod_row_fn(tm_comb), final_gain, tm_comb, split_rows)


def _lane_row(vals, offset):
    n = vals.shape[0]
    return jnp.zeros((1, LANE), F32).at[0, offset:offset + n].set(vals.astype(F32))


def kernel(x_prompt, x_sample, state_delta, c, c_ctx, norm1, norm2, w_mod, b_mod, w_in, sc_w, sc_b, dn_a_log, dn_dt_bias, dn_norm, cv_dw_w, cv_dw_b, cv_ln_g, cv_ln_b, w_out, ffn_wg, ffn_wu, ffn_wd, moe_router, moe_wg, moe_wu, moe_wd, final_norm):
    b_ctx, l_ctx, d = x_prompt.shape
    b_lat, l_lat, _ = x_sample.shape
    depth = w_in.shape[0]
    t_ctx = b_ctx * l_ctx
    t_lat = b_lat * l_lat
    f_w = d // 4
    f_gw = f_w // F_GROUPS
    dn_w = d // 2
    dk = dn_w // DN_HEADS
    cv_w = d // 4
    qkvz_w = 4 * dn_w
    n_gate = 2 * N_DIRS * DN_HEADS
    n_exp = moe_wg.shape[1]
    assert dk == LANE and t_ctx % l_lat == 0 and b_lat + 1 <= MOD_ROWS
    assert l_ctx % CHUNK == 0 and l_lat % CHUNK == 0 and l_lat % GRID_W == 0 and CHUNK == GRID_W

    tm = _pick_tile(512, t_ctx, l_lat)

    def mod_row_fn(tile):
        n_ctx_tiles = t_ctx // tile
        per_seq = l_lat // tile
        return lambda i: jnp.where(i < n_ctx_tiles, 0, 1 + (i - n_ctx_tiles) // per_seq)

    x = (x_prompt.reshape(t_ctx, d), x_sample.reshape(t_lat, d))
    cvec = jnp.zeros((MOD_ROWS, d), F32).at[0].set(c_ctx).at[1:1 + b_lat].set(c)
    mods = _adaln(cvec, w_mod, b_mod).reshape(depth, MOD_ROWS, 1, N_MOD * d)

    cg, sg = _dft_tables(f_gw)
    eye_g = np.eye(F_GROUPS)
    ccs = jnp.asarray(np.concatenate([np.kron(eye_g, cg), np.kron(eye_g, sg)], axis=1), F32).astype(BF16)

    passes = ((l_ctx, b_ctx, 0, False), (l_lat, b_lat, t_ctx, True))
    o_q = f_w
    o_ba = f_w + qkvz_w
    o_cv = o_ba + n_gate
    new_states = []
    for l in range(depth):
        w = w_in[l]
        w_perm = jnp.concatenate(
            [w[:, :o_ba], w[:, o_cv:o_cv + 2 * cv_w], w[:, o_ba:o_cv], jnp.zeros((d, LANE - n_gate), F32)],
            axis=1).astype(BF16)
        alog = _lane_row(dn_a_log[l].reshape(-1), N_DIRS * DN_HEADS)
        dtb = _lane_row(dn_dt_bias[l].reshape(-1), N_DIRS * DN_HEADS)
        xc, xs, qkvz, cv, gates = _proj(x, norm1[l][None], mods[l], w_perm, ccs, alog, dtb,
                                        mod_row_fn(tm), tm, t_ctx // tm, f_w, qkvz_w, 2 * cv_w)
        g16 = gates[:, :n_gate].reshape(-1, 2 * N_DIRS, DN_HEADS)
        gcol = g16.transpose(2, 0, 1)
        grow = g16.reshape(-1, CHUNK, 2 * N_DIRS, DN_HEADS).transpose(3, 0, 2, 1)

        parts = [[], [], []]
        for seq_len, n_batch, row_off, latent in passes:
            parts[0].append(_fourier(xc, xs, seq_len, n_batch, row_off, f_gw))
            res = _delta(qkvz, gcol, grow, sc_w[l], sc_b[l][None], dn_norm[l][None],
                         state_delta if latent else None, l, seq_len, n_batch, row_off, dk,
                         write_state=not latent)
            if latent:
                parts[1].append(res)
            else:
                parts[1].append(res[0])
                new_states.append(res[1])
            parts[2].append(_conformer(cv, cv_dw_w[l], cv_dw_b[l][None], cv_ln_g[l][None],
                                       cv_ln_b[l][None], seq_len, n_batch, row_off, latent))
        x = _outproj(parts, w_out[l].astype(BF16), x, mods[l], mod_row_fn(tm), tm, t_ctx // tm)

        fg = final_norm[None] if l == depth - 1 else None
        if l % 2 == 0:
            i = l // 2
            ff = ffn_wg.shape[-1]
            tf = max(t for t in range(LANE, FFN_TILE_MAX + 1, LANE) if ff % t == 0)
            x = _ffn(x, norm2[l][None], mods[l], mod_row_fn(tm), ffn_wg[i].astype(BF16), ffn_wu[i].astype(BF16),
                     ffn_wd[i].astype(BF16), fg, tm, tf)
        else:
            i = l // 2
            x = _moe(x, norm2[l][None], mods[l], mod_row_fn, moe_router[i], moe_wg[i], moe_wu[i], moe_wd[i], fg,
                     split_rows=t_ctx if l == depth - 1 else None,
                     tm_route=tm, tm_exp=MOE_TILE, tm_comb=_pick_tile(256, t_ctx, l_lat),
                     tf=_pick_tile(512, moe_wg.shape[-1]), sub=MOE_SUB)

    y_ctx, y_lat = x if isinstance(x, tuple) else (x[:t_ctx], x[t_ctx:])
    y_prompt = y_ctx.reshape(b_ctx, l_ctx, d)
    y_sample = y_lat.reshape(b_lat, l_lat, d)
    return y_prompt, y_sample, jnp.stack(new_states, axis=1)
```

```python
import functools
import math

import numpy as np
import jax
import jax.numpy as jnp
from jax import lax
from jax.experimental import pallas as pl
from jax.experimental.pallas import tpu as pltpu

F32 = jnp.float32
BF16 = jnp.bfloat16
EPS = 1e-6

LANE = 128
SUBLANE = 8
VMEM_LIMIT = 56 * 1024 * 1024

CHUNK = 64
GRID_W = 64
F_GROUPS = 4
DN_HEADS = 4
N_DIRS = 2
DELTA_VMEM_BUDGET = VMEM_LIMIT - 12 * 1024 * 1024
FFN_TILE_MAX = 1408
PREP_UNROLL = 16
TOP_K = 2
N_MOD = 6
MOD_ROWS = 16


def _cparams(sem):
    return pltpu.CompilerParams(dimension_semantics=sem, vmem_limit_bytes=VMEM_LIMIT)


def _pick_tile(limit, *sizes):
    t = limit
    while any(s % t for s in sizes):
        t //= 2
    return t


def _dot(a, b):
    return jnp.dot(a, b, preferred_element_type=F32)


def _mm(a, b):
    return jnp.dot(a.astype(BF16), b.astype(BF16), preferred_element_type=F32)


def _mm_nt(a, b):
    return lax.dot_general(a.astype(BF16), b.astype(BF16), (((1,), (1,)), ((), ())),
                           preferred_element_type=F32)


def _mm_tn(a, b):
    return lax.dot_general(a.astype(BF16), b.astype(BF16), (((0,), (0,)), ((), ())),
                           preferred_element_type=F32)


def _split(a):
    hi = a.astype(BF16)
    lo = (a - hi.astype(F32)).astype(BF16)
    return hi, lo


def _mm3(a, b):
    ah, al = _split(a)
    bh, bl = _split(b)
    return _dot(ah, bh) + (_dot(ah, bl) + _dot(al, bh))


def _sigmoid(x):
    return 1.0 / (1.0 + jnp.exp(-x))


def _silu(x):
    return x * _sigmoid(x)


def _rms(x):
    return x * lax.rsqrt(jnp.mean(x * x, axis=-1, keepdims=True) + EPS)


def _stream_specs(x, tm, n_ctx_tiles):
    if isinstance(x, tuple):
        d = x[0].shape[1]
        return ([pl.BlockSpec((tm, d), lambda i, *_: (jnp.minimum(i, n_ctx_tiles - 1), 0)),
                 pl.BlockSpec((tm, d), lambda i, *_: (jnp.maximum(i - n_ctx_tiles, 0), 0))], list(x))
    return [pl.BlockSpec((tm, x.shape[1]), lambda i, *_: (i, 0))], [x]


def _read_stream(x_refs, n_ctx_tiles):
    if len(x_refs) == 2:
        return jnp.where(pl.program_id(0) < n_ctx_tiles, x_refs[0][...], x_refs[1][...])
    return x_refs[0][...]


def _adaln_kernel(c_ref, w_ref, b_ref, o_ref):
    o_ref[...] = _mm(_silu(c_ref[...]), w_ref[...]) + b_ref[...]


def _adaln(cvec, w_mod, b_mod):
    depth, d, n = w_mod.shape
    tn = _pick_tile(1024, n)
    return pl.pallas_call(
        _adaln_kernel,
        out_shape=jax.ShapeDtypeStruct((depth, MOD_ROWS, n), F32),
        grid=(depth, n // tn),
        in_specs=[pl.BlockSpec((MOD_ROWS, d), lambda l, j: (0, 0)),
                  pl.BlockSpec((None, d, tn), lambda l, j: (l, 0, j)),
                  pl.BlockSpec((None, 1, tn), lambda l, j: (l, 0, j))],
        out_specs=pl.BlockSpec((None, MOD_ROWS, tn), lambda l, j: (l, 0, j)),
        compiler_params=_cparams(("parallel", "parallel")),
        name="adaln",
    )(cvec, w_mod, b_mod.reshape(depth, 1, n))


def _proj_kernel(*refs, n_x, n_ctx_tiles, d, f_w, qkvz_w, cv_w):
    g_ref, mod_ref, w_ref, ccs_ref, alog_ref, dtb_ref, xc_o, xs_o, qkvz_o, cv_o, gate_o = refs[n_x:]
    x = _read_stream(refs[:n_x], n_ctx_tiles)
    mod = mod_ref[...]
    h = _rms(x) * g_ref[...]
    h = (h * (1.0 + mod[:, d:2 * d]) + mod[:, 0:d]).astype(BF16)

    xf = _dot(h, w_ref[:, 0:f_w])
    xcs = _dot(xf.astype(BF16), ccs_ref[...])
    xc_o[...] = xcs[:, :f_w].astype(xc_o.dtype)
    xs_o[...] = xcs[:, f_w:].astype(xs_o.dtype)
    o = f_w
    step = 4 * LANE
    for n0 in range(0, qkvz_w, step):
        qkvz_o[:, n0:n0 + step] = _dot(h, w_ref[:, o + n0:o + n0 + step])
    o += qkvz_w
    cv_o[...] = _dot(h, w_ref[:, o:o + cv_w])
    o += cv_w
    ba = _dot(h, w_ref[:, o:o + LANE])

    tm = ba.shape[0]
    lane = lax.broadcasted_iota(jnp.int32, ba.shape, 1)
    row = lax.broadcasted_iota(jnp.int32, ba.shape, 0) % CHUNK
    beta = _sigmoid(ba)
    t = ba + dtb_ref[...]
    softplus = jnp.maximum(t, 0.0) + jnp.log1p(jnp.exp(-jnp.abs(t)))
    g = -jnp.exp(alog_ref[...]) * softplus
    cf = g
    cb = g
    s = 1
    while s < CHUNK:
        cf = cf + jnp.where(row >= s, pltpu.roll(cf, s, axis=0), 0.0)
        cb = cb + jnp.where(row < CHUNK - s, pltpu.roll(cb, tm - s, axis=0), 0.0)
        s *= 2
    n_beta = N_DIRS * DN_HEADS
    gate_o[...] = jnp.where(lane < n_beta, beta, jnp.where(lane < n_beta + DN_HEADS, cf, cb))


def _proj(x, gain, mod, w_perm, ccs, alog, dtb, mod_row, tm, n_ctx_tiles, f_w, qkvz_w, cv_w):
    x_specs, x_args = _stream_specs(x, tm, n_ctx_tiles)
    t = sum(a.shape[0] for a in x_args)
    d, n = w_perm.shape
    kern = functools.partial(_proj_kernel, n_x=len(x_args), n_ctx_tiles=n_ctx_tiles, d=d, f_w=f_w,
                             qkvz_w=qkvz_w, cv_w=cv_w)
    row = lambda i: (i, 0)
    const = lambda i: (0, 0)
    return pl.pallas_call(
        kern,
        out_shape=(jax.ShapeDtypeStruct((t, f_w), BF16), jax.ShapeDtypeStruct((t, f_w), BF16),
                   jax.ShapeDtypeStruct((t, qkvz_w), F32), jax.ShapeDtypeStruct((t, cv_w), F32),
                   jax.ShapeDtypeStruct((t, LANE), F32)),
        grid=(t // tm,),
        in_specs=x_specs + [
            pl.BlockSpec((1, d), const),
            pl.BlockSpec((None, 1, N_MOD * d), lambda i: (mod_row(i), 0, 0)),
            pl.BlockSpec((d, n), const),
            pl.BlockSpec(ccs.shape, const),
            pl.BlockSpec((1, LANE), const),
            pl.BlockSpec((1, LANE), const)],
        out_specs=(pl.BlockSpec((tm, f_w), row), pl.BlockSpec((tm, f_w), row),
                   pl.BlockSpec((tm, qkvz_w), row), pl.BlockSpec((tm, cv_w), row),
                   pl.BlockSpec((tm, LANE), row)),
        compiler_params=_cparams(("parallel",)),
        name="proj",
    )(*x_args, gain, mod, w_perm, ccs, alog, dtb)


def _fourier_kernel(xc_ref, xs_ref, cn_ref, sn_ref, o_ref, *, scale):
    y = _dot(cn_ref[...], xc_ref[...]) - _dot(sn_ref[...], xs_ref[...])
    o_ref[...] = (y * scale).astype(o_ref.dtype)


def _dft_tables(n):
    j = np.arange(n, dtype=np.int64)
    ang = (2.0 * np.pi / n) * ((j[:, None] * j[None, :]) % n).astype(np.float64)
    return np.cos(ang), np.sin(ang)


def _fourier(xc, xs, seq_len, n_batch, row_off, f_gw):
    f_w = xc.shape[1]
    cn, sn = _dft_tables(seq_len)
    cn = jnp.asarray(cn, F32).astype(BF16)
    sn = jnp.asarray(sn, F32).astype(BF16)
    tr = _pick_tile(1024, seq_len)
    nt = seq_len // tr
    blk0 = row_off // seq_len
    kern = functools.partial(_fourier_kernel, scale=1.0 / math.sqrt(seq_len * f_gw))
    return pl.pallas_call(
        kern,
        out_shape=jax.ShapeDtypeStruct((n_batch * seq_len, f_w), BF16),
        grid=(nt, n_batch),
        in_specs=[pl.BlockSpec((seq_len, f_w), lambda i, b: (blk0 + b, 0)),
                  pl.BlockSpec((seq_len, f_w), lambda i, b: (blk0 + b, 0)),
                  pl.BlockSpec((tr, seq_len), lambda i, b: (i, 0)),
                  pl.BlockSpec((tr, seq_len), lambda i, b: (i, 0))],
        out_specs=pl.BlockSpec((tr, f_w), lambda i, b: (b * nt + i, 0)),
        compiler_params=_cparams(("parallel", "parallel")),
        name="fourier",
    )(xc, xs, cn, sn)


def _inverse_consts():
    ri = lax.broadcasted_iota(jnp.int32, (CHUNK, CHUNK), 0)
    ci = lax.broadcasted_iota(jnp.int32, (CHUNK, CHUNK), 1)
    eye = (ri == ci).astype(F32)
    same32 = (ri // 32) == (ci // 32)
    m16 = ((ri // 16) == (ci // 16)).astype(F32)
    return eye, m16, (same32.astype(F32) - m16, 1.0 - same32.astype(F32))


def _unit_tri_inverse(mats, consts):
    eye, m16, offs = consts
    ps = [-(a * m16) for a in mats]
    ts = [eye + p for p in ps]
    for _ in range(3):
        ps = [_mm(p, p) for p in ps]
        ts = [t + _mm(t, p) for t, p in zip(ts, ps)]
    for off in offs:
        us = [_mm(t, a * off) for t, a in zip(ts, mats)]
        ts = [t - _mm(u, t) for t, u in zip(ts, us)]
    return ts


def _delta_kernel(*refs, seq_len, dk, zero_init, write_state):
    (q_ref, k_ref, v_ref, z_ref, gcol_ref, grow_ref,
     wq_ref, wk_ref, wv_ref, bq_ref, bk_ref, bv_ref, og_ref) = refs[:13]
    pos = 13
    if not zero_init:
        s0_ref = refs[pos]
        pos += 1
    y_ref = refs[pos]
    pos += 1
    if write_state:
        sfin_ref = refs[pos]
        pos += 1
    qs, ks, vs, o_s, st_s, pq_s, n_s = refs[pos:]

    n_chunks = seq_len // CHUNK
    n_heads = qs.shape[1] // dk
    heads = range(n_heads)
    rows = lax.broadcasted_iota(jnp.int32, qs.shape, 0)

    def hcols(h):
        return slice(h * dk, (h + 1) * dk)

    def conv_silu(x_ref, w_ref, b_ref):
        x = x_ref[...]
        width = w_ref.shape[0]
        pad = (width - 1) // 2
        acc = jnp.zeros_like(x) + b_ref[...]
        for s in range(width):
            o = s - pad
            if o == 0:
                xs = x
            else:
                xs = pltpu.roll(x, (-o) % seq_len, axis=0)
                xs = jnp.where((rows + o >= 0) & (rows + o < seq_len), xs, 0.0)
            acc = acc + xs * w_ref[s:s + 1, :]
        return _silu(acc)

    def l2norm(x):
        return x * lax.rsqrt(jnp.sum(x * x, axis=-1, keepdims=True) + EPS)

    qc = conv_silu(q_ref, wq_ref, bq_ref)
    kc = conv_silu(k_ref, wk_ref, bk_ref)
    for h in heads:
        qs[:, hcols(h)] = l2norm(qc[:, hcols(h)]) * (dk ** -0.5)
        ks[:, hcols(h)] = l2norm(kc[:, hcols(h)])
    vs[...] = conv_silu(v_ref, wv_ref, bv_ref)
    if zero_init:
        st_s[...] = jnp.zeros_like(st_s)
    else:
        st_s[...] = s0_ref[...]

    ri = lax.broadcasted_iota(jnp.int32, (CHUNK, CHUNK), 0)
    ci = lax.broadcasted_iota(jnp.int32, (CHUNK, CHUNK), 1)
    incl = (ri >= ci, ri <= ci)
    unroll = math.gcd(PREP_UNROLL // n_heads, n_chunks)
    inv_consts = _inverse_consts()
    strict = (ri > ci, ri < ci)
    n_beta = N_DIRS

    last_row = (CHUNK - 1, 0)

    def prep_group(first_chunk, chunks):
        blocks = [(c, h) for c in chunks for h in heads]
        chains = [(c, h, d) for c, h in blocks for d in range(N_DIRS)]
        cidx = {c: first_chunk + c for c in chunks}
        r0 = {c: pl.multiple_of(cidx[c] * CHUNK, CHUNK) for c in chunks}
        q = {(c, h): qs[pl.ds(r0[c], CHUNK), hcols(h)] for c, h in blocks}
        k = {(c, h): ks[pl.ds(r0[c], CHUNK), hcols(h)] for c, h in blocks}
        v = {(c, h): vs[pl.ds(r0[c], CHUNK), hcols(h)] for c, h in blocks}
        gc4 = {(c, h): gcol_ref[h, pl.ds(r0[c], CHUNK), :] for c, h in blocks}
        gr4 = {(c, h): grow_ref[h, cidx[c]] for c, h in blocks}
        kq = {}
        for ch in blocks:
            k16 = k[ch].astype(BF16)
            kq[ch] = _mm_nt(jnp.concatenate([k16, q[ch].astype(BF16)], axis=0), k16)
        beta, gcl, decay, a = {}, {}, {}, []
        for c, h, d in chains:
            beta[c, h, d] = gc4[c, h][:, d:d + 1]
            gcl[c, h, d] = gc4[c, h][:, n_beta + d:n_beta + d + 1]
            grw = gr4[c, h][n_beta + d:n_beta + d + 1, :]
            decay[c, h, d] = jnp.exp(jnp.where(incl[d], gcl[c, h, d] - grw, -jnp.inf))
            a.append(jnp.where(strict[d], beta[c, h, d] * kq[c, h][:CHUNK] * decay[c, h, d], 0.0))
        t = dict(zip(chains, _unit_tri_inverse(a, inv_consts)))
        eg = {chd: jnp.exp(gcl[chd]) for chd in chains}
        sol = {(c, h, d): _mm(t[c, h, d], jnp.concatenate(
            [v[c, h] * beta[c, h, d], k[c, h] * (beta[c, h, d] * eg[c, h, d])], axis=1))
            for c, h, d in chains}
        aw = {(c, h, d): _mm(kq[c, h][CHUNK:] * decay[c, h, d], sol[c, h, d])
              for c, h, d in chains}
        kuw = {}
        for c, h, d in chains:
            g_last = gcl[c, h, d][last_row[d]:last_row[d] + 1, :]
            kuw[c, h, d] = _mm_tn(k[c, h] * jnp.exp(g_last - gcl[c, h, d]), sol[c, h, d])
        for c, h, d in chains:
            pq_s[d, h, cidx[c]] = jnp.concatenate(
                [kuw[c, h, d][:, dk:], q[c, h] * eg[c, h, d] - aw[c, h, d][:, dk:]], axis=0).astype(BF16)
            n_s[d, h, cidx[c]] = kuw[c, h, d][:, :dk]
        for c, h in blocks:
            o_s[pl.ds(r0[c], CHUNK), hcols(h)] = aw[c, h, 0][:, :dk] + aw[c, h, 1][:, :dk]

    def prep_body(i, carry):
        prep_group(i * unroll, range(unroll))
        return carry

    lax.fori_loop(0, n_chunks // unroll, prep_body, 0)

    def scan_body(n, carry):
        for h in heads:
            for d in range(N_DIRS):
                c = n if d == 0 else n_chunks - 1 - n
                r0 = pl.multiple_of(c * CHUNK, CHUNK)
                g_last = grow_ref[h, c][n_beta + d:n_beta + d + 1, last_row[d]:last_row[d] + 1]
                s = st_s[d, h]
                r = _dot(pq_s[d, h, c], s.astype(BF16))
                st_s[d, h] = s * jnp.exp(g_last) + n_s[d, h, c] - r[:dk]
                o_s[pl.ds(r0, CHUNK), hcols(h)] += r[dk:]
        return carry

    lax.fori_loop(0, n_chunks, scan_body, 0)

    o = o_s[...]
    z = _silu(z_ref[...])
    for h in heads:
        y_ref[:, hcols(h)] = (_rms(o[:, hcols(h)]) * og_ref[...] * z[:, hcols(h)]).astype(y_ref.dtype)
    if write_state:
        sfin_ref[...] = st_s[...]


def _delta(qkvz, gcol, grow, sc_w, sc_b, o_g, s0, layer, seq_len, n_batch, row_off, dk, write_state):
    nh = DN_HEADS
    blk0 = row_off // seq_len
    n_chunks = seq_len // CHUNK
    zero_init = s0 is None
    kern = functools.partial(_delta_kernel, seq_len=seq_len, dk=dk, zero_init=zero_init,
                             write_state=write_state)

    def vmem_bytes(hp):
        rows = seq_len * hp * dk * 4
        per_chunk = N_DIRS * hp * n_chunks * dk * ((dk + CHUNK) * 2 + dk * 4)
        gate_cols = 2 * hp * seq_len * LANE * 4
        return 2 * 4 * rows + 4 * rows + per_chunk + gate_cols

    hp = max(h for h in (1, 2, 4) if nh % h == 0 and (h == 1 or vmem_bytes(h) <= DELTA_VMEM_BUDGET))
    gw = hp * dk
    groups = nh // hp

    def col(group):
        return pl.BlockSpec((seq_len, gw), lambda b, g: (blk0 + b, group * groups + g))

    def wspec(group, rows):
        return pl.BlockSpec((rows, gw), lambda b, g: (0, group * groups + g))

    width = sc_w.shape[0]
    in_specs = [col(0), col(1), col(2), col(3),
                pl.BlockSpec((hp, seq_len, 2 * N_DIRS), lambda b, g: (g, blk0 + b, 0)),
                pl.BlockSpec((hp, n_chunks, 2 * N_DIRS, CHUNK), lambda b, g: (g, blk0 + b, 0, 0)),
                wspec(0, width), wspec(1, width), wspec(2, width),
                wspec(0, 1), wspec(1, 1), wspec(2, 1),
                pl.BlockSpec((1, dk), lambda b, g: (0, 0))]
    args = [qkvz, qkvz, qkvz, qkvz, gcol, grow, sc_w, sc_w, sc_w, sc_b, sc_b, sc_b, o_g]
    if not zero_init:
        in_specs.append(pl.BlockSpec((None, None, N_DIRS, hp, dk, dk),
                                     lambda b, g: (b, layer, 0, g, 0, 0)))
        args.append(s0)
    y_shape = jax.ShapeDtypeStruct((n_batch * seq_len, nh * dk), BF16)
    y_spec = pl.BlockSpec((seq_len, gw), lambda b, g: (b, g))
    if write_state:
        out_shape = (y_shape, jax.ShapeDtypeStruct((n_batch, N_DIRS, nh, dk, dk), F32))
        out_specs = (y_spec, pl.BlockSpec((None, N_DIRS, hp, dk, dk), lambda b, g: (b, 0, g, 0, 0)))
    else:
        out_shape = y_shape
        out_specs = y_spec
    return pl.pallas_call(
        kern,
        out_shape=out_shape,
        grid=(n_batch, groups),
        in_specs=in_specs,
        out_specs=out_specs,
        scratch_shapes=[pltpu.VMEM((seq_len, gw), F32)] * 4 + [
            pltpu.VMEM((N_DIRS, hp, dk, dk), F32),
            pltpu.VMEM((N_DIRS, hp, n_chunks, dk + CHUNK, dk), BF16),
            pltpu.VMEM((N_DIRS, hp, n_chunks, dk, dk), F32)],
        compiler_params=_cparams(("parallel", "parallel")),
        name="delta",
    )(*args)


def _conformer_kernel(x_ref, w_ref, b_ref, lg_ref, lb_ref, o_ref, pad_s, *, n_seg, seg_blocks, halo):
    c = o_ref.shape[-1]
    width = w_ref.shape[0]
    pad = (width - 1) // 2
    seg_len = seg_blocks * GRID_W
    pad_s[:, 0:halo, :] = jnp.zeros((n_seg, halo, c), F32)
    pad_s[:, halo + seg_len:2 * halo + seg_len, :] = jnp.zeros((n_seg, halo, c), F32)
    for j in range(seg_blocks):
        if seg_blocks == 1:
            x = x_ref[...]
            pad_s[:, halo:halo + GRID_W, :] = x[..., :c] * _sigmoid(x[..., c:])
        else:
            x = x_ref[j]
            pad_s[0, halo + j * GRID_W:halo + (j + 1) * GRID_W, :] = x[:, :c] * _sigmoid(x[:, c:])

    win = GRID_W + 2 * halo

    def seg_body(r, carry):
        for j in range(seg_blocks):
            x = pad_s[r, j * GRID_W:j * GRID_W + win, :]
            shifted = [x] + [pltpu.roll(x, win - b, axis=0) for b in range(1, SUBLANE)]
            acc = jnp.zeros((GRID_W, c), F32) + b_ref[...]
            for s in range(width):
                o = halo - pad + s
                a0 = (o // SUBLANE) * SUBLANE
                acc = acc + shifted[o % SUBLANE][a0:a0 + GRID_W, :] * w_ref[s:s + 1, :]
            mu = jnp.mean(acc, axis=-1, keepdims=True)
            xc = acc - mu
            var = jnp.mean(xc * xc, axis=-1, keepdims=True)
            y = xc * lax.rsqrt(var + EPS) * lg_ref[...] + lb_ref[...]
            o_ref[r * seg_blocks + j] = _silu(y).astype(o_ref.dtype)
        return carry

    lax.fori_loop(0, n_seg, seg_body, 0)


def _conformer(cv, dw_w, dw_b, ln_g, ln_b, seq_len, n_batch, row_off, latent):
    t, c2 = cv.shape
    c = c2 // 2
    cv3 = cv.reshape(t // GRID_W, GRID_W, c2)
    blocks = seq_len // GRID_W
    n_seg, seg_blocks = (blocks, 1) if latent else (1, blocks)
    halo = 2 * SUBLANE
    assert (dw_w.shape[0] - 1) // 2 <= halo
    blk0 = row_off // seq_len
    kern = functools.partial(_conformer_kernel, n_seg=n_seg, seg_blocks=seg_blocks, halo=halo)
    const = lambda b: (0, 0)
    out = pl.pallas_call(
        kern,
        out_shape=jax.ShapeDtypeStruct((n_batch * blocks, GRID_W, c), BF16),
        grid=(n_batch,),
        in_specs=[pl.BlockSpec((blocks, GRID_W, c2), lambda b: (blk0 + b, 0, 0)),
                  pl.BlockSpec(dw_w.shape, const),
                  pl.BlockSpec((1, c), const), pl.BlockSpec((1, c), const), pl.BlockSpec((1, c), const)],
        out_specs=pl.BlockSpec((blocks, GRID_W, c), lambda b: (b, 0, 0)),
        scratch_shapes=[pltpu.VMEM((n_seg, seg_blocks * GRID_W + 2 * halo, c), F32)],
        compiler_params=_cparams(("parallel",)),
        name="conformer",
    )(cv3, dw_w, dw_b, ln_g, ln_b)
    return out.reshape(n_batch * seq_len, c)


def _outproj_kernel(*refs, n_x, d, n_ctx_tiles):
    yfc, yfl, ydc, ydl, ycc, ycl, w_ref, mod_ref, o_ref = refs[n_x:]
    is_ctx = pl.program_id(0) < n_ctx_tiles
    y = None
    k0 = 0
    for a_c, a_l in ((yfc, yfl), (ydc, ydl), (ycc, ycl)):
        kw = a_c.shape[1]
        a = jnp.where(is_ctx, a_c[...], a_l[...])
        part = _dot(a, w_ref[k0:k0 + kw, :])
        y = part if y is None else y + part
        k0 += kw
    o_ref[...] = _read_stream(refs[:n_x], n_ctx_tiles) + mod_ref[:, 2 * d:3 * d] * y


def _outproj(parts, w_out, x, mod, mod_row, tm, n_ctx_tiles):
    in_specs, args = _stream_specs(x, tm, n_ctx_tiles)
    t = sum(a.shape[0] for a in args)
    d = w_out.shape[1]
    kern = functools.partial(_outproj_kernel, n_x=len(args), d=d, n_ctx_tiles=n_ctx_tiles)
    for a_c, a_l in parts:
        kw = a_c.shape[1]
        in_specs.append(pl.BlockSpec((tm, kw), lambda i: (jnp.minimum(i, n_ctx_tiles - 1), 0)))
        in_specs.append(pl.BlockSpec((tm, kw), lambda i: (jnp.maximum(i - n_ctx_tiles, 0), 0)))
        args += [a_c, a_l]
    in_specs += [pl.BlockSpec(w_out.shape, lambda i: (0, 0)),
                 pl.BlockSpec((None, 1, N_MOD * d), lambda i: (mod_row(i), 0, 0))]
    args += [w_out, mod]
    return pl.pallas_call(
        kern,
        out_shape=jax.ShapeDtypeStruct((t, d), F32),
        grid=(t // tm,),
        in_specs=in_specs,
        out_specs=pl.BlockSpec((tm, d), lambda i: (i, 0)),
        compiler_params=_cparams(("parallel",)),
        name="outproj",
    )(*args)


def _ffn_kernel(*refs, d, final_norm):
    x_ref, g_ref, mod_ref, wg_ref, wu_ref, wd_ref = refs[:6]
    fg_ref = refs[6] if final_norm else None
    o_ref, h_s, acc_s = refs[-3:]
    f = pl.program_id(1)

    @pl.when(f == 0)
    def _():
        mod = mod_ref[...]
        h = _rms(x_ref[...]) * g_ref[...]
        h = h * (1.0 + mod[:, 4 * d:5 * d]) + mod[:, 3 * d:4 * d]
        h_s[...] = h.astype(BF16)
        acc_s[...] = jnp.zeros_like(acc_s)

    h = h_s[...]
    a = _silu(_dot(h, wg_ref[...].astype(BF16))) * _dot(h, wu_ref[...].astype(BF16))
    acc_s[...] += _dot(a.astype(BF16), wd_ref[...].astype(BF16))

    @pl.when(f == pl.num_programs(1) - 1)
    def _():
        o = x_ref[...] + mod_ref[:, 5 * d:6 * d] * acc_s[...]
        if final_norm:
            o = _rms(o) * fg_ref[...]
        o_ref[...] = o


def _ffn(x, gain, mod, mod_row, wg, wu, wd, final_gain, tm, tf):
    t, d = x.shape
    ff = wg.shape[1]
    final_norm = final_gain is not None
    kern = functools.partial(_ffn_kernel, d=d, final_norm=final_norm)
    row = lambda i, f: (i, 0)
    const = lambda i, f: (0, 0)
    in_specs = [pl.BlockSpec((tm, d), row),
                pl.BlockSpec((1, d), const),
                pl.BlockSpec((None, 1, N_MOD * d), lambda i, f: (mod_row(i), 0, 0)),
                pl.BlockSpec((d, tf), lambda i, f: (0, f)),
                pl.BlockSpec((d, tf), lambda i, f: (0, f)),
                pl.BlockSpec((tf, d), lambda i, f: (f, 0))]
    args = [x, gain, mod, wg, wu, wd]
    if final_norm:
        in_specs.append(pl.BlockSpec((1, d), const))
        args.append(final_gain)
    return pl.pallas_call(
        kern,
        out_shape=jax.ShapeDtypeStruct((t, d), F32),
        grid=(t // tm, ff // tf),
        in_specs=in_specs,
        out_specs=pl.BlockSpec((tm, d), row),
        scratch_shapes=[pltpu.VMEM((tm, d), BF16), pltpu.VMEM((tm, d), F32)],
        compiler_params=_cparams(("parallel", "arbitrary")),
        name="ffn",
    )(*args)


INFO_E, INFO_W, INFO_R = 0, 2, 4
MOE_TILE = 1024
MOE_SUB = 256


def _route_kernel(x_ref, g_ref, mod_ref, router_ref, h_o, info_o, cnt_o, cnt_s, *, d, n_exp):
    i = pl.program_id(0)

    @pl.when(i == 0)
    def _():
        cnt_s[...] = jnp.zeros_like(cnt_s)

    mod = mod_ref[...]
    h = _rms(x_ref[...]) * g_ref[...]
    h = h * (1.0 + mod[:, 4 * d:5 * d]) + mod[:, 3 * d:4 * d]
    h_o[...] = h

    logits = _mm3(h, router_ref[...])
    tm = logits.shape[0]
    lane = lax.broadcasted_iota(jnp.int32, logits.shape, 1)
    lg = jnp.where(lane < n_exp, logits, -jnp.inf)
    m1 = jnp.max(lg, axis=-1, keepdims=True)
    i1 = jnp.min(jnp.where(lg == m1, lane, LANE), axis=-1, keepdims=True)
    lg2 = jnp.where(lane == i1, -jnp.inf, lg)
    m2 = jnp.max(lg2, axis=-1, keepdims=True)
    i2 = jnp.min(jnp.where(lg2 == m2, lane, LANE), axis=-1, keepdims=True)
    w1 = 1.0 / (1.0 + jnp.exp(m2 - m1))

    sel1 = lane == i1
    sel2 = lane == i2
    member = jnp.where(sel1 | sel2, 1.0, 0.0)
    ri = lax.broadcasted_iota(jnp.int32, (tm, tm), 0)
    ci = lax.broadcasted_iota(jnp.int32, (tm, tm), 1)
    before = jnp.where(ri > ci, 1.0, 0.0)
    rank = cnt_s[...] + _mm(before, member)
    r1 = jnp.sum(jnp.where(sel1, rank, 0.0), axis=-1, keepdims=True)
    r2 = jnp.sum(jnp.where(sel2, rank, 0.0), axis=-1, keepdims=True)
    cnt_s[...] += jnp.sum(member, axis=0, keepdims=True)
    cnt_o[...] = jnp.broadcast_to(cnt_s[...], cnt_o.shape)

    info = jnp.zeros(logits.shape, F32)
    for ln, val in ((INFO_E, i1.astype(F32)), (INFO_E + 1, i2.astype(F32)), (INFO_W, w1),
                    (INFO_W + 1, 1.0 - w1), (INFO_R, r1), (INFO_R + 1, r2)):
        info = jnp.where(lane == ln, val, info)
    info_o[...] = info


def _route(x, gain, mod, mod_row, router, tm):
    t, d = x.shape
    n_exp = router.shape[1]
    router_p = jnp.zeros((d, LANE), F32).at[:, :n_exp].set(router)
    kern = functools.partial(_route_kernel, d=d, n_exp=n_exp)
    return pl.pallas_call(
        kern,
        out_shape=(jax.ShapeDtypeStruct((t, d), F32), jax.ShapeDtypeStruct((t, LANE), F32),
                   jax.ShapeDtypeStruct((SUBLANE, LANE), F32)),
        grid=(t // tm,),
        in_specs=[pl.BlockSpec((tm, d), lambda i: (i, 0)),
                  pl.BlockSpec((1, d), lambda i: (0, 0)),
                  pl.BlockSpec((None, 1, N_MOD * d), lambda i: (mod_row(i), 0, 0)),
                  pl.BlockSpec((d, LANE), lambda i: (0, 0))],
        out_specs=(pl.BlockSpec((tm, d), lambda i: (i, 0)), pl.BlockSpec((tm, LANE), lambda i: (i, 0)),
                   pl.BlockSpec((SUBLANE, LANE), lambda i: (0, 0))),
        scratch_shapes=[pltpu.VMEM((1, LANE), F32)],
        compiler_params=_cparams(("arbitrary",)),
        name="moe_route",
    )(x, gain, mod, router_p)


def _row_copy(src_hbm, row, dst, slot, sem):
    return pltpu.make_async_copy(src_hbm.at[pl.ds(row, 1), :], dst.at[pl.ds(slot, 1), :], sem)


def _experts_kernel(expert_sm, nvalid_sm, tok_ref, tok_next_ref, h_hbm, wg_ref, wu_ref, wd_ref, y_ref,
                    hbuf, h16, acc, wg16, wu16, wd16, sems, *, sub, n_f):
    del expert_sm
    i = pl.program_id(0)
    j = pl.program_id(1)
    n_i = pl.num_programs(0)
    tm = h16.shape[0]
    slot = i % 2
    nxt = 1 - slot
    nv = nvalid_sm[i]
    n_sub = (nv + sub - 1) // sub
    has_next = jnp.where(i + 1 < n_i, nvalid_sm[jnp.minimum(i + 1, n_i - 1)], 0) > 0
    per_step = -(-tm // n_f)
    base = j * per_step

    def start_row(tok, r, buf):
        _row_copy(h_hbm, tok[0, r], hbuf.at[buf], r, sems.at[buf]).start()

    def fetch_loop(tok, lo, hi, buf):
        def issue(r, carry):
            start_row(tok, r, buf)
            return carry

        lax.fori_loop(lo, hi, issue, 0)

    @pl.when(j == 0)
    def _():
        @pl.when((i == 0) & (nv > 0))
        def _():
            fetch_loop(tok_ref, 0, tm, 0)

        acc[...] = jnp.zeros_like(acc)

        @pl.when(nv > 0)
        def _():
            pltpu.make_async_copy(h_hbm.at[pl.ds(0, tm), :], hbuf.at[slot], sems.at[slot]).wait()

            def cast(s, carry):
                rows = pl.ds(pl.multiple_of(s * sub, sub), sub)
                h16[rows, :] = hbuf[slot, rows, :].astype(BF16)
                return carry

            lax.fori_loop(0, n_sub, cast, 0)

    fast = (n_sub == tm // sub) & has_next

    @pl.when(fast)
    def _():
        n_safe = tm - (n_f - 1) * per_step
        for r in range(min(per_step, n_safe)):
            start_row(tok_next_ref, base + r, nxt)
        h = h16[...]
        a = _silu(_dot(h, wg_ref[...].astype(BF16))) * _dot(h, wu_ref[...].astype(BF16))
        acc[...] += _dot(a.astype(BF16), wd_ref[...].astype(BF16))
        for r in range(n_safe, per_step):
            pl.when(base + r < tm)(functools.partial(start_row, tok_next_ref, base + r, nxt))

    @pl.when(jnp.logical_not(fast))
    def _():
        @pl.when(has_next)
        def _():
            fetch_loop(tok_next_ref, base, jnp.minimum(base + per_step, tm), nxt)

        @pl.when(n_sub > 0)
        def _():
            wg16[...] = wg_ref[...].astype(BF16)
            wu16[...] = wu_ref[...].astype(BF16)
            wd16[...] = wd_ref[...].astype(BF16)

            def block(s, carry):
                rows = pl.ds(pl.multiple_of(s * sub, sub), sub)
                h = h16[rows, :]
                a = _silu(_dot(h, wg16[...])) * _dot(h, wu16[...])
                acc[rows, :] += _dot(a.astype(BF16), wd16[...])
                return carry

            lax.fori_loop(0, n_sub, block, 0)

    @pl.when(j == n_f - 1)
    def _():
        y_ref[...] = acc[...]


def _experts(h, src_tok, tile_expert, tile_nvalid, wg, wu, wd, tm, tf, sub):
    t, d = h.shape
    n_exp, _, ff = wg.shape
    n_tiles = tile_expert.shape[0]
    n_f = ff // tf

    def w_idx(i, j, expert_sm, nvalid_sm):
        return expert_sm[i], jnp.where(nvalid_sm[i] > 0, j, n_f - 1)

    def wgu_map(i, j, expert_sm, nvalid_sm):
        e, jj = w_idx(i, j, expert_sm, nvalid_sm)
        return e, 0, jj

    def wd_map(i, j, expert_sm, nvalid_sm):
        e, jj = w_idx(i, j, expert_sm, nvalid_sm)
        return e, jj, 0

    kern = functools.partial(_experts_kernel, sub=sub, n_f=n_f)
    tok = src_tok.reshape(n_tiles, 1, tm)
    grid_spec = pltpu.PrefetchScalarGridSpec(
        num_scalar_prefetch=2,
        grid=(n_tiles, n_f),
        in_specs=[pl.BlockSpec((None, 1, tm), lambda i, j, *_: (i, 0, 0), memory_space=pltpu.SMEM),
                  pl.BlockSpec((None, 1, tm), lambda i, j, *_: (jnp.minimum(i + 1, n_tiles - 1), 0, 0),
                               memory_space=pltpu.SMEM),
                  pl.BlockSpec(memory_space=pl.ANY),
                  pl.BlockSpec((None, d, tf), wgu_map),
                  pl.BlockSpec((None, d, tf), wgu_map),
                  pl.BlockSpec((None, tf, d), wd_map)],
        out_specs=pl.BlockSpec((tm, d), lambda i, j, *_: (i, 0)),
        scratch_shapes=[pltpu.VMEM((2, tm, d), F32), pltpu.VMEM((tm, d), BF16), pltpu.VMEM((tm, d), F32),
                        pltpu.VMEM((d, tf), BF16), pltpu.VMEM((d, tf), BF16), pltpu.VMEM((tf, d), BF16),
                        pltpu.SemaphoreType.DMA((2,))])
    return pl.pallas_call(
        kern,
        out_shape=jax.ShapeDtypeStruct((n_tiles * tm, d), F32),
        grid_spec=grid_spec,
        compiler_params=_cparams(("arbitrary", "arbitrary")),
        name="moe_experts",
    )(tile_expert, tile_nvalid, tok, tok, h, wg, wu, wd)


def _combine_kernel(*refs, d, final_norm, split_tiles):
    pos_ref, pos_next_ref, info_ref, x_ref, mod_ref = refs[:5]
    p = 5
    if final_norm:
        fg_ref = refs[p]
        p += 1
    y_hbm = refs[p]
    o_refs = refs[p + 1:-2]
    ybuf, sems = refs[-2:]
    tm = x_ref.shape[0]
    i = pl.program_id(0)
    slot = i % 2

    def fetch(pos, buf):
        def issue(r, carry):
            for k in range(TOP_K):
                _row_copy(y_hbm, pos[0, k * tm + r], ybuf.at[buf, k], r, sems.at[buf]).start(priority=k % 2)
            return carry

        lax.fori_loop(0, tm, issue, 0, unroll=8)

    @pl.when(i == 0)
    def _():
        fetch(pos_ref, 0)

    @pl.when(i + 1 < pl.num_programs(0))
    def _():
        fetch(pos_next_ref, 1 - slot)

    for k in range(TOP_K):
        pltpu.make_async_copy(y_hbm.at[pl.ds(0, tm), :], ybuf.at[slot, k], sems.at[slot]).wait()

    info = info_ref[...]
    f = None
    for k in range(TOP_K):
        term = info[:, INFO_W + k:INFO_W + k + 1] * ybuf[slot, k]
        f = term if f is None else f + term
    o = x_ref[...] + mod_ref[:, 5 * d:6 * d] * f
    if final_norm:
        o = _rms(o) * fg_ref[...]
    if split_tiles is None:
        o_refs[0][...] = o
    else:
        @pl.when(i < split_tiles)
        def _():
            o_refs[0][...] = o

        @pl.when(i >= split_tiles)
        def _():
            o_refs[1][...] = o


def _combine(x, info, pos, y, mod, mod_row, final_gain, tm, split_rows):
    t, d = x.shape
    final_norm = final_gain is not None
    split_tiles = None if split_rows is None else split_rows // tm
    kern = functools.partial(_combine_kernel, d=d, final_norm=final_norm, split_tiles=split_tiles)
    n_tiles = t // tm
    if split_tiles is None:
        out_shape = jax.ShapeDtypeStruct((t, d), F32)
        out_specs = pl.BlockSpec((tm, d), lambda i: (i, 0))
    else:
        out_shape = (jax.ShapeDtypeStruct((split_rows, d), F32), jax.ShapeDtypeStruct((t - split_rows, d), F32))
        out_specs = (pl.BlockSpec((tm, d), lambda i: (jnp.minimum(i, split_tiles - 1), 0)),
                     pl.BlockSpec((tm, d), lambda i: (jnp.maximum(i - split_tiles, 0), 0)))
    in_specs = [pl.BlockSpec((None, 1, TOP_K * tm), lambda i: (i, 0, 0), memory_space=pltpu.SMEM),
                pl.BlockSpec((None, 1, TOP_K * tm), lambda i: (jnp.minimum(i + 1, n_tiles - 1), 0, 0),
                             memory_space=pltpu.SMEM),
                pl.BlockSpec((tm, LANE), lambda i: (i, 0)),
                pl.BlockSpec((tm, d), lambda i: (i, 0)),
                pl.BlockSpec((None, 1, N_MOD * d), lambda i: (mod_row(i), 0, 0))]
    args = [pos, pos, info, x, mod]
    if final_norm:
        in_specs.append(pl.BlockSpec((1, d), lambda i: (0, 0)))
        args.append(final_gain)
    in_specs.append(pl.BlockSpec(memory_space=pl.ANY))
    args.append(y)
    return pl.pallas_call(
        kern,
        out_shape=out_shape,
        grid=(t // tm,),
        in_specs=in_specs,
        out_specs=out_specs,
        scratch_shapes=[pltpu.VMEM((2, TOP_K, tm, d), F32), pltpu.SemaphoreType.DMA((2,))],
        compiler_params=_cparams(("arbitrary",)),
        name="moe_combine",
    )(*args)


def _moe(x, gain, mod, mod_row_fn, router, wg, wu, wd, final_gain, split_rows, tm_route, tm_exp, tm_comb, tf, sub):
    t, d = x.shape
    n_exp = wg.shape[0]
    h, info, cnt = _route(x, gain, mod, mod_row_fn(tm_route), router, tm_route)

    counts = cnt[0, :n_exp].astype(jnp.int32)
    tiles_per = (counts + tm_exp - 1) // tm_exp
    tile_end = jnp.cumsum(tiles_per)
    tile_start = tile_end - tiles_per
    n_tiles = (TOP_K * t) // tm_exp + n_exp
    tile_id = jnp.arange(n_tiles, dtype=jnp.int32)
    tile_expert = jnp.minimum(jnp.sum(tile_id[:, None] >= tile_end[None, :], axis=1), n_exp - 1).astype(jnp.int32)
    tile_nvalid = jnp.where(tile_id < tile_end[n_exp - 1],
                            jnp.clip(counts[tile_expert] - (tile_id - tile_start[tile_expert]) * tm_exp, 0, tm_exp),
                            0).astype(jnp.int32)
    slot_start = tile_start * tm_exp
    experts = info[:, INFO_E:INFO_E + TOP_K].astype(jnp.int32)
    slots = slot_start[experts] + info[:, INFO_R:INFO_R + TOP_K].astype(jnp.int32)
    token = jnp.broadcast_to(jnp.arange(t, dtype=jnp.int32)[:, None], slots.shape)
    src_tok = jnp.zeros((n_tiles * tm_exp,), jnp.int32).at[slots.reshape(-1)].set(
        token.reshape(-1), unique_indices=True)

    y = _experts(h, src_tok, tile_expert, tile_nvalid, wg, wu, wd, tm_exp, tf, sub)
    pos = slots.reshape(t // tm_comb, tm_comb, TOP_K).transpose(0, 2, 1).reshape(t // tm_comb, 1, TOP_K * tm_comb)
    return _combine(x, info, pos, y, mod, mod_row_fn(tm_comb), final_gain, tm_comb, split_rows)


def _lane_row(vals, offset):
    n = vals.shape[0]
    return jnp.zeros((1, LANE), F32).at[0, offset:offset + n].set(vals.astype(F32))


def kernel(x_prompt, x_sample, state_delta, c, c_ctx, norm1, norm2, w_mod, b_mod, w_in, sc_w, sc_b, dn_a_log, dn_dt_bias, dn_norm, cv_dw_w, cv_dw_b, cv_ln_g, cv_ln_b, w_out, ffn_wg, ffn_wu, ffn_wd, moe_router, moe_wg, moe_wu, moe_wd, final_norm):
    b_ctx, l_ctx, d = x_prompt.shape
    b_lat, l_lat, _ = x_sample.shape
    depth = w_in.shape[0]
    t_ctx = b_ctx * l_ctx
    t_lat = b_lat * l_lat
    f_w = d // 4
    f_gw = f_w // F_GROUPS
    dn_w = d // 2
    dk = dn_w // DN_HEADS
    cv_w = d // 4
    qkvz_w = 4 * dn_w
    n_gate = 2 * N_DIRS * DN_HEADS
    n_exp = moe_wg.shape[1]
    assert dk == LANE and t_ctx % l_lat == 0 and b_lat + 1 <= MOD_ROWS
    assert l_ctx % CHUNK == 0 and l_lat % CHUNK == 0 and l_lat % GRID_W == 0 and CHUNK == GRID_W

    tm = _pick_tile(512, t_ctx, l_lat)

    def mod_row_fn(tile):
        n_ctx_tiles = t_ctx // tile
        per_seq = l_lat // tile
        return lambda i: jnp.where(i < n_ctx_tiles, 0, 1 + (i - n_ctx_tiles) // per_seq)

    x = (x_prompt.reshape(t_ctx, d), x_sample.reshape(t_lat, d))
    cvec = jnp.zeros((MOD_ROWS, d), F32).at[0].set(c_ctx).at[1:1 + b_lat].set(c)
    mods = _adaln(cvec, w_mod, b_mod).reshape(depth, MOD_ROWS, 1, N_MOD * d)

    cg, sg = _dft_tables(f_gw)
    eye_g = np.eye(F_GROUPS)
    ccs = jnp.asarray(np.concatenate([np.kron(eye_g, cg), np.kron(eye_g, sg)], axis=1), F32).astype(BF16)

    passes = ((l_ctx, b_ctx, 0, False), (l_lat, b_lat, t_ctx, True))
    o_q = f_w
    o_ba = f_w + qkvz_w
    o_cv = o_ba + n_gate
    new_states = []
    for l in range(depth):
        w = w_in[l]
        w_perm = jnp.concatenate(
            [w[:, :o_ba], w[:, o_cv:o_cv + 2 * cv_w], w[:, o_ba:o_cv], jnp.zeros((d, LANE - n_gate), F32)],
            axis=1).astype(BF16)
        alog = _lane_row(dn_a_log[l].reshape(-1), N_DIRS * DN_HEADS)
        dtb = _lane_row(dn_dt_bias[l].reshape(-1), N_DIRS * DN_HEADS)
        xc, xs, qkvz, cv, gates = _proj(x, norm1[l][None], mods[l], w_perm, ccs, alog, dtb,
                                        mod_row_fn(tm), tm, t_ctx // tm, f_w, qkvz_w, 2 * cv_w)
        g16 = gates[:, :n_gate].reshape(-1, 2 * N_DIRS, DN_HEADS)
        gcol = g16.transpose(2, 0, 1)
        grow = g16.reshape(-1, CHUNK, 2 * N_DIRS, DN_HEADS).transpose(3, 0, 2, 1)

        parts = [[], [], []]
        for seq_len, n_batch, row_off, latent in passes:
            parts[0].append(_fourier(xc, xs, seq_len, n_batch, row_off, f_gw))
            res = _delta(qkvz, gcol, grow, sc_w[l], sc_b[l][None], dn_norm[l][None],
                         state_delta if latent else None, l, seq_len, n_batch, row_off, dk,
                         write_state=not latent)
            if latent:
                parts[1].append(res)
            else:
                parts[1].append(res[0])
                new_states.append(res[1])
            parts[2].append(_conformer(cv, cv_dw_w[l], cv_dw_b[l][None], cv_ln_g[l][None],
                                       cv_ln_b[l][None], seq_len, n_batch, row_off, latent))
        x = _outproj(parts, w_out[l].astype(BF16), x, mods[l], mod_row_fn(tm), tm, t_ctx // tm)

        fg = final_norm[None] if l == depth - 1 else None
        if l % 2 == 0:
            i = l // 2
            ff = ffn_wg.shape[-1]
            tf = max(t for t in range(LANE, FFN_TILE_MAX + 1, LANE) if ff % t == 0)
            x = _ffn(x, norm2[l][None], mods[l], mod_row_fn(tm), ffn_wg[i].astype(BF16), ffn_wu[i].astype(BF16),
                     ffn_wd[i].astype(BF16), fg, tm, tf)
        else:
            i = l // 2
            x = _moe(x, norm2[l][None], mods[l], mod_row_fn, moe_router[i], moe_wg[i], moe_wu[i], moe_wd[i], fg,
                     split_rows=t_ctx if l == depth - 1 else None,
                     tm_route=tm, tm_exp=MOE_TILE, tm_comb=_pick_tile(256, t_ctx, l_lat),
                     tf=_pick_tile(512, moe_wg.shape[-1]), sub=MOE_SUB)

    y_ctx, y_lat = x if isinstance(x, tuple) else (x[:t_ctx], x[t_ctx:])
    y_prompt = y_ctx.reshape(b_ctx, l_ctx, d)
    y_sample = y_lat.reshape(b_lat, l_lat, d)
    return y_prompt, y_sample, jnp.stack(new_states, axis=1)
```
